```python
import math
import jax, jax.numpy as jnp
from jax import lax
import numpy as np

D_MODEL = 1024
BATCH = 8
SEQ = 2048
DEPTH = 2
DEC_BATCH = 128
DEC_SEQ = 4
PAST_LEN = 16384
PAGE_SIZE = 128

MIX_W = D_MODEL
D_RNN = MIX_W // 2
RG_HEADS = 8
RG_HEAD_DIM = D_RNN // RG_HEADS
RG_CONV = 4
RG_C = 8.0
CF_W = MIX_W // 4
CF_CONV = 31
S5_W = MIX_W - D_RNN - CF_W
S5_GROUP_CH = 16
S5_GROUPS = S5_W // S5_GROUP_CH
S5_STATE = 64
P_IN = 2 * D_RNN + 2 * CF_W + S5_W
N_GROUPS = 4
EXP_PER_GROUP = 4
N_EXPERTS = N_GROUPS * EXP_PER_GROUP
TOP_K = 2
EXPERT_FF = D_MODEL // 4
EPS = 1e-6

kernel_name = 'hymba_style_rglru_conformer_s5_hmoe_step'


def rmsnorm(x, g):
    xf = x.astype(jnp.float32)
    y = xf * lax.rsqrt(jnp.mean(xf * xf, axis=-1, keepdims=True) + EPS)
    return (y * g.astype(jnp.float32)).astype(x.dtype)


def causal_dwconv(x, buf, w, b):
    xp = jnp.concatenate([buf.astype(x.dtype), x], axis=1)
    y = lax.conv_general_dilated(xp, w[:, None, :].astype(x.dtype), window_strides=(1,),
                                 padding='VALID', dimension_numbers=('NWC', 'WIO', 'NWC'),
                                 feature_group_count=x.shape[-1])
    return y + b, xp[:, xp.shape[1] - (w.shape[0] - 1):]


def linear_scan(a, b, h0):
    def comb(l, r):
        return (l[0] * r[0], r[0] * l[1] + r[1])
    a_cum, b_cum = lax.associative_scan(comb, (a, b), axis=1)
    return a_cum * h0[:, None] + b_cum


def complex_linear_scan(a_re, a_im, b_re, b_im, h0_re, h0_im):
    def comb(l, r):
        lar, lai, lbr, lbi = l
        rar, rai, rbr, rbi = r
        return (lar * rar - lai * rai, lar * rai + lai * rar,
                rar * lbr - rai * lbi + rbr, rar * lbi + rai * lbr + rbi)
    ar, ai, br, bi = lax.associative_scan(comb, (a_re, a_im, b_re, b_im), axis=1)
    h0r, h0i = h0_re[:, None], h0_im[:, None]
    return ar * h0r - ai * h0i + br, ar * h0i + ai * h0r + bi


def rglru_branch(xb, yb, h0, cbuf, conv_w, conv_b, wa, ba, wx, bx, lam):
    B, T, _ = xb.shape
    xc, cbuf_new = causal_dwconv(xb, cbuf, conv_w, conv_b)
    xh = xc.reshape(B, T, RG_HEADS, RG_HEAD_DIM)
    r = jax.nn.sigmoid(jnp.einsum('bthi,hij->bthj', xh, wa).reshape(B, T, D_RNN) + ba)
    i = jax.nn.sigmoid(jnp.einsum('bthi,hij->bthj', xh, wx).reshape(B, T, D_RNN) + bx)
    log_a = -RG_C * r * jax.nn.softplus(-lam)
    a = jnp.exp(log_a)
    mult = jnp.sqrt(-jnp.expm1(2.0 * log_a))
    h = linear_scan(a, mult * (i * xc), h0)
    return h * jax.nn.gelu(yb), h[:, -1], cbuf_new


def conformer_conv_branch(ca, cg, fbuf, w, b, ln_g, ln_b):
    u = ca * jax.nn.sigmoid(cg)
    v, fbuf_new = causal_dwconv(u, fbuf, w, b)
    vf = v.astype(jnp.float32)
    mu = jnp.mean(vf, axis=-1, keepdims=True)
    var = jnp.mean(jnp.square(vf - mu), axis=-1, keepdims=True)
    vn = ((vf - mu) * lax.rsqrt(var + EPS) * ln_g.astype(jnp.float32) + ln_b.astype(jnp.float32)).astype(v.dtype)
    return jax.nn.silu(vn), fbuf_new


def s5_branch(u, s_re, s_im, a_re, a_im, log_dt, b_re, b_im, c_re, c_im, d, w_glu, b_glu):
    B, T, _ = u.shape
    ug = u.reshape(B, T, S5_GROUPS, S5_GROUP_CH)
    dt = jnp.exp(log_dt)[:, None]
    mag = jnp.exp(a_re * dt)
    ab_re, ab_im = mag * jnp.cos(a_im * dt), mag * jnp.sin(a_im * dt)
    den = a_re * a_re + a_im * a_im
    nr, ni = ab_re - 1.0, ab_im
    k_re = (nr * a_re + ni * a_im) / den
    k_im = (ni * a_re - nr * a_im) / den
    bb_re = k_re[..., None] * b_re - k_im[..., None] * b_im
    bb_im = k_re[..., None] * b_im + k_im[..., None] * b_re
    bu_re = jnp.einsum('btgh,gph->btgp', ug, bb_re)
    bu_im = jnp.einsum('btgh,gph->btgp', ug, bb_im)
    x_re, x_im = complex_linear_scan(jnp.broadcast_to(ab_re, bu_re.shape), jnp.broadcast_to(ab_im, bu_im.shape),
                                     bu_re, bu_im, s_re, s_im)
    y = jnp.einsum('btgp,ghp->btgh', x_re, c_re) - jnp.einsum('btgp,ghp->btgh', x_im, c_im)
    y = y.reshape(B, T, S5_W) + d * u
    z = jax.nn.gelu(y)
    return z * jax.nn.sigmoid(z @ w_glu + b_glu), x_re[:, -1], x_im[:, -1]


def hier_moe(h, w_grp, b_grp, w_exp, b_exp, w_gate, w_up, w_down):
    B, T, D = h.shape
    t = h.reshape(B * T, D)
    gl = (t @ w_grp + b_grp).astype(jnp.float32)
    pg = jax.nn.softmax(gl, axis=-1)
    g_oh = jax.nn.one_hot(jnp.argmax(gl, axis=-1), N_GROUPS, dtype=jnp.float32)
    p_sel = jnp.sum(pg * g_oh, axis=-1)
    el = (jnp.einsum('td,dge->tge', t, w_exp) + b_exp).astype(jnp.float32)
    el_sel = jnp.einsum('tge,tg->te', el, g_oh)
    vals, idx = lax.top_k(el_sel, TOP_K)
    w2 = jax.nn.softmax(vals, axis=-1)
    e_gate = jnp.sum(w2[..., None] * jax.nn.one_hot(idx, EXP_PER_GROUP, dtype=jnp.float32), axis=1)
    gate = (g_oh[:, :, None] * e_gate[:, None, :] * p_sel[:, None, None]).reshape(B * T, N_EXPERTS).astype(t.dtype)
    hg = jnp.einsum('td,edf->tef', t, w_gate)
    hu = jnp.einsum('td,edf->tef', t, w_up)
    act = jax.nn.silu(hg) * hu * gate[..., None]
    return jnp.einsum('tef,efd->td', act, w_down).reshape(B, T, D)


def layer(x, st, lw):
    h0, rbuf, fbuf, s_re, s_im = st
    hn = rmsnorm(x, lw['norm1'])
    proj = hn @ lw['w_in']
    o = (0, D_RNN, 2 * D_RNN, 2 * D_RNN + CF_W, 2 * D_RNN + 2 * CF_W, P_IN)
    xb, yb, ca, cg, su = [proj[..., o[j]:o[j + 1]] for j in range(5)]
    out_a, h_last, rbuf_new = rglru_branch(xb, yb, h0, rbuf, lw['rg_conv_w'], lw['rg_conv_b'], lw['rg_wa'],
                                           lw['rg_ba'], lw['rg_wx'], lw['rg_bx'], lw['rg_lambda'])
    out_b, fbuf_new = conformer_conv_branch(ca, cg, fbuf, lw['cf_conv_w'], lw['cf_conv_b'], lw['cf_ln_g'], lw['cf_ln_b'])
    out_c, sre_new, sim_new = s5_branch(su, s_re, s_im, lw['s5_a_re'], lw['s5_a_im'], lw['s5_log_dt'],
                                        lw['s5_b_re'], lw['s5_b_im'], lw['s5_c_re'], lw['s5_c_im'],
                                        lw['s5_d'], lw['s5_w_glu'], lw['s5_b_glu'])
    mix = jnp.concatenate([rmsnorm(out_a, lw['gn_rg']), rmsnorm(out_b, lw['gn_cf']), rmsnorm(out_c, lw['gn_s5'])], axis=-1)
    x = x + mix @ lw['w_out']
    x = x + hier_moe(rmsnorm(x, lw['norm2']), lw['moe_w_grp'], lw['moe_b_grp'], lw['moe_w_exp'], lw['moe_b_exp'],
                     lw['moe_w_gate'], lw['moe_w_up'], lw['moe_w_down'])
    return x, (h_last, rbuf_new, fbuf_new, sre_new, sim_new)


def setup_inputs(seed: int = 0) -> dict:
    key = jax.random.key(seed)
    ks = iter(jax.random.split(key, 64))
    f32 = jnp.float32
    L = DEPTH

    def nrm(shape, scale):
        return jax.random.normal(next(ks), shape, f32) * scale

    def gain(shape):
        return 1.0 + nrm(shape, 0.02)

    u_a = jax.random.uniform(next(ks), (L, D_RNN), f32, 0.9, 0.999)
    log_dt = jax.random.uniform(next(ks), (L, S5_GROUPS), f32, math.log(1e-3), math.log(1e-1))
    a_im0 = jnp.pi * jnp.arange(S5_STATE, dtype=f32)[None, None, :]
    return {
        'x_prompt': nrm((BATCH, SEQ, D_MODEL), 1.0),
        'x_sample': nrm((DEC_BATCH, DEC_SEQ, D_MODEL), 1.0),
        'state_rglru_h': nrm((L, DEC_BATCH, D_RNN), 0.5),
        'state_rglru_conv': nrm((L, DEC_BATCH, RG_CONV - 1, D_RNN), 1.0),
        'state_conf_conv': nrm((L, DEC_BATCH, CF_CONV - 1, CF_W), 0.5),
        'state_s5_re': nrm((L, DEC_BATCH, S5_GROUPS, S5_STATE), 0.1),
        'state_s5_im': nrm((L, DEC_BATCH, S5_GROUPS, S5_STATE), 0.1),
        'norm1': gain((L, D_MODEL)),
        'w_in': nrm((L, D_MODEL, P_IN), D_MODEL ** -0.5),
        'rg_conv_w': nrm((L, RG_CONV, D_RNN), RG_CONV ** -0.5),
        'rg_conv_b': nrm((L, D_RNN), 0.01),
        'rg_wa': nrm((L, RG_HEADS, RG_HEAD_DIM, RG_HEAD_DIM), RG_HEAD_DIM ** -0.5),
        'rg_ba': nrm((L, D_RNN), 0.01),
        'rg_wx': nrm((L, RG_HEADS, RG_HEAD_DIM, RG_HEAD_DIM), RG_HEAD_DIM ** -0.5),
        'rg_bx': nrm((L, D_RNN), 0.01),
        'rg_lambda': jnp.log(u_a / (1.0 - u_a)),
        'cf_conv_w': nrm((L, CF_CONV, CF_W), CF_CONV ** -0.5),
        'cf_conv_b': nrm((L, CF_W), 0.01),
        'cf_ln_g': gain((L, CF_W)),
        'cf_ln_b': nrm((L, CF_W), 0.01),
        's5_a_re': -0.5 + nrm((L, S5_GROUPS, S5_STATE), 0.01),
        's5_a_im': a_im0 + nrm((L, S5_GROUPS, S5_STATE), 0.01),
        's5_log_dt': log_dt,
        's5_b_re': nrm((L, S5_GROUPS, S5_STATE, S5_GROUP_CH), (2 * S5_GROUP_CH) ** -0.5),
        's5_b_im': nrm((L, S5_GROUPS, S5_STATE, S5_GROUP_CH), (2 * S5_GROUP_CH) ** -0.5),
        's5_c_re': nrm((L, S5_GROUPS, S5_GROUP_CH, S5_STATE), (2 * S5_STATE) ** -0.5),
        's5_c_im': nrm((L, S5_GROUPS, S5_GROUP_CH, S5_STATE), (2 * S5_STATE) ** -0.5),
        's5_d': nrm((L, S5_W), 1.0),
        's5_w_glu': nrm((L, S5_W, S5_W), S5_W ** -0.5),
        's5_b_glu': nrm((L, S5_W), 0.01),
        'gn_rg': gain((L, D_RNN)),
        'gn_cf': gain((L, CF_W)),
        'gn_s5': gain((L, S5_W)),
        'w_out': nrm((L, MIX_W, D_MODEL), MIX_W ** -0.5),
        'norm2': gain((L, D_MODEL)),
        'moe_w_grp': nrm((L, D_MODEL, N_GROUPS), D_MODEL ** -0.5),
        'moe_b_grp': nrm((L, N_GROUPS), 0.01),
        'moe_w_exp': nrm((L, D_MODEL, N_GROUPS, EXP_PER_GROUP), D_MODEL ** -0.5),
        'moe_b_exp': nrm((L, N_GROUPS, EXP_PER_GROUP), 0.01),
        'moe_w_gate': nrm((L, N_EXPERTS, D_MODEL, EXPERT_FF), D_MODEL ** -0.5),
        'moe_w_up': nrm((L, N_EXPERTS, D_MODEL, EXPERT_FF), D_MODEL ** -0.5),
        'moe_w_down': nrm((L, N_EXPERTS, EXPERT_FF, D_MODEL), EXPERT_FF ** -0.5),
        'final_norm': gain((D_MODEL,)),
    }


def reference(x_prompt, x_sample, state_rglru_h, state_rglru_conv, state_conf_conv, state_s5_re, state_s5_im,
              norm1, w_in, rg_conv_w, rg_conv_b, rg_wa, rg_ba, rg_wx, rg_bx, rg_lambda,
              cf_conv_w, cf_conv_b, cf_ln_g, cf_ln_b,
              s5_a_re, s5_a_im, s5_log_dt, s5_b_re, s5_b_im, s5_c_re, s5_c_im, s5_d, s5_w_glu, s5_b_glu,
              gn_rg, gn_cf, gn_s5, w_out, norm2,
              moe_w_grp, moe_b_grp, moe_w_exp, moe_b_exp, moe_w_gate, moe_w_up, moe_w_down, final_norm):
    W = dict(norm1=norm1, w_in=w_in, rg_conv_w=rg_conv_w, rg_conv_b=rg_conv_b, rg_wa=rg_wa, rg_ba=rg_ba,
             rg_wx=rg_wx, rg_bx=rg_bx, rg_lambda=rg_lambda, cf_conv_w=cf_conv_w, cf_conv_b=cf_conv_b,
             cf_ln_g=cf_ln_g, cf_ln_b=cf_ln_b, s5_a_re=s5_a_re, s5_a_im=s5_a_im, s5_log_dt=s5_log_dt,
             s5_b_re=s5_b_re, s5_b_im=s5_b_im, s5_c_re=s5_c_re, s5_c_im=s5_c_im, s5_d=s5_d,
             s5_w_glu=s5_w_glu, s5_b_glu=s5_b_glu, gn_rg=gn_rg, gn_cf=gn_cf, gn_s5=gn_s5, w_out=w_out,
             norm2=norm2, moe_w_grp=moe_w_grp, moe_b_grp=moe_b_grp, moe_w_exp=moe_w_exp,
             moe_b_exp=moe_b_exp, moe_w_gate=moe_w_gate, moe_w_up=moe_w_up, moe_w_down=moe_w_down)
    dt = x_prompt.dtype
    xp, xs = x_prompt, x_sample
    p_new, s_new = [], []
    for l in range(DEPTH):
        lw = {k: v[l] for k, v in W.items()}
        st_p = (jnp.zeros((BATCH, D_RNN), dt), jnp.zeros((BATCH, RG_CONV - 1, D_RNN), dt),
                jnp.zeros((BATCH, CF_CONV - 1, CF_W), dt), jnp.zeros((BATCH, S5_GROUPS, S5_STATE), dt),
                jnp.zeros((BATCH, S5_GROUPS, S5_STATE), dt))
        st_s = (state_rglru_h[l], state_rglru_conv[l], state_conf_conv[l], state_s5_re[l], state_s5_im[l])
        xp, np_l = layer(xp, st_p, lw)
        xs, ns_l = layer(xs, st_s, lw)
        p_new.append(np_l)
        s_new.append(ns_l)
    p_h, p_rc, p_cf, p_sre, p_sim = [jnp.stack([e[j] for e in p_new], axis=0) for j in range(5)]
    s_h, s_rc, s_cf, s_sre, s_sim = [jnp.stack([e[j] for e in s_new], axis=0) for j in range(5)]
    y_prompt = rmsnorm(xp, final_norm)
    y_sample = rmsnorm(xs, final_norm)
    return (y_prompt, y_sample, p_h, p_rc, p_cf, p_sre, p_sim, s_h, s_rc, s_cf, s_sre, s_sim)
```

```python
import functools

import jax
import jax.numpy as jnp
from jax import lax
from jax.experimental import pallas as pl
from jax.experimental.pallas import tpu as pltpu

F32 = jnp.float32
BF16 = jnp.bfloat16

D_MODEL = 1024
D_RNN = 512
RG_HEADS = 8
RG_CONV = 4
RG_C = 8.0
CF_W = 256
CF_CONV = 31
S5_W = 256
S5_GROUPS = 16
S5_GROUP_CH = 16
S5_STATE = 64
S5_N = S5_GROUPS * S5_STATE
P_IN = 2 * D_RNN + 2 * CF_W + S5_W
N_GROUPS = 4
EXP_PER_GROUP = 4
N_EXPERTS = 16
EXPERT_FF = 256
EPS = 1e-6

ROUTER_LANES = 128
VMEM_LIMIT_BYTES = 60000 * 1024
ROW_BLOCK = 64
CHUNK_ROWS = 512


def _rms(x, g):
    return x * lax.rsqrt(jnp.mean(x * x, axis=-1, keepdims=True) + EPS) * g


def _softplus(z):
    return jnp.maximum(z, 0.0) + jnp.log1p(jnp.exp(-jnp.abs(z)))


def _row_blocks(rows, rb, fn):
    def body(i, carry):
        fn(pl.multiple_of(i * rb, rb))
        return carry
    lax.fori_loop(0, rows // rb, body, 0)


def _s5_prep_kernel(are_ref, aim_ref, ldt_ref, bre_ref, bim_ref,
                    abre_ref, abim_ref, bbre_ref, bbim_ref):
    a_re = are_ref[...]
    a_im = aim_ref[...]
    dt = jnp.exp(ldt_ref[...])
    mag = jnp.exp(a_re * dt)
    ab_re = mag * jnp.cos(a_im * dt)
    ab_im = mag * jnp.sin(a_im * dt)
    den = a_re * a_re + a_im * a_im
    nr = ab_re - 1.0
    ni = ab_im
    k_re = (nr * a_re + ni * a_im) / den
    k_im = (ni * a_re - nr * a_im) / den
    abre_ref[...] = ab_re
    abim_ref[...] = ab_im
    b_re = bre_ref[...]
    b_im = bim_ref[...]
    bbre_ref[...] = k_re[:, None, :] * b_re - k_im[:, None, :] * b_im
    bbim_ref[...] = k_re[:, None, :] * b_im + k_im[:, None, :] * b_re


def _s5_prep(a_re, a_im, log_dt, b_re_ghp, b_im_ghp):
    L = a_re.shape[0]
    gp = pl.BlockSpec((None, S5_GROUPS, S5_STATE), lambda l: (l, 0, 0))
    ghp = pl.BlockSpec((None, S5_GROUPS, S5_GROUP_CH, S5_STATE), lambda l: (l, 0, 0, 0))
    return pl.pallas_call(
        _s5_prep_kernel,
        grid=(L,),
        in_specs=[gp, gp, pl.BlockSpec((None, S5_GROUPS, 1), lambda l: (l, 0, 0)), ghp, ghp],
        out_specs=[gp, gp, ghp, ghp],
        out_shape=[jax.ShapeDtypeStruct((L, S5_GROUPS, S5_STATE), F32)] * 2
        + [jax.ShapeDtypeStruct((L, S5_GROUPS, S5_GROUP_CH, S5_STATE), F32)] * 2,
        name="s5_prep",
    )(a_re, a_im, log_dt[..., None], b_re_ghp, b_im_ghp)


def _mixer_kernel(bt, tc, n_chunks,
                  x_ref, h0_ref, rb_ref, fb_ref, sre_ref, sim_ref,
                  n1_ref, win_ref, rgw_ref, rgb_ref, wg_ref, bg_ref, lam_ref,
                  cfw_ref, cfb_ref, lng_ref, lnb_ref,
                  abre_ref, abim_ref, bbre_ref, bbim_ref, cre_ref, cim_ref, d_ref,
                  wglu_ref, bglu_ref, gnrg_ref, gncf_ref, gns5_ref, wout_ref,
                  x1_ref, hl_ref, rbn_ref, fbn_ref, sren_ref, simn_ref,
                  xpad, upad, h_sc, sre_sc, sim_sc, hn_sc, proj_sc, xc_sc, gate_sc,
                  a_sc, b_sc, v_sc, bre_sc, bim_sc, mix_sc):
    rows = tc * bt
    rb = ROW_BLOCK
    rg_hist = (RG_CONV - 1) * bt
    cf_hist = (CF_CONV - 1) * bt
    c = pl.program_id(0)

    @pl.when(c == 0)
    def _init():
        xpad[0:rg_hist, :] = rb_ref[...]
        upad[0:cf_hist, :] = fb_ref[...]
        h_sc[...] = h0_ref[...]
        sre_sc[...] = sre_ref[...]
        sim_sc[...] = sim_ref[...]

    def norm_in(r0):
        x = x_ref[pl.ds(r0, rb), :]
        hn_sc[pl.ds(r0, rb), :] = _rms(x, n1_ref[...]).astype(BF16)
    _row_blocks(rows, rb, norm_in)
    proj_sc[...] = jnp.dot(hn_sc[...], win_ref[...], preferred_element_type=F32)

    xpad[rg_hist:rg_hist + rows, :] = proj_sc[:, 0:D_RNN]

    def rg_conv(r0):
        acc = rgb_ref[...] + rgw_ref[0:1, :] * xpad[pl.ds(r0, rb), :]
        for k in range(1, RG_CONV):
            acc = acc + rgw_ref[k:k + 1, :] * xpad[pl.ds(r0 + k * bt, rb), :]
        xc_sc[pl.ds(r0, rb), :] = acc
    _row_blocks(rows, rb, rg_conv)

    gate_sc[...] = jnp.dot(xc_sc[...].astype(BF16), wg_ref[...],
                           preferred_element_type=F32) + bg_ref[...]

    def rg_coeffs(r0):
        g = gate_sc[pl.ds(r0, rb), :]
        r = jax.nn.sigmoid(g[:, 0:D_RNN])
        i = jax.nn.sigmoid(g[:, D_RNN:2 * D_RNN])
        log_a = -RG_C * r * _softplus(-lam_ref[...])
        a_sc[pl.ds(r0, rb), :] = jnp.exp(log_a)
        mult = jnp.sqrt(1.0 - jnp.exp(2.0 * log_a))
        b_sc[pl.ds(r0, rb), :] = mult * (i * xc_sc[pl.ds(r0, rb), :])
    _row_blocks(rows, rb, rg_coeffs)

    def rg_step(t, h):
        r0 = pl.multiple_of(t * bt, bt)
        h = a_sc[pl.ds(r0, bt), :] * h + b_sc[pl.ds(r0, bt), :]
        b_sc[pl.ds(r0, bt), :] = h
        return h
    h_last = lax.fori_loop(0, tc, rg_step, h_sc[...], unroll=min(tc, 8))
    h_sc[...] = h_last
    hl_ref[...] = h_last

    def rg_out(r0):
        h = b_sc[pl.ds(r0, rb), :]
        yb = proj_sc[pl.ds(r0, rb), D_RNN:2 * D_RNN]
        mix_sc[pl.ds(r0, rb), 0:D_RNN] = _rms(h * jax.nn.gelu(yb), gnrg_ref[...]).astype(BF16)
    _row_blocks(rows, rb, rg_out)

    rbn_ref[...] = xpad[rows:rows + rg_hist, :]
    if n_chunks > 1:
        xpad[0:rg_hist, :] = xpad[rows:rows + rg_hist, :]

    o_cf = 2 * D_RNN

    def cf_glu(r0):
        ca = proj_sc[pl.ds(r0, rb), o_cf:o_cf + CF_W]
        cg = proj_sc[pl.ds(r0, rb), o_cf + CF_W:o_cf + 2 * CF_W]
        upad[pl.ds(cf_hist + r0, rb), :] = ca * jax.nn.sigmoid(cg)
    _row_blocks(rows, rb, cf_glu)

    def cf_conv(r0):
        acc = cfb_ref[...] + cfw_ref[0:1, :] * upad[pl.ds(r0, rb), :]
        for k in range(1, CF_CONV):
            acc = acc + cfw_ref[k:k + 1, :] * upad[pl.ds(r0 + k * bt, rb), :]
        mu = jnp.mean(acc, axis=-1, keepdims=True)
        cen = acc - mu
        var = jnp.mean(cen * cen, axis=-1, keepdims=True)
        vn = cen * lax.rsqrt(var + EPS) * lng_ref[...] + lnb_ref[...]
        ob = vn * jax.nn.sigmoid(vn)
        mix_sc[pl.ds(r0, rb), D_RNN:D_RNN + CF_W] = _rms(ob, gncf_ref[...]).astype(BF16)
    _row_blocks(rows, rb, cf_conv)

    fbn_ref[...] = upad[rows:rows + cf_hist, :]
    if n_chunks > 1:
        upad[0:cf_hist, :] = upad[rows:rows + cf_hist, :]

    o_s5 = o_cf + 2 * CF_W
    su_bf = proj_sc[:, o_s5:o_s5 + S5_W].astype(BF16)
    bre_sc[...] = jnp.dot(su_bf, bbre_ref[...], preferred_element_type=F32)
    bim_sc[...] = jnp.dot(su_bf, bbim_ref[...], preferred_element_type=F32)

    a_r = jnp.broadcast_to(abre_ref[...], (bt, S5_N))
    a_i = jnp.broadcast_to(abim_ref[...], (bt, S5_N))

    def s5_step(t, carry):
        s_r, s_i = carry
        r0 = pl.multiple_of(t * bt, bt)
        n_r = a_r * s_r - a_i * s_i + bre_sc[pl.ds(r0, bt), :]
        n_i = a_r * s_i + a_i * s_r + bim_sc[pl.ds(r0, bt), :]
        bre_sc[pl.ds(r0, bt), :] = n_r
        bim_sc[pl.ds(r0, bt), :] = n_i
        return n_r, n_i
    s_r, s_i = lax.fori_loop(0, tc, s5_step, (sre_sc[...], sim_sc[...]), unroll=min(tc, 4))
    sre_sc[...] = s_r
    sim_sc[...] = s_i
    sren_ref[...] = s_r
    simn_ref[...] = s_i

    v_sc[...] = (jnp.dot(bre_sc[...].astype(BF16), cre_ref[...], preferred_element_type=F32)
                 - jnp.dot(bim_sc[...].astype(BF16), cim_ref[...], preferred_element_type=F32))

    def s5_out(r0):
        su = proj_sc[pl.ds(r0, rb), o_s5:o_s5 + S5_W]
        z = jax.nn.gelu(v_sc[pl.ds(r0, rb), :] + d_ref[...] * su)
        gl = jnp.dot(z.astype(BF16), wglu_ref[...], preferred_element_type=F32) + bglu_ref[...]
        oc = z * jax.nn.sigmoid(gl)
        mix_sc[pl.ds(r0, rb), D_RNN + CF_W:D_MODEL] = _rms(oc, gns5_ref[...]).astype(BF16)
    _row_blocks(rows, rb, s5_out)

    x1_ref[...] = x_ref[...] + jnp.dot(mix_sc[...], wout_ref[...], preferred_element_type=F32)


def _const_spec(shape):
    nd = len(shape)
    return pl.BlockSpec(shape, lambda c: (0,) * nd)


def _mixer(x_tm, bt, tc, state, w):
    n_rows = x_tm.shape[0]
    rows = tc * bt
    n_chunks = n_rows // rows
    assert n_chunks * rows == n_rows and rows % ROW_BLOCK == 0 and bt % 8 == 0
    assert n_chunks == 1 or tc >= CF_CONV - 1
    h0, rbuf, fbuf, s_re, s_im = state
    weights = [w["norm1"], w["w_in"], w["rg_conv_w"], w["rg_conv_b"], w["rg_wg"], w["rg_bg"],
               w["rg_lambda"], w["cf_conv_w"], w["cf_conv_b"], w["cf_ln_g"], w["cf_ln_b"],
               w["s5_ab_re"], w["s5_ab_im"], w["s5_bb_re"], w["s5_bb_im"], w["s5_c_re"], w["s5_c_im"],
               w["s5_d"], w["s5_w_glu"], w["s5_b_glu"], w["gn_rg"], w["gn_cf"], w["gn_s5"], w["w_out"]]
    states = [h0, rbuf, fbuf, s_re, s_im]
    row_spec = pl.BlockSpec((rows, D_MODEL), lambda c: (c, 0))
    in_specs = [row_spec] + [_const_spec(a.shape) for a in states + weights]
    out_shape = [jax.ShapeDtypeStruct((n_rows, D_MODEL), F32)] + [
        jax.ShapeDtypeStruct(a.shape, F32) for a in states]
    out_specs = [row_spec] + [_const_spec(a.shape) for a in states]
    scratch = [
        pltpu.VMEM(((tc + RG_CONV - 1) * bt, D_RNN), F32),
        pltpu.VMEM(((tc + CF_CONV - 1) * bt, CF_W), F32),
        pltpu.VMEM((bt, D_RNN), F32),
        pltpu.VMEM((bt, S5_N), F32),
        pltpu.VMEM((bt, S5_N), F32),
        pltpu.VMEM((rows, D_MODEL), BF16),
        pltpu.VMEM((rows, P_IN), F32),
        pltpu.VMEM((rows, D_RNN), F32),
        pltpu.VMEM((rows, 2 * D_RNN), F32),
        pltpu.VMEM((rows, D_RNN), F32),
        pltpu.VMEM((rows, D_RNN), F32),
        pltpu.VMEM((rows, S5_W), F32),
        pltpu.VMEM((rows, S5_N), F32),
        pltpu.VMEM((rows, S5_N), F32),
        pltpu.VMEM((rows, D_MODEL), BF16),
    ]
    return pl.pallas_call(
        functools.partial(_mixer_kernel, bt, tc, n_chunks),
        grid=(n_chunks,),
        in_specs=in_specs,
        out_specs=out_specs,
        out_shape=out_shape,
        scratch_shapes=scratch,
        compiler_params=pltpu.CompilerParams(dimension_semantics=("arbitrary",),
                                             vmem_limit_bytes=VMEM_LIMIT_BYTES),
        name="mixer",
    )(x_tm, *states, *weights)


def _route(logits):
    shape = logits.shape
    lane = lax.broadcasted_iota(jnp.int32, shape, 1)
    neg = jnp.float32(-jnp.inf)
    is_grp = lane < N_GROUPS
    gl = jnp.where(is_grp, logits, neg)
    gmax = jnp.max(gl, axis=-1, keepdims=True)
    gidx = jnp.min(jnp.where(gl == gmax, lane, ROUTER_LANES), axis=-1, keepdims=True)
    p_sel = 1.0 / jnp.sum(jnp.where(is_grp, jnp.exp(logits - gmax), 0.0), axis=-1, keepdims=True)
    lo = N_GROUPS + EXP_PER_GROUP * gidx
    in_grp = jnp.logical_and(lane >= lo, lane < lo + EXP_PER_GROUP)
    e1 = jnp.where(in_grp, logits, neg)
    v1 = jnp.max(e1, axis=-1, keepdims=True)
    i1 = jnp.min(jnp.where(e1 == v1, lane, ROUTER_LANES), axis=-1, keepdims=True)
    e2 = jnp.where(lane == i1, neg, e1)
    v2 = jnp.max(e2, axis=-1, keepdims=True)
    i2 = jnp.min(jnp.where(e2 == v2, lane, ROUTER_LANES), axis=-1, keepdims=True)
    ex = jnp.exp(v2 - v1)
    w1 = 1.0 / (1.0 + ex)
    w2 = ex * w1
    return jnp.where(lane == i1, p_sel * w1, jnp.where(lane == i2, p_sel * w2, 0.0))


def _moe_kernel(final, x_ref, n2_ref, wr_ref, br_ref, wg_ref, wu_ref, wd_ref, fn_ref, o_ref,
                t_sc, acc_sc):
    x = x_ref[...]
    t = _rms(x, n2_ref[...])
    t_hi = t.astype(BF16)
    t_sc[...] = t_hi
    t_lo = (t - t_hi.astype(F32)).astype(BF16)
    wr = wr_ref[...]
    wr_hi = wr.astype(BF16)
    wr_lo = (wr - wr_hi.astype(F32)).astype(BF16)
    logits = (jnp.dot(t_hi, wr_hi, preferred_element_type=F32)
              + jnp.dot(t_lo, wr_hi, preferred_element_type=F32)
              + jnp.dot(t_hi, wr_lo, preferred_element_type=F32)) + br_ref[...]
    gate = _route(logits)

    gw = EXP_PER_GROUP * EXPERT_FF
    for gi in range(N_GROUPS):
        hg = jnp.dot(t_sc[...], wg_ref[:, gi * gw:(gi + 1) * gw], preferred_element_type=F32)
        hu = jnp.dot(t_sc[...], wu_ref[:, gi * gw:(gi + 1) * gw], preferred_element_type=F32)
        parts = []
        for e in range(EXP_PER_GROUP):
            col = N_GROUPS + gi * EXP_PER_GROUP + e
            sl = slice(e * EXPERT_FF, (e + 1) * EXPERT_FF)
            hge = hg[:, sl]
            parts.append((hge * jax.nn.sigmoid(hge) * hu[:, sl] * gate[:, col:col + 1]).astype(BF16))
        act = jnp.concatenate(parts, axis=-1)
        contrib = jnp.dot(act, wd_ref[gi * gw:(gi + 1) * gw, :], preferred_element_type=F32)
        if gi == 0:
            acc_sc[...] = contrib
        else:
            acc_sc[...] += contrib
    y = x + acc_sc[...]
    if final:
        y = _rms(y, fn_ref[...])
    o_ref[...] = y


def _moe(x1, w, final, rows):
    n_rows = x1.shape[0]
    assert n_rows % rows == 0
    row_spec = pl.BlockSpec((rows, D_MODEL), lambda i: (i, 0))
    weights = [w["norm2"], w["moe_w_r"], w["moe_b_r"], w["moe_w_gate"], w["moe_w_up"], w["moe_w_down"],
               w["final_norm"]]

    def wspec(a):
        nd = a.ndim
        return pl.BlockSpec(a.shape, lambda i: (0,) * nd, pipeline_mode=pl.Buffered(1))
    return pl.pallas_call(
        functools.partial(_moe_kernel, final),
        grid=(n_rows // rows,),
        in_specs=[row_spec] + [wspec(a) for a in weights],
        out_specs=row_spec,
        out_shape=jax.ShapeDtypeStruct((n_rows, D_MODEL), F32),
        scratch_shapes=[pltpu.VMEM((rows, D_MODEL), BF16), pltpu.VMEM((rows, D_MODEL), F32)],
        compiler_params=pltpu.CompilerParams(dimension_semantics=("arbitrary",),
                                             vmem_limit_bytes=VMEM_LIMIT_BYTES),
        name="moe",
    )(x1, *weights)


def _block_diag(blocks):
    L, n, r, c = blocks.shape
    eye = jnp.eye(n, dtype=blocks.dtype)
    return (blocks[:, :, :, None, :] * eye[None, :, None, :, None]).reshape(L, n * r, n * c)


def _to_tm(a):
    a = jnp.swapaxes(a, 0, 1)
    return a.reshape((a.shape[0] * a.shape[1],) + a.shape[2:])


def _from_tm(a, b):
    return jnp.swapaxes(a.reshape((a.shape[0] // b, b) + a.shape[1:]), 0, 1)


def kernel(x_prompt, x_sample, state_rglru_h, state_rglru_conv, state_conf_conv, state_s5_re, state_s5_im, norm1, w_in, rg_conv_w, rg_conv_b, rg_wa, rg_ba, rg_wx, rg_bx, rg_lambda, cf_conv_w, cf_conv_b, cf_ln_g, cf_ln_b, s5_a_re, s5_a_im, s5_log_dt, s5_b_re, s5_b_im, s5_c_re, s5_c_im, s5_d, s5_w_glu, s5_b_glu, gn_rg, gn_cf, gn_s5, w_out, norm2, moe_w_grp, moe_b_grp, moe_w_exp, moe_b_exp, moe_w_gate, moe_w_up, moe_w_down, final_norm):
    L = w_in.shape[0]
    B, T, _ = x_prompt.shape
    DB, DT, _ = x_sample.shape

    ab_re, ab_im, bb_re, bb_im = _s5_prep(s5_a_re, s5_a_im, s5_log_dt,
                                          jnp.swapaxes(s5_b_re, 2, 3), jnp.swapaxes(s5_b_im, 2, 3))

    def row(a):
        return a.reshape(L, 1, -1)

    W = dict(
        norm1=row(norm1), w_in=w_in.astype(BF16),
        rg_conv_w=rg_conv_w, rg_conv_b=row(rg_conv_b),
        rg_wg=jnp.concatenate([_block_diag(rg_wa), _block_diag(rg_wx)], axis=-1).astype(BF16),
        rg_bg=jnp.concatenate([row(rg_ba), row(rg_bx)], axis=-1),
        rg_lambda=row(rg_lambda),
        cf_conv_w=cf_conv_w, cf_conv_b=row(cf_conv_b), cf_ln_g=row(cf_ln_g), cf_ln_b=row(cf_ln_b),
        s5_ab_re=row(ab_re), s5_ab_im=row(ab_im),
        s5_bb_re=_block_diag(bb_re).astype(BF16), s5_bb_im=_block_diag(bb_im).astype(BF16),
        s5_c_re=_block_diag(jnp.swapaxes(s5_c_re, 2, 3)).astype(BF16),
        s5_c_im=_block_diag(jnp.swapaxes(s5_c_im, 2, 3)).astype(BF16),
        s5_d=row(s5_d), s5_w_glu=s5_w_glu.astype(BF16), s5_b_glu=row(s5_b_glu),
        gn_rg=row(gn_rg), gn_cf=row(gn_cf), gn_s5=row(gn_s5), w_out=w_out.astype(BF16),
        norm2=row(norm2),
        moe_w_r=jnp.concatenate(
            [moe_w_grp, moe_w_exp.reshape(L, D_MODEL, N_EXPERTS),
             jnp.zeros((L, D_MODEL, ROUTER_LANES - N_GROUPS - N_EXPERTS), F32)], axis=-1),
        moe_b_r=jnp.concatenate(
            [moe_b_grp, moe_b_exp.reshape(L, N_EXPERTS),
             jnp.zeros((L, ROUTER_LANES - N_GROUPS - N_EXPERTS), F32)], axis=-1).reshape(L, 1, ROUTER_LANES),
        moe_w_gate=jnp.swapaxes(moe_w_gate, 1, 2).reshape(L, D_MODEL, N_EXPERTS * EXPERT_FF).astype(BF16),
        moe_w_up=jnp.swapaxes(moe_w_up, 1, 2).reshape(L, D_MODEL, N_EXPERTS * EXPERT_FF).astype(BF16),
        moe_w_down=moe_w_down.reshape(L, N_EXPERTS * EXPERT_FF, D_MODEL).astype(BF16),
    )
    fnorm = final_norm.reshape(1, D_MODEL)

    xp = _to_tm(x_prompt)
    xs = _to_tm(x_sample)
    tc_p = CHUNK_ROWS // B
    p_states, s_states = [], []
    for l in range(L):
        lw = {k: v[l] for k, v in W.items()}
        lw["final_norm"] = fnorm
        st_p = (jnp.zeros((B, D_RNN), F32), jnp.zeros(((RG_CONV - 1) * B, D_RNN), F32),
                jnp.zeros(((CF_CONV - 1) * B, CF_W), F32), jnp.zeros((B, S5_N), F32),
                jnp.zeros((B, S5_N), F32))
        st_s = (state_rglru_h[l], _to_tm(state_rglru_conv[l]), _to_tm(state_conf_conv[l]),
                state_s5_re[l].reshape(DB, S5_N), state_s5_im[l].reshape(DB, S5_N))
        xp, *np_l = _mixer(xp, B, tc_p, st_p, lw)
        xs, *ns_l = _mixer(xs, DB, DT, st_s, lw)
        final = l == L - 1
        xp = _moe(xp, lw, final, 256)
        xs = _moe(xs, lw, final, 256)
        p_states.append(np_l)
        s_states.append(ns_l)

    def assemble(states, b):
        h = jnp.stack([s[0] for s in states])
        rc = jnp.stack([_from_tm(s[1], b) for s in states])
        cf = jnp.stack([_from_tm(s[2], b) for s in states])
        sre = jnp.stack([s[3].reshape(b, S5_GROUPS, S5_STATE) for s in states])
        sim = jnp.stack([s[4].reshape(b, S5_GROUPS, S5_STATE) for s in states])
        return h, rc, cf, sre, sim

    y_prompt = _from_tm(xp, B)
    y_sample = _from_tm(xs, DB)
    return (y_prompt, y_sample) + assemble(p_states, B) + assemble(s_states, DB)
```

```python
import functools

import jax
import jax.numpy as jnp
from jax import lax
from jax.experimental import pallas as pl
from jax.experimental.pallas import tpu as pltpu

F32 = jnp.float32
BF16 = jnp.bfloat16

D_MODEL = 1024
D_RNN = 512
RG_HEADS = 8
RG_CONV = 4
RG_C = 8.0
CF_W = 256
CF_CONV = 31
S5_W = 256
S5_GROUPS = 16
S5_GROUP_CH = 16
S5_STATE = 64
S5_N = S5_GROUPS * S5_STATE
P_IN = 2 * D_RNN + 2 * CF_W + S5_W
N_GROUPS = 4
EXP_PER_GROUP = 4
N_EXPERTS = 16
EXPERT_FF = 256
EPS = 1e-6

ROUTER_LANES = 128
VMEM_LIMIT_BYTES = 60000 * 1024
ROW_BLOCK = 64
CHUNK_ROWS = 512
MOE_GROUP_CAP = 192


def _rms(x, g):
    return x * lax.rsqrt(jnp.mean(x * x, axis=-1, keepdims=True) + EPS) * g


def _softplus(z):
    return jnp.maximum(z, 0.0) + jnp.log1p(jnp.exp(-jnp.abs(z)))


def _row_blocks(rows, rb, fn):
    def body(i, carry):
        fn(pl.multiple_of(i * rb, rb))
        return carry
    lax.fori_loop(0, rows // rb, body, 0)


def _s5_prep_kernel(are_ref, aim_ref, ldt_ref, bre_ref, bim_ref,
                    abre_ref, abim_ref, bbre_ref, bbim_ref):
    a_re = are_ref[...]
    a_im = aim_ref[...]
    dt = jnp.exp(ldt_ref[...])
    mag = jnp.exp(a_re * dt)
    ab_re = mag * jnp.cos(a_im * dt)
    ab_im = mag * jnp.sin(a_im * dt)
    den = a_re * a_re + a_im * a_im
    nr = ab_re - 1.0
    ni = ab_im
    k_re = (nr * a_re + ni * a_im) / den
    k_im = (ni * a_re - nr * a_im) / den
    abre_ref[...] = ab_re
    abim_ref[...] = ab_im
    b_re = bre_ref[...]
    b_im = bim_ref[...]
    bbre_ref[...] = k_re[:, None, :] * b_re - k_im[:, None, :] * b_im
    bbim_ref[...] = k_re[:, None, :] * b_im + k_im[:, None, :] * b_re


def _s5_prep(a_re, a_im, log_dt, b_re_ghp, b_im_ghp):
    L = a_re.shape[0]
    gp = pl.BlockSpec((None, S5_GROUPS, S5_STATE), lambda l: (l, 0, 0))
    ghp = pl.BlockSpec((None, S5_GROUPS, S5_GROUP_CH, S5_STATE), lambda l: (l, 0, 0, 0))
    return pl.pallas_call(
        _s5_prep_kernel,
        grid=(L,),
        in_specs=[gp, gp, pl.BlockSpec((None, S5_GROUPS, 1), lambda l: (l, 0, 0)), ghp, ghp],
        out_specs=[gp, gp, ghp, ghp],
        out_shape=[jax.ShapeDtypeStruct((L, S5_GROUPS, S5_STATE), F32)] * 2
        + [jax.ShapeDtypeStruct((L, S5_GROUPS, S5_GROUP_CH, S5_STATE), F32)] * 2,
        name="s5_prep",
    )(a_re, a_im, log_dt[..., None], b_re_ghp, b_im_ghp)


def _mixer_kernel(bt, tc, n_chunks,
                  x_ref, h0_ref, rb_ref, fb_ref, sre_ref, sim_ref,
                  n1_ref, win_ref, rgw_ref, rgb_ref, wg_ref, bg_ref, lam_ref,
                  cfw_ref, cfb_ref, lng_ref, lnb_ref,
                  abre_ref, abim_ref, bbre_ref, bbim_ref, cre_ref, cim_ref, d_ref,
                  wglu_ref, bglu_ref, gnrg_ref, gncf_ref, gns5_ref, wout_ref,
                  x1_ref, hl_ref, rbn_ref, fbn_ref, sren_ref, simn_ref,
                  xpad, upad, h_sc, sre_sc, sim_sc, hn_sc, proj_sc, xc_sc, gate_sc,
                  a_sc, b_sc, v_sc, bre_sc, bim_sc, mix_sc):
    rows = tc * bt
    rb = ROW_BLOCK
    rg_hist = (RG_CONV - 1) * bt
    cf_hist = (CF_CONV - 1) * bt
    c = pl.program_id(0)

    @pl.when(c == 0)
    def _init():
        xpad[0:rg_hist, :] = rb_ref[...]
        upad[0:cf_hist, :] = fb_ref[...]
        h_sc[...] = h0_ref[...]
        sre_sc[...] = sre_ref[...]
        sim_sc[...] = sim_ref[...]

    def norm_in(r0):
        x = x_ref[pl.ds(r0, rb), :]
        hn_sc[pl.ds(r0, rb), :] = _rms(x, n1_ref[...]).astype(BF16)
    _row_blocks(rows, rb, norm_in)
    proj_sc[...] = jnp.dot(hn_sc[...], win_ref[...], preferred_element_type=F32)

    xpad[rg_hist:rg_hist + rows, :] = proj_sc[:, 0:D_RNN]

    def rg_conv(r0):
        acc = rgb_ref[...] + rgw_ref[0:1, :] * xpad[pl.ds(r0, rb), :]
        for k in range(1, RG_CONV):
            acc = acc + rgw_ref[k:k + 1, :] * xpad[pl.ds(r0 + k * bt, rb), :]
        xc_sc[pl.ds(r0, rb), :] = acc
    _row_blocks(rows, rb, rg_conv)

    gate_sc[...] = jnp.dot(xc_sc[...].astype(BF16), wg_ref[...],
                           preferred_element_type=F32) + bg_ref[...]

    def rg_coeffs(r0):
        g = gate_sc[pl.ds(r0, rb), :]
        r = jax.nn.sigmoid(g[:, 0:D_RNN])
        i = jax.nn.sigmoid(g[:, D_RNN:2 * D_RNN])
        log_a = -RG_C * r * _softplus(-lam_ref[...])
        a_sc[pl.ds(r0, rb), :] = jnp.exp(log_a)
        mult = jnp.sqrt(1.0 - jnp.exp(2.0 * log_a))
        b_sc[pl.ds(r0, rb), :] = mult * (i * xc_sc[pl.ds(r0, rb), :])
    _row_blocks(rows, rb, rg_coeffs)

    def rg_step(t, h):
        r0 = pl.multiple_of(t * bt, bt)
        h = a_sc[pl.ds(r0, bt), :] * h + b_sc[pl.ds(r0, bt), :]
        b_sc[pl.ds(r0, bt), :] = h
        return h
    h_last = lax.fori_loop(0, tc, rg_step, h_sc[...], unroll=min(tc, 8))
    h_sc[...] = h_last
    hl_ref[...] = h_last

    def rg_out(r0):
        h = b_sc[pl.ds(r0, rb), :]
        yb = proj_sc[pl.ds(r0, rb), D_RNN:2 * D_RNN]
        mix_sc[pl.ds(r0, rb), 0:D_RNN] = _rms(h * jax.nn.gelu(yb), gnrg_ref[...]).astype(BF16)
    _row_blocks(rows, rb, rg_out)

    rbn_ref[...] = xpad[rows:rows + rg_hist, :]
    if n_chunks > 1:
        xpad[0:rg_hist, :] = xpad[rows:rows + rg_hist, :]

    o_cf = 2 * D_RNN

    def cf_glu(r0):
        ca = proj_sc[pl.ds(r0, rb), o_cf:o_cf + CF_W]
        cg = proj_sc[pl.ds(r0, rb), o_cf + CF_W:o_cf + 2 * CF_W]
        upad[pl.ds(cf_hist + r0, rb), :] = ca * jax.nn.sigmoid(cg)
    _row_blocks(rows, rb, cf_glu)

    def cf_conv(r0):
        acc = cfb_ref[...] + cfw_ref[0:1, :] * upad[pl.ds(r0, rb), :]
        for k in range(1, CF_CONV):
            acc = acc + cfw_ref[k:k + 1, :] * upad[pl.ds(r0 + k * bt, rb), :]
        mu = jnp.mean(acc, axis=-1, keepdims=True)
        cen = acc - mu
        var = jnp.mean(cen * cen, axis=-1, keepdims=True)
        vn = cen * lax.rsqrt(var + EPS) * lng_ref[...] + lnb_ref[...]
        ob = vn * jax.nn.sigmoid(vn)
        mix_sc[pl.ds(r0, rb), D_RNN:D_RNN + CF_W] = _rms(ob, gncf_ref[...]).astype(BF16)
    _row_blocks(rows, rb, cf_conv)

    fbn_ref[...] = upad[rows:rows + cf_hist, :]
    if n_chunks > 1:
        upad[0:cf_hist, :] = upad[rows:rows + cf_hist, :]

    o_s5 = o_cf + 2 * CF_W
    su_bf = proj_sc[:, o_s5:o_s5 + S5_W].astype(BF16)
    bre_sc[...] = jnp.dot(su_bf, bbre_ref[...], preferred_element_type=F32)
    bim_sc[...] = jnp.dot(su_bf, bbim_ref[...], preferred_element_type=F32)

    a_r = jnp.broadcast_to(abre_ref[...], (bt, S5_N))
    a_i = jnp.broadcast_to(abim_ref[...], (bt, S5_N))

    def s5_step(t, carry):
        s_r, s_i = carry
        r0 = pl.multiple_of(t * bt, bt)
        n_r = a_r * s_r - a_i * s_i + bre_sc[pl.ds(r0, bt), :]
        n_i = a_r * s_i + a_i * s_r + bim_sc[pl.ds(r0, bt), :]
        bre_sc[pl.ds(r0, bt), :] = n_r
        bim_sc[pl.ds(r0, bt), :] = n_i
        return n_r, n_i
    s_r, s_i = lax.fori_loop(0, tc, s5_step, (sre_sc[...], sim_sc[...]), unroll=min(tc, 4))
    sre_sc[...] = s_r
    sim_sc[...] = s_i
    sren_ref[...] = s_r
    simn_ref[...] = s_i

    v_sc[...] = (jnp.dot(bre_sc[...].astype(BF16), cre_ref[...], preferred_element_type=F32)
                 - jnp.dot(bim_sc[...].astype(BF16), cim_ref[...], preferred_element_type=F32))

    def s5_out(r0):
        su = proj_sc[pl.ds(r0, rb), o_s5:o_s5 + S5_W]
        z = jax.nn.gelu(v_sc[pl.ds(r0, rb), :] + d_ref[...] * su)
        gl = jnp.dot(z.astype(BF16), wglu_ref[...], preferred_element_type=F32) + bglu_ref[...]
        oc = z * jax.nn.sigmoid(gl)
        mix_sc[pl.ds(r0, rb), D_RNN + CF_W:D_MODEL] = _rms(oc, gns5_ref[...]).astype(BF16)
    _row_blocks(rows, rb, s5_out)

    x1_ref[...] = x_ref[...] + jnp.dot(mix_sc[...], wout_ref[...], preferred_element_type=F32)


def _const_spec(shape):
    nd = len(shape)
    return pl.BlockSpec(shape, lambda c: (0,) * nd)


def _mixer(x_tm, bt, tc, state, w):
    n_rows = x_tm.shape[0]
    rows = tc * bt
    n_chunks = n_rows // rows
    assert n_chunks * rows == n_rows and rows % ROW_BLOCK == 0 and bt % 8 == 0
    assert n_chunks == 1 or tc >= CF_CONV - 1
    h0, rbuf, fbuf, s_re, s_im = state
    weights = [w["norm1"], w["w_in"], w["rg_conv_w"], w["rg_conv_b"], w["rg_wg"], w["rg_bg"],
               w["rg_lambda"], w["cf_conv_w"], w["cf_conv_b"], w["cf_ln_g"], w["cf_ln_b"],
               w["s5_ab_re"], w["s5_ab_im"], w["s5_bb_re"], w["s5_bb_im"], w["s5_c_re"], w["s5_c_im"],
               w["s5_d"], w["s5_w_glu"], w["s5_b_glu"], w["gn_rg"], w["gn_cf"], w["gn_s5"], w["w_out"]]
    states = [h0, rbuf, fbuf, s_re, s_im]
    row_spec = pl.BlockSpec((rows, D_MODEL), lambda c: (c, 0))
    in_specs = [row_spec] + [_const_spec(a.shape) for a in states + weights]
    out_shape = [jax.ShapeDtypeStruct((n_rows, D_MODEL), F32)] + [
        jax.ShapeDtypeStruct(a.shape, F32) for a in states]
    out_specs = [row_spec] + [_const_spec(a.shape) for a in states]
    scratch = [
        pltpu.VMEM(((tc + RG_CONV - 1) * bt, D_RNN), F32),
        pltpu.VMEM(((tc + CF_CONV - 1) * bt, CF_W), F32),
        pltpu.VMEM((bt, D_RNN), F32),
        pltpu.VMEM((bt, S5_N), F32),
        pltpu.VMEM((bt, S5_N), F32),
        pltpu.VMEM((rows, D_MODEL), BF16),
        pltpu.VMEM((rows, P_IN), F32),
        pltpu.VMEM((rows, D_RNN), F32),
        pltpu.VMEM((rows, 2 * D_RNN), F32),
        pltpu.VMEM((rows, D_RNN), F32),
        pltpu.VMEM((rows, D_RNN), F32),
        pltpu.VMEM((rows, S5_W), F32),
        pltpu.VMEM((rows, S5_N), F32),
        pltpu.VMEM((rows, S5_N), F32),
        pltpu.VMEM((rows, D_MODEL), BF16),
    ]
    return pl.pallas_call(
        functools.partial(_mixer_kernel, bt, tc, n_chunks),
        grid=(n_chunks,),
        in_specs=in_specs,
        out_specs=out_specs,
        out_shape=out_shape,
        scratch_shapes=scratch,
        compiler_params=pltpu.CompilerParams(dimension_semantics=("arbitrary",),
                                             vmem_limit_bytes=VMEM_LIMIT_BYTES),
        name="mixer",
    )(x_tm, *states, *weights)


def _route(logits):
    shape = logits.shape
    lane = lax.broadcasted_iota(jnp.int32, shape, 1)
    neg = jnp.float32(-jnp.inf)
    is_grp = lane < N_GROUPS
    gl = jnp.where(is_grp, logits, neg)
    gmax = jnp.max(gl, axis=-1, keepdims=True)
    gidx = jnp.min(jnp.where(gl == gmax, lane, ROUTER_LANES), axis=-1, keepdims=True)
    p_sel = 1.0 / jnp.sum(jnp.where(is_grp, jnp.exp(logits - gmax), 0.0), axis=-1, keepdims=True)
    lo = N_GROUPS + EXP_PER_GROUP * gidx
    in_grp = jnp.logical_and(lane >= lo, lane < lo + EXP_PER_GROUP)
    e1 = jnp.where(in_grp, logits, neg)
    v1 = jnp.max(e1, axis=-1, keepdims=True)
    i1 = jnp.min(jnp.where(e1 == v1, lane, ROUTER_LANES), axis=-1, keepdims=True)
    e2 = jnp.where(lane == i1, neg, e1)
    v2 = jnp.max(e2, axis=-1, keepdims=True)
    i2 = jnp.min(jnp.where(e2 == v2, lane, ROUTER_LANES), axis=-1, keepdims=True)
    ex = jnp.exp(v2 - v1)
    w1 = 1.0 / (1.0 + ex)
    w2 = ex * w1
    gate = jnp.where(lane == i1, p_sel * w1, jnp.where(lane == i2, p_sel * w2, 0.0))
    return gate, gidx


def _group_mlp(t_rows, gate_rows, gi, wg_ref, wu_ref, wd_ref):
    gw = EXP_PER_GROUP * EXPERT_FF
    hg = jnp.dot(t_rows, wg_ref[:, gi * gw:(gi + 1) * gw], preferred_element_type=F32)
    hu = jnp.dot(t_rows, wu_ref[:, gi * gw:(gi + 1) * gw], preferred_element_type=F32)
    parts = []
    for e in range(EXP_PER_GROUP):
        col = N_GROUPS + gi * EXP_PER_GROUP + e
        sl = slice(e * EXPERT_FF, (e + 1) * EXPERT_FF)
        hge = hg[:, sl]
        parts.append((hge * jax.nn.sigmoid(hge) * hu[:, sl] * gate_rows[:, col:col + 1]).astype(BF16))
    act = jnp.concatenate(parts, axis=-1)
    return jnp.dot(act, wd_ref[gi * gw:(gi + 1) * gw, :], preferred_element_type=F32)


def _moe_kernel(final, cap, x_ref, n2_ref, wr_ref, br_ref, wg_ref, wu_ref, wd_ref, fn_ref, o_ref,
                t_sc, gate_sc, ts_sc, os_sc, acc_sc):
    rows = x_ref.shape[0]
    slots = N_GROUPS * cap
    x = x_ref[...]
    t = _rms(x, n2_ref[...])
    t_hi = t.astype(BF16)
    t_sc[...] = t_hi
    t_lo = (t - t_hi.astype(F32)).astype(BF16)
    wr = wr_ref[...]
    wr_hi = wr.astype(BF16)
    wr_lo = (wr - wr_hi.astype(F32)).astype(BF16)
    logits = (jnp.dot(t_hi, wr_hi, preferred_element_type=F32)
              + jnp.dot(t_lo, wr_hi, preferred_element_type=F32)
              + jnp.dot(t_hi, wr_lo, preferred_element_type=F32)) + br_ref[...]
    gate, gidx = _route(logits)
    gate_sc[...] = gate

    lane = lax.broadcasted_iota(jnp.int32, (rows, ROUTER_LANES), 1)
    onehot = jnp.where(lane == gidx, 1.0, 0.0)
    r_i = lax.broadcasted_iota(jnp.int32, (rows, rows), 0)
    c_i = lax.broadcasted_iota(jnp.int32, (rows, rows), 1)
    tri = jnp.where(c_i < r_i, 1.0, 0.0).astype(BF16)
    before = jnp.dot(tri, onehot.astype(BF16), preferred_element_type=F32)
    rank = jnp.sum(before * onehot, axis=-1, keepdims=True)
    counts = jnp.sum(onehot, axis=0, keepdims=True)
    fits = jnp.max(counts) <= cap

    @pl.when(fits)
    def _sparse():
        slot = gidx.astype(F32) * cap + rank
        slot_t = jnp.transpose(jnp.broadcast_to(slot, (rows, ROUTER_LANES)))[0:1, :]
        s_sub = lax.broadcasted_iota(jnp.int32, (slots, rows), 0).astype(F32)
        perm = jnp.where(s_sub == slot_t, 1.0, 0.0).astype(BF16)
        s_lane = lax.broadcasted_iota(jnp.int32, (rows, slots), 1).astype(F32)
        perm_t = jnp.where(s_lane == slot, 1.0, 0.0).astype(BF16)

        ts_sc[...] = jnp.dot(perm, t_sc[...], preferred_element_type=F32).astype(BF16)
        g = gate_sc[...]
        g_hi = g.astype(BF16).astype(F32)
        packed = (g_hi + pltpu.roll(g - g_hi, ROUTER_LANES // 2, 1)).astype(BF16)
        gs = jnp.dot(perm, packed, preferred_element_type=F32)
        gs = gs + pltpu.roll(gs, ROUTER_LANES // 2, 1)
        for gi in range(N_GROUPS):
            sl = slice(gi * cap, (gi + 1) * cap)
            og = _group_mlp(ts_sc[sl, :], gs[sl, :], gi, wg_ref, wu_ref, wd_ref)
            os_sc[sl, :] = og.astype(BF16)
        acc_sc[...] = jnp.dot(perm_t, os_sc[...], preferred_element_type=F32)

    @pl.when(jnp.logical_not(fits))
    def _dense():
        for gi in range(N_GROUPS):
            contrib = _group_mlp(t_sc[...], gate_sc[...], gi, wg_ref, wu_ref, wd_ref)
            if gi == 0:
                acc_sc[...] = contrib
            else:
                acc_sc[...] += contrib

    y = x_ref[...] + acc_sc[...]
    if final:
        y = _rms(y, fn_ref[...])
    o_ref[...] = y


def _moe(x1, w, final, rows, cap):
    n_rows = x1.shape[0]
    assert n_rows % rows == 0 and cap % 16 == 0
    slots = N_GROUPS * cap
    row_spec = pl.BlockSpec((rows, D_MODEL), lambda i: (i, 0))
    weights = [w["norm2"], w["moe_w_r"], w["moe_b_r"], w["moe_w_gate"], w["moe_w_up"], w["moe_w_down"],
               w["final_norm"]]

    def wspec(a):
        nd = a.ndim
        return pl.BlockSpec(a.shape, lambda i: (0,) * nd, pipeline_mode=pl.Buffered(1))
    return pl.pallas_call(
        functools.partial(_moe_kernel, final, cap),
        grid=(n_rows // rows,),
        in_specs=[row_spec] + [wspec(a) for a in weights],
        out_specs=row_spec,
        out_shape=jax.ShapeDtypeStruct((n_rows, D_MODEL), F32),
        scratch_shapes=[pltpu.VMEM((rows, D_MODEL), BF16),
                        pltpu.VMEM((rows, ROUTER_LANES), F32),
                        pltpu.VMEM((slots, D_MODEL), BF16),
                        pltpu.VMEM((slots, D_MODEL), BF16),
                        pltpu.VMEM((rows, D_MODEL), F32)],
        compiler_params=pltpu.CompilerParams(dimension_semantics=("arbitrary",),
                                             vmem_limit_bytes=VMEM_LIMIT_BYTES),
        name="moe",
    )(x1, *weights)


def _block_diag(blocks):
    L, n, r, c = blocks.shape
    eye = jnp.eye(n, dtype=blocks.dtype)
    return (blocks[:, :, :, None, :] * eye[None, :, None, :, None]).reshape(L, n * r, n * c)


def _to_tm(a):
    a = jnp.swapaxes(a, 0, 1)
    return a.reshape((a.shape[0] * a.shape[1],) + a.shape[2:])


def _from_tm(a, b):
    return jnp.swapaxes(a.reshape((a.shape[0] // b, b) + a.shape[1:]), 0, 1)


def kernel(x_prompt, x_sample, state_rglru_h, state_rglru_conv, state_conf_conv, state_s5_re, state_s5_im, norm1, w_in, rg_conv_w, rg_conv_b, rg_wa, rg_ba, rg_wx, rg_bx, rg_lambda, cf_conv_w, cf_conv_b, cf_ln_g, cf_ln_b, s5_a_re, s5_a_im, s5_log_dt, s5_b_re, s5_b_im, s5_c_re, s5_c_im, s5_d, s5_w_glu, s5_b_glu, gn_rg, gn_cf, gn_s5, w_out, norm2, moe_w_grp, moe_b_grp, moe_w_exp, moe_b_exp, moe_w_gate, moe_w_up, moe_w_down, final_norm):
    L = w_in.shape[0]
    B, T, _ = x_prompt.shape
    DB, DT, _ = x_sample.shape

    ab_re, ab_im, bb_re, bb_im = _s5_prep(s5_a_re, s5_a_im, s5_log_dt,
                                          jnp.swapaxes(s5_b_re, 2, 3), jnp.swapaxes(s5_b_im, 2, 3))

    def row(a):
        return a.reshape(L, 1, -1)

    W = dict(
        norm1=row(norm1), w_in=w_in.astype(BF16),
        rg_conv_w=rg_conv_w, rg_conv_b=row(rg_conv_b),
        rg_wg=jnp.concatenate([_block_diag(rg_wa), _block_diag(rg_wx)], axis=-1).astype(BF16),
        rg_bg=jnp.concatenate([row(rg_ba), row(rg_bx)], axis=-1),
        rg_lambda=row(rg_lambda),
        cf_conv_w=cf_conv_w, cf_conv_b=row(cf_conv_b), cf_ln_g=row(cf_ln_g), cf_ln_b=row(cf_ln_b),
        s5_ab_re=row(ab_re), s5_ab_im=row(ab_im),
        s5_bb_re=_block_diag(bb_re).astype(BF16), s5_bb_im=_block_diag(bb_im).astype(BF16),
        s5_c_re=_block_diag(jnp.swapaxes(s5_c_re, 2, 3)).astype(BF16),
        s5_c_im=_block_diag(jnp.swapaxes(s5_c_im, 2, 3)).astype(BF16),
        s5_d=row(s5_d), s5_w_glu=s5_w_glu.astype(BF16), s5_b_glu=row(s5_b_glu),
        gn_rg=row(gn_rg), gn_cf=row(gn_cf), gn_s5=row(gn_s5), w_out=w_out.astype(BF16),
        norm2=row(norm2),
        moe_w_r=jnp.concatenate(
            [moe_w_grp, moe_w_exp.reshape(L, D_MODEL, N_EXPERTS),
             jnp.zeros((L, D_MODEL, ROUTER_LANES - N_GROUPS - N_EXPERTS), F32)], axis=-1),
        moe_b_r=jnp.concatenate(
            [moe_b_grp, moe_b_exp.reshape(L, N_EXPERTS),
             jnp.zeros((L, ROUTER_LANES - N_GROUPS - N_EXPERTS), F32)], axis=-1).reshape(L, 1, ROUTER_LANES),
        moe_w_gate=jnp.swapaxes(moe_w_gate, 1, 2).reshape(L, D_MODEL, N_EXPERTS * EXPERT_FF).astype(BF16),
        moe_w_up=jnp.swapaxes(moe_w_up, 1, 2).reshape(L, D_MODEL, N_EXPERTS * EXPERT_FF).astype(BF16),
        moe_w_down=moe_w_down.reshape(L, N_EXPERTS * EXPERT_FF, D_MODEL).astype(BF16),
    )
    fnorm = final_norm.reshape(1, D_MODEL)

    xp = _to_tm(x_prompt)
    xs = _to_tm(x_sample)
    tc_p = CHUNK_ROWS // B
    p_states, s_states = [], []
    for l in range(L):
        lw = {k: v[l] for k, v in W.items()}
        lw["final_norm"] = fnorm
        st_p = (jnp.zeros((B, D_RNN), F32), jnp.zeros(((RG_CONV - 1) * B, D_RNN), F32),
                jnp.zeros(((CF_CONV - 1) * B, CF_W), F32), jnp.zeros((B, S5_N), F32),
                jnp.zeros((B, S5_N), F32))
        st_s = (state_rglru_h[l], _to_tm(state_rglru_conv[l]), _to_tm(state_conf_conv[l]),
                state_s5_re[l].reshape(DB, S5_N), state_s5_im[l].reshape(DB, S5_N))
        xp, *np_l = _mixer(xp, B, tc_p, st_p, lw)
        xs, *ns_l = _mixer(xs, DB, DT, st_s, lw)
        final = l == L - 1
        xp = _moe(xp, lw, final, CHUNK_ROWS, MOE_GROUP_CAP)
        xs = _moe(xs, lw, final, CHUNK_ROWS, MOE_GROUP_CAP)
        p_states.append(np_l)
        s_states.append(ns_l)

    def assemble(states, b):
        h = jnp.stack([s[0] for s in states])
        rc = jnp.stack([_from_tm(s[1], b) for s in states])
        cf = jnp.stack([_from_tm(s[2], b) for s in states])
        sre = jnp.stack([s[3].reshape(b, S5_GROUPS, S5_STATE) for s in states])
        sim = jnp.stack([s[4].reshape(b, S5_GROUPS, S5_STATE) for s in states])
        return h, rc, cf, sre, sim

    y_prompt = _from_tm(xp, B)
    y_sample = _from_tm(xs, DB)
    return (y_prompt, y_sample) + assemble(p_states, B) + assemble(s_states, DB)
```

```python
import functools

import jax
import jax.numpy as jnp
from jax import lax
from jax.experimental import pallas as pl
from jax.experimental.pallas import tpu as pltpu

F32 = jnp.float32
BF16 = jnp.bfloat16

D_MODEL = 1024
D_RNN = 512
RG_HEADS = 8
RG_CONV = 4
RG_C = 8.0
CF_W = 256
CF_CONV = 31
S5_W = 256
S5_GROUPS = 16
S5_GROUP_CH = 16
S5_STATE = 64
S5_N = S5_GROUPS * S5_STATE
P_IN = 2 * D_RNN + 2 * CF_W + S5_W
N_GROUPS = 4
EXP_PER_GROUP = 4
N_EXPERTS = 16
EXPERT_FF = 256
EPS = 1e-6

ROUTER_LANES = 128
VMEM_LIMIT_BYTES = 60000 * 1024
ROW_BLOCK = 64
CHUNK_ROWS = 512
MOE_UNIT = 16
MOE_TILE_UNITS = CHUNK_ROWS // MOE_UNIT + N_GROUPS
MOE_TILE_SLOTS = MOE_TILE_UNITS * MOE_UNIT
MOE_MTILE_UNITS = 16


def _rms(x, g):
    return x * lax.rsqrt(jnp.mean(x * x, axis=-1, keepdims=True) + EPS) * g


def _softplus(z):
    return jnp.maximum(z, 0.0) + jnp.log1p(jnp.exp(-jnp.abs(z)))


def _row_blocks(rows, rb, fn):
    def body(i, carry):
        fn(pl.multiple_of(i * rb, rb))
        return carry
    lax.fori_loop(0, rows // rb, body, 0)


def _s5_prep_kernel(are_ref, aim_ref, ldt_ref, bre_ref, bim_ref,
                    abre_ref, abim_ref, bbre_ref, bbim_ref):
    a_re = are_ref[...]
    a_im = aim_ref[...]
    dt = jnp.exp(ldt_ref[...])
    mag = jnp.exp(a_re * dt)
    ab_re = mag * jnp.cos(a_im * dt)
    ab_im = mag * jnp.sin(a_im * dt)
    den = a_re * a_re + a_im * a_im
    nr = ab_re - 1.0
    ni = ab_im
    k_re = (nr * a_re + ni * a_im) / den
    k_im = (ni * a_re - nr * a_im) / den
    abre_ref[...] = ab_re
    abim_ref[...] = ab_im
    b_re = bre_ref[...]
    b_im = bim_ref[...]
    bbre_ref[...] = k_re[:, None, :] * b_re - k_im[:, None, :] * b_im
    bbim_ref[...] = k_re[:, None, :] * b_im + k_im[:, None, :] * b_re


def _s5_prep(a_re, a_im, log_dt, b_re_ghp, b_im_ghp):
    L = a_re.shape[0]
    gp = pl.BlockSpec((None, S5_GROUPS, S5_STATE), lambda l: (l, 0, 0))
    ghp = pl.BlockSpec((None, S5_GROUPS, S5_GROUP_CH, S5_STATE), lambda l: (l, 0, 0, 0))
    return pl.pallas_call(
        _s5_prep_kernel,
        grid=(L,),
        in_specs=[gp, gp, pl.BlockSpec((None, S5_GROUPS, 1), lambda l: (l, 0, 0)), ghp, ghp],
        out_specs=[gp, gp, ghp, ghp],
        out_shape=[jax.ShapeDtypeStruct((L, S5_GROUPS, S5_STATE), F32)] * 2
        + [jax.ShapeDtypeStruct((L, S5_GROUPS, S5_GROUP_CH, S5_STATE), F32)] * 2,
        name="s5_prep",
    )(a_re, a_im, log_dt[..., None], b_re_ghp, b_im_ghp)


def _mixer_kernel(bt, tc, n_chunks,
                  x_ref, h0_ref, rb_ref, fb_ref, sre_ref, sim_ref,
                  n1_ref, win_ref, rgw_ref, rgb_ref, wg_ref, bg_ref, lam_ref,
                  cfw_ref, cfb_ref, lng_ref, lnb_ref,
                  abre_ref, abim_ref, bbre_ref, bbim_ref, cre_ref, cim_ref, d_ref,
                  wglu_ref, bglu_ref, gnrg_ref, gncf_ref, gns5_ref, wout_ref,
                  x1_ref, hl_ref, rbn_ref, fbn_ref, sren_ref, simn_ref,
                  xpad, upad, h_sc, sre_sc, sim_sc, hn_sc, proj_sc, xc_sc, gate_sc,
                  a_sc, b_sc, v_sc, bre_sc, bim_sc, mix_sc):
    rows = tc * bt
    rb = ROW_BLOCK
    rg_hist = (RG_CONV - 1) * bt
    cf_hist = (CF_CONV - 1) * bt
    c = pl.program_id(0)

    @pl.when(c == 0)
    def _init():
        xpad[0:rg_hist, :] = rb_ref[...]
        upad[0:cf_hist, :] = fb_ref[...]
        h_sc[...] = h0_ref[...]
        sre_sc[...] = sre_ref[...]
        sim_sc[...] = sim_ref[...]

    def norm_in(r0):
        x = x_ref[pl.ds(r0, rb), :]
        hn_sc[pl.ds(r0, rb), :] = _rms(x, n1_ref[...]).astype(BF16)
    _row_blocks(rows, rb, norm_in)
    proj_sc[...] = jnp.dot(hn_sc[...], win_ref[...], preferred_element_type=F32)

    xpad[rg_hist:rg_hist + rows, :] = proj_sc[:, 0:D_RNN]

    def rg_conv(r0):
        acc = rgb_ref[...] + rgw_ref[0:1, :] * xpad[pl.ds(r0, rb), :]
        for k in range(1, RG_CONV):
            acc = acc + rgw_ref[k:k + 1, :] * xpad[pl.ds(r0 + k * bt, rb), :]
        xc_sc[pl.ds(r0, rb), :] = acc
    _row_blocks(rows, rb, rg_conv)

    gate_sc[...] = jnp.dot(xc_sc[...].astype(BF16), wg_ref[...],
                           preferred_element_type=F32) + bg_ref[...]

    def rg_coeffs(r0):
        g = gate_sc[pl.ds(r0, rb), :]
        r = jax.nn.sigmoid(g[:, 0:D_RNN])
        i = jax.nn.sigmoid(g[:, D_RNN:2 * D_RNN])
        log_a = -RG_C * r * _softplus(-lam_ref[...])
        a_sc[pl.ds(r0, rb), :] = jnp.exp(log_a)
        mult = jnp.sqrt(1.0 - jnp.exp(2.0 * log_a))
        b_sc[pl.ds(r0, rb), :] = mult * (i * xc_sc[pl.ds(r0, rb), :])
    _row_blocks(rows, rb, rg_coeffs)

    def rg_step(t, h):
        r0 = pl.multiple_of(t * bt, bt)
        h = a_sc[pl.ds(r0, bt), :] * h + b_sc[pl.ds(r0, bt), :]
        b_sc[pl.ds(r0, bt), :] = h
        return h
    h_last = lax.fori_loop(0, tc, rg_step, h_sc[...], unroll=min(tc, 8))
    h_sc[...] = h_last
    hl_ref[...] = h_last

    def rg_out(r0):
        h = b_sc[pl.ds(r0, rb), :]
        yb = proj_sc[pl.ds(r0, rb), D_RNN:2 * D_RNN]
        mix_sc[pl.ds(r0, rb), 0:D_RNN] = _rms(h * jax.nn.gelu(yb), gnrg_ref[...]).astype(BF16)
    _row_blocks(rows, rb, rg_out)

    rbn_ref[...] = xpad[rows:rows + rg_hist, :]
    if n_chunks > 1:
        xpad[0:rg_hist, :] = xpad[rows:rows + rg_hist, :]

    o_cf = 2 * D_RNN

    def cf_glu(r0):
        ca = proj_sc[pl.ds(r0, rb), o_cf:o_cf + CF_W]
        cg = proj_sc[pl.ds(r0, rb), o_cf + CF_W:o_cf + 2 * CF_W]
        upad[pl.ds(cf_hist + r0, rb), :] = ca * jax.nn.sigmoid(cg)
    _row_blocks(rows, rb, cf_glu)

    def cf_conv(r0):
        acc = cfb_ref[...] + cfw_ref[0:1, :] * upad[pl.ds(r0, rb), :]
        for k in range(1, CF_CONV):
            acc = acc + cfw_ref[k:k + 1, :] * upad[pl.ds(r0 + k * bt, rb), :]
        mu = jnp.mean(acc, axis=-1, keepdims=True)
        cen = acc - mu
        var = jnp.mean(cen * cen, axis=-1, keepdims=True)
        vn = cen * lax.rsqrt(var + EPS) * lng_ref[...] + lnb_ref[...]
        ob = vn * jax.nn.sigmoid(vn)
        mix_sc[pl.ds(r0, rb), D_RNN:D_RNN + CF_W] = _rms(ob, gncf_ref[...]).astype(BF16)
    _row_blocks(rows, rb, cf_conv)

    fbn_ref[...] = upad[rows:rows + cf_hist, :]
    if n_chunks > 1:
        upad[0:cf_hist, :] = upad[rows:rows + cf_hist, :]

    o_s5 = o_cf + 2 * CF_W
    su_bf = proj_sc[:, o_s5:o_s5 + S5_W].astype(BF16)
    bre_sc[...] = jnp.dot(su_bf, bbre_ref[...], preferred_element_type=F32)
    bim_sc[...] = jnp.dot(su_bf, bbim_ref[...], preferred_element_type=F32)

    a_r = jnp.broadcast_to(abre_ref[...], (bt, S5_N))
    a_i = jnp.broadcast_to(abim_ref[...], (bt, S5_N))

    def s5_step(t, carry):
        s_r, s_i = carry
        r0 = pl.multiple_of(t * bt, bt)
        n_r = a_r * s_r - a_i * s_i + bre_sc[pl.ds(r0, bt), :]
        n_i = a_r * s_i + a_i * s_r + bim_sc[pl.ds(r0, bt), :]
        bre_sc[pl.ds(r0, bt), :] = n_r
        bim_sc[pl.ds(r0, bt), :] = n_i
        return n_r, n_i
    s_r, s_i = lax.fori_loop(0, tc, s5_step, (sre_sc[...], sim_sc[...]), unroll=min(tc, 4))
    sre_sc[...] = s_r
    sim_sc[...] = s_i
    sren_ref[...] = s_r
    simn_ref[...] = s_i

    v_sc[...] = (jnp.dot(bre_sc[...].astype(BF16), cre_ref[...], preferred_element_type=F32)
                 - jnp.dot(bim_sc[...].astype(BF16), cim_ref[...], preferred_element_type=F32))

    def s5_out(r0):
        su = proj_sc[pl.ds(r0, rb), o_s5:o_s5 + S5_W]
        z = jax.nn.gelu(v_sc[pl.ds(r0, rb), :] + d_ref[...] * su)
        gl = jnp.dot(z.astype(BF16), wglu_ref[...], preferred_element_type=F32) + bglu_ref[...]
        oc = z * jax.nn.sigmoid(gl)
        mix_sc[pl.ds(r0, rb), D_RNN + CF_W:D_MODEL] = _rms(oc, gns5_ref[...]).astype(BF16)
    _row_blocks(rows, rb, s5_out)

    x1_ref[...] = x_ref[...] + jnp.dot(mix_sc[...], wout_ref[...], preferred_element_type=F32)


def _const_spec(shape):
    nd = len(shape)
    return pl.BlockSpec(shape, lambda c: (0,) * nd)


def _mixer(x_tm, bt, tc, state, w):
    n_rows = x_tm.shape[0]
    rows = tc * bt
    n_chunks = n_rows // rows
    assert n_chunks * rows == n_rows and rows % ROW_BLOCK == 0 and bt % 8 == 0
    assert n_chunks == 1 or tc >= CF_CONV - 1
    h0, rbuf, fbuf, s_re, s_im = state
    weights = [w["norm1"], w["w_in"], w["rg_conv_w"], w["rg_conv_b"], w["rg_wg"], w["rg_bg"],
               w["rg_lambda"], w["cf_conv_w"], w["cf_conv_b"], w["cf_ln_g"], w["cf_ln_b"],
               w["s5_ab_re"], w["s5_ab_im"], w["s5_bb_re"], w["s5_bb_im"], w["s5_c_re"], w["s5_c_im"],
               w["s5_d"], w["s5_w_glu"], w["s5_b_glu"], w["gn_rg"], w["gn_cf"], w["gn_s5"], w["w_out"]]
    states = [h0, rbuf, fbuf, s_re, s_im]
    row_spec = pl.BlockSpec((rows, D_MODEL), lambda c: (c, 0))
    in_specs = [row_spec] + [_const_spec(a.shape) for a in states + weights]
    out_shape = [jax.ShapeDtypeStruct((n_rows, D_MODEL), F32)] + [
        jax.ShapeDtypeStruct(a.shape, F32) for a in states]
    out_specs = [row_spec] + [_const_spec(a.shape) for a in states]
    scratch = [
        pltpu.VMEM(((tc + RG_CONV - 1) * bt, D_RNN), F32),
        pltpu.VMEM(((tc + CF_CONV - 1) * bt, CF_W), F32),
        pltpu.VMEM((bt, D_RNN), F32),
        pltpu.VMEM((bt, S5_N), F32),
        pltpu.VMEM((bt, S5_N), F32),
        pltpu.VMEM((rows, D_MODEL), BF16),
        pltpu.VMEM((rows, P_IN), F32),
        pltpu.VMEM((rows, D_RNN), F32),
        pltpu.VMEM((rows, 2 * D_RNN), F32),
        pltpu.VMEM((rows, D_RNN), F32),
        pltpu.VMEM((rows, D_RNN), F32),
        pltpu.VMEM((rows, S5_W), F32),
        pltpu.VMEM((rows, S5_N), F32),
        pltpu.VMEM((rows, S5_N), F32),
        pltpu.VMEM((rows, D_MODEL), BF16),
    ]
    return pl.pallas_call(
        functools.partial(_mixer_kernel, bt, tc, n_chunks),
        grid=(n_chunks,),
        in_specs=in_specs,
        out_specs=out_specs,
        out_shape=out_shape,
        scratch_shapes=scratch,
        compiler_params=pltpu.CompilerParams(dimension_semantics=("arbitrary",),
                                             vmem_limit_bytes=VMEM_LIMIT_BYTES),
        name="mixer",
    )(x_tm, *states, *weights)


def _route(logits):
    shape = logits.shape
    lane = lax.broadcasted_iota(jnp.int32, shape, 1)
    neg = jnp.float32(-jnp.inf)
    is_grp = lane < N_GROUPS
    gl = jnp.where(is_grp, logits, neg)
    gmax = jnp.max(gl, axis=-1, keepdims=True)
    gidx = jnp.min(jnp.where(gl == gmax, lane, ROUTER_LANES), axis=-1, keepdims=True)
    p_sel = 1.0 / jnp.sum(jnp.where(is_grp, jnp.exp(logits - gmax), 0.0), axis=-1, keepdims=True)
    lo = N_GROUPS + EXP_PER_GROUP * gidx
    in_grp = jnp.logical_and(lane >= lo, lane < lo + EXP_PER_GROUP)
    e1 = jnp.where(in_grp, logits, neg)
    v1 = jnp.max(e1, axis=-1, keepdims=True)
    i1 = jnp.min(jnp.where(e1 == v1, lane, ROUTER_LANES), axis=-1, keepdims=True)
    e2 = jnp.where(lane == i1, neg, e1)
    v2 = jnp.max(e2, axis=-1, keepdims=True)
    i2 = jnp.min(jnp.where(e2 == v2, lane, ROUTER_LANES), axis=-1, keepdims=True)
    ex = jnp.exp(v2 - v1)
    w1 = 1.0 / (1.0 + ex)
    w2 = ex * w1
    gate = jnp.where(lane == i1, p_sel * w1, jnp.where(lane == i2, p_sel * w2, 0.0))
    return gate, gidx


def _moe_route_kernel(x_ref, n2_ref, wr_ref, br_ref, ts_ref, gs_ref, slot_ref, cnt_ref):
    rows = x_ref.shape[0]
    x = x_ref[...]
    t = _rms(x, n2_ref[...])
    t_hi = t.astype(BF16)
    t_lo = (t - t_hi.astype(F32)).astype(BF16)
    wr = wr_ref[...]
    wr_hi = wr.astype(BF16)
    wr_lo = (wr - wr_hi.astype(F32)).astype(BF16)
    logits = (jnp.dot(t_hi, wr_hi, preferred_element_type=F32)
              + jnp.dot(t_lo, wr_hi, preferred_element_type=F32)
              + jnp.dot(t_hi, wr_lo, preferred_element_type=F32)) + br_ref[...]
    gate, gidx = _route(logits)

    lane = lax.broadcasted_iota(jnp.int32, (rows, ROUTER_LANES), 1)
    onehot = jnp.where(lane == gidx, 1.0, 0.0)
    r_i = lax.broadcasted_iota(jnp.int32, (rows, rows), 0)
    c_i = lax.broadcasted_iota(jnp.int32, (rows, rows), 1)
    tri = jnp.where(c_i < r_i, 1.0, 0.0).astype(BF16)
    before = jnp.dot(tri, onehot.astype(BF16), preferred_element_type=F32)
    rank = jnp.sum(before * onehot, axis=-1, keepdims=True)
    counts = jnp.sum(onehot, axis=0, keepdims=True)

    units = jnp.ceil(counts * (1.0 / MOE_UNIT))
    first = MOE_UNIT * (pltpu.roll(units, 1, 1) + pltpu.roll(units, 2, 1) + pltpu.roll(units, 3, 1))
    slot = jnp.sum(onehot * first, axis=-1, keepdims=True) + rank
    slot_b = jnp.broadcast_to(slot, (rows, ROUTER_LANES))
    slot_t = jnp.transpose(slot_b)[0:1, :]
    s_sub = lax.broadcasted_iota(jnp.int32, (MOE_TILE_SLOTS, rows), 0).astype(F32)
    perm = jnp.where(s_sub == slot_t, 1.0, 0.0).astype(BF16)

    ts_ref[...] = jnp.dot(perm, t_hi, preferred_element_type=F32).astype(BF16)
    gate4 = jnp.zeros_like(gate)
    for gi in range(N_GROUPS):
        shift = ROUTER_LANES - (N_GROUPS + EXP_PER_GROUP * gi)
        gate4 = gate4 + jnp.where(gidx == gi, pltpu.roll(gate, shift, 1), 0.0)
    g_hi = gate4.astype(BF16).astype(F32)
    packed = (g_hi + pltpu.roll(gate4 - g_hi, ROUTER_LANES // 2, 1)).astype(BF16)
    gs = jnp.dot(perm, packed, preferred_element_type=F32)
    gs_ref[...] = gs + pltpu.roll(gs, ROUTER_LANES // 2, 1)
    slot_ref[...] = slot_b
    cnt_ref[...] = jnp.broadcast_to(counts, cnt_ref.shape)


def _moe_expert_kernel(src_ref, grp_ref, nt_ref, *refs):
    del src_ref, grp_ref
    ts_units = refs[0:MOE_MTILE_UNITS]
    gs_units = refs[MOE_MTILE_UNITS:2 * MOE_MTILE_UNITS]
    wg_ref, wu_ref, wd_ref, out_ref = refs[2 * MOE_MTILE_UNITS:]
    j = pl.program_id(0)

    @pl.when(j < nt_ref[0])
    def _compute():
        ts = jnp.concatenate([u[...] for u in ts_units], axis=0)
        gs = jnp.concatenate([u[...] for u in gs_units], axis=0)
        acc = None
        for e in range(EXP_PER_GROUP):
            hg = jnp.dot(ts, wg_ref[e], preferred_element_type=F32)
            hu = jnp.dot(ts, wu_ref[e], preferred_element_type=F32)
            act = (hg * jax.nn.sigmoid(hg) * hu * gs[:, e:e + 1]).astype(BF16)
            o = jnp.dot(act, wd_ref[e], preferred_element_type=F32)
            acc = o if acc is None else acc + o
        out_ref[...] = acc.astype(BF16)

    @pl.when(j >= nt_ref[0])
    def _idle():
        out_ref[...] = jnp.zeros_like(out_ref)


def _moe_combine_kernel(final, pos_ref, x_ref, slot_ref, *refs):
    del pos_ref
    units = refs[0:MOE_TILE_UNITS]
    fn_ref, o_ref = refs[MOE_TILE_UNITS:]
    rows = x_ref.shape[0]
    os_local = jnp.concatenate([u[...] for u in units], axis=0)
    s_lane = lax.broadcasted_iota(jnp.int32, (rows, MOE_TILE_SLOTS), 1).astype(F32)
    perm_t = jnp.where(s_lane == slot_ref[:, 0:1], 1.0, 0.0).astype(BF16)
    y = x_ref[...] + jnp.dot(perm_t, os_local, preferred_element_type=F32)
    if final:
        y = _rms(y, fn_ref[...])
    o_ref[...] = y


def _moe_plan(counts, n_tiles, n_mtiles):
    i32 = jnp.int32
    units = (counts + (MOE_UNIT - 1)) // MOE_UNIT
    loc_off = jnp.cumsum(units, axis=1) - units
    grp_tiles = (jnp.sum(units, axis=0) + (MOE_MTILE_UNITS - 1)) // MOE_MTILE_UNITS
    grp_end = jnp.cumsum(grp_tiles)
    tile_start = grp_end - grp_tiles
    before_c = jnp.cumsum(units, axis=0) - units
    u = jnp.arange(MOE_TILE_UNITS, dtype=i32)
    g_of = jnp.sum((u[None, :, None] >= (loc_off + units)[:, None, :]).astype(i32), axis=-1)
    valid = g_of < N_GROUPS
    g_c = jnp.minimum(g_of, N_GROUPS - 1)
    k = u[None, :] - jnp.take_along_axis(loc_off, g_c, axis=1)
    pos = MOE_MTILE_UNITS * tile_start[g_c] + jnp.take_along_axis(before_c, g_c, axis=1) + k
    pos = jnp.where(valid, pos, 0).astype(i32)
    local_id = (jnp.arange(n_tiles, dtype=i32)[:, None] * MOE_TILE_UNITS + u[None, :])
    n_pos = n_mtiles * MOE_MTILE_UNITS
    src = jnp.zeros((n_pos,), i32).at[jnp.where(valid, pos, n_pos).reshape(-1)].set(
        local_id.reshape(-1), mode="drop")
    j = jnp.arange(n_mtiles, dtype=i32)
    grp = jnp.minimum(jnp.sum((j[:, None] >= grp_end[None, :]).astype(i32), axis=-1), N_GROUPS - 1)
    return src, grp.astype(i32), grp_end[-1:].astype(i32), pos.reshape(-1)


def _moe(x1, w, final):
    n_rows = x1.shape[0]
    rows = CHUNK_ROWS
    assert n_rows % rows == 0
    n_tiles = n_rows // rows
    n_mtiles = -(-(n_tiles * MOE_TILE_UNITS) // MOE_MTILE_UNITS) + N_GROUPS
    mrows = MOE_MTILE_UNITS * MOE_UNIT
    cparams = pltpu.CompilerParams(dimension_semantics=("arbitrary",), vmem_limit_bytes=VMEM_LIMIT_BYTES)

    row_spec = pl.BlockSpec((rows, D_MODEL), lambda i: (i, 0))
    ts, gs, slot, cnt = pl.pallas_call(
        _moe_route_kernel,
        grid=(n_tiles,),
        in_specs=[row_spec, _const_spec(w["norm2"].shape), _const_spec(w["moe_w_r"].shape),
                  _const_spec(w["moe_b_r"].shape)],
        out_specs=[pl.BlockSpec((MOE_TILE_SLOTS, D_MODEL), lambda i: (i, 0)),
                   pl.BlockSpec((MOE_TILE_SLOTS, ROUTER_LANES), lambda i: (i, 0)),
                   pl.BlockSpec((rows, ROUTER_LANES), lambda i: (i, 0)),
                   pl.BlockSpec((8, ROUTER_LANES), lambda i: (i, 0))],
        out_shape=[jax.ShapeDtypeStruct((n_tiles * MOE_TILE_SLOTS, D_MODEL), BF16),
                   jax.ShapeDtypeStruct((n_tiles * MOE_TILE_SLOTS, ROUTER_LANES), F32),
                   jax.ShapeDtypeStruct((n_rows, ROUTER_LANES), F32),
                   jax.ShapeDtypeStruct((n_tiles * 8, ROUTER_LANES), F32)],
        compiler_params=cparams,
        name="moe_route",
    )(x1, w["norm2"], w["moe_w_r"], w["moe_b_r"])

    counts = cnt.reshape(n_tiles, 8, ROUTER_LANES)[:, 0, :N_GROUPS].astype(jnp.int32)
    src, grp, n_used, pos = _moe_plan(counts, n_tiles, n_mtiles)

    def unit_in(width, k):
        return pl.BlockSpec((MOE_UNIT, width), lambda j, src, grp, nt: (src[j * MOE_MTILE_UNITS + k], 0))

    def grp_w(shape):
        return pl.BlockSpec((EXP_PER_GROUP,) + shape, lambda j, src, grp, nt: (grp[j], 0, 0))
    os_sorted = pl.pallas_call(
        _moe_expert_kernel,
        grid_spec=pltpu.PrefetchScalarGridSpec(
            num_scalar_prefetch=3,
            grid=(n_mtiles,),
            in_specs=[unit_in(D_MODEL, k) for k in range(MOE_MTILE_UNITS)]
            + [unit_in(ROUTER_LANES, k) for k in range(MOE_MTILE_UNITS)]
            + [grp_w((D_MODEL, EXPERT_FF)), grp_w((D_MODEL, EXPERT_FF)), grp_w((EXPERT_FF, D_MODEL))],
            out_specs=pl.BlockSpec((mrows, D_MODEL), lambda j, src, grp, nt: (j, 0)),
        ),
        out_shape=jax.ShapeDtypeStruct((n_mtiles * mrows, D_MODEL), BF16),
        compiler_params=cparams,
        name="moe_experts",
    )(src, grp, n_used, *([ts] * MOE_MTILE_UNITS), *([gs] * MOE_MTILE_UNITS),
      w["moe_w_gate"], w["moe_w_up"], w["moe_w_down"])

    def unit_out(u):
        return pl.BlockSpec((MOE_UNIT, D_MODEL), lambda i, pos: (pos[i * MOE_TILE_UNITS + u], 0))
    return pl.pallas_call(
        functools.partial(_moe_combine_kernel, final),
        grid_spec=pltpu.PrefetchScalarGridSpec(
            num_scalar_prefetch=1,
            grid=(n_tiles,),
            in_specs=[pl.BlockSpec((rows, D_MODEL), lambda i, pos: (i, 0)),
                      pl.BlockSpec((rows, ROUTER_LANES), lambda i, pos: (i, 0))]
            + [unit_out(u) for u in range(MOE_TILE_UNITS)]
            + [pl.BlockSpec(w["final_norm"].shape, lambda i, pos: (0, 0))],
            out_specs=pl.BlockSpec((rows, D_MODEL), lambda i, pos: (i, 0)),
        ),
        out_shape=jax.ShapeDtypeStruct((n_rows, D_MODEL), F32),
        compiler_params=cparams,
        name="moe_combine",
    )(pos, x1, slot, *([os_sorted] * MOE_TILE_UNITS), w["final_norm"])


def _block_diag(blocks):
    L, n, r, c = blocks.shape
    eye = jnp.eye(n, dtype=blocks.dtype)
    return (blocks[:, :, :, None, :] * eye[None, :, None, :, None]).reshape(L, n * r, n * c)


def _to_tm(a):
    a = jnp.swapaxes(a, 0, 1)
    return a.reshape((a.shape[0] * a.shape[1],) + a.shape[2:])


def _from_tm(a, b):
    return jnp.swapaxes(a.reshape((a.shape[0] // b, b) + a.shape[1:]), 0, 1)


def kernel(x_prompt, x_sample, state_rglru_h, state_rglru_conv, state_conf_conv, state_s5_re, state_s5_im, norm1, w_in, rg_conv_w, rg_conv_b, rg_wa, rg_ba, rg_wx, rg_bx, rg_lambda, cf_conv_w, cf_conv_b, cf_ln_g, cf_ln_b, s5_a_re, s5_a_im, s5_log_dt, s5_b_re, s5_b_im, s5_c_re, s5_c_im, s5_d, s5_w_glu, s5_b_glu, gn_rg, gn_cf, gn_s5, w_out, norm2, moe_w_grp, moe_b_grp, moe_w_exp, moe_b_exp, moe_w_gate, moe_w_up, moe_w_down, final_norm):
    L = w_in.shape[0]
    B, T, _ = x_prompt.shape
    DB, DT, _ = x_sample.shape

    ab_re, ab_im, bb_re, bb_im = _s5_prep(s5_a_re, s5_a_im, s5_log_dt,
                                          jnp.swapaxes(s5_b_re, 2, 3), jnp.swapaxes(s5_b_im, 2, 3))

    def row(a):
        return a.reshape(L, 1, -1)

    W = dict(
        norm1=row(norm1), w_in=w_in.astype(BF16),
        rg_conv_w=rg_conv_w, rg_conv_b=row(rg_conv_b),
        rg_wg=jnp.concatenate([_block_diag(rg_wa), _block_diag(rg_wx)], axis=-1).astype(BF16),
        rg_bg=jnp.concatenate([row(rg_ba), row(rg_bx)], axis=-1),
        rg_lambda=row(rg_lambda),
        cf_conv_w=cf_conv_w, cf_conv_b=row(cf_conv_b), cf_ln_g=row(cf_ln_g), cf_ln_b=row(cf_ln_b),
        s5_ab_re=row(ab_re), s5_ab_im=row(ab_im),
        s5_bb_re=_block_diag(bb_re).astype(BF16), s5_bb_im=_block_diag(bb_im).astype(BF16),
        s5_c_re=_block_diag(jnp.swapaxes(s5_c_re, 2, 3)).astype(BF16),
        s5_c_im=_block_diag(jnp.swapaxes(s5_c_im, 2, 3)).astype(BF16),
        s5_d=row(s5_d), s5_w_glu=s5_w_glu.astype(BF16), s5_b_glu=row(s5_b_glu),
        gn_rg=row(gn_rg), gn_cf=row(gn_cf), gn_s5=row(gn_s5), w_out=w_out.astype(BF16),
        norm2=row(norm2),
        moe_w_r=jnp.concatenate(
            [moe_w_grp, moe_w_exp.reshape(L, D_MODEL, N_EXPERTS),
             jnp.zeros((L, D_MODEL, ROUTER_LANES - N_GROUPS - N_EXPERTS), F32)], axis=-1),
        moe_b_r=jnp.concatenate(
            [moe_b_grp, moe_b_exp.reshape(L, N_EXPERTS),
             jnp.zeros((L, ROUTER_LANES - N_GROUPS - N_EXPERTS), F32)], axis=-1).reshape(L, 1, ROUTER_LANES),
        moe_w_gate=moe_w_gate.astype(BF16), moe_w_up=moe_w_up.astype(BF16),
        moe_w_down=moe_w_down.astype(BF16),
    )
    fnorm = final_norm.reshape(1, D_MODEL)

    xp = _to_tm(x_prompt)
    xs = _to_tm(x_sample)
    tc_p = CHUNK_ROWS // B
    p_states, s_states = [], []
    for l in range(L):
        lw = {k: v[l] for k, v in W.items()}
        lw["final_norm"] = fnorm
        st_p = (jnp.zeros((B, D_RNN), F32), jnp.zeros(((RG_CONV - 1) * B, D_RNN), F32),
                jnp.zeros(((CF_CONV - 1) * B, CF_W), F32), jnp.zeros((B, S5_N), F32),
                jnp.zeros((B, S5_N), F32))
        st_s = (state_rglru_h[l], _to_tm(state_rglru_conv[l]), _to_tm(state_conf_conv[l]),
                state_s5_re[l].reshape(DB, S5_N), state_s5_im[l].reshape(DB, S5_N))
        xp, *np_l = _mixer(xp, B, tc_p, st_p, lw)
        xs, *ns_l = _mixer(xs, DB, DT, st_s, lw)
        final = l == L - 1
        xp = _moe(xp, lw, final)
        xs = _moe(xs, lw, final)
        p_states.append(np_l)
        s_states.append(ns_l)

    def assemble(states, b):
        h = jnp.stack([s[0] for s in states])
        rc = jnp.stack([_from_tm(s[1], b) for s in states])
        cf = jnp.stack([_from_tm(s[2], b) for s in states])
        sre = jnp.stack([s[3].reshape(b, S5_GROUPS, S5_STATE) for s in states])
        sim = jnp.stack([s[4].reshape(b, S5_GROUPS, S5_STATE) for s in states])
        return h, rc, cf, sre, sim

    y_prompt = _from_tm(xp, B)
    y_sample = _from_tm(xs, DB)
    return (y_prompt, y_sample) + assemble(p_states, B) + assemble(s_states, DB)
```

```python
import functools

import jax
import jax.numpy as jnp
from jax import lax
from jax.experimental import pallas as pl
from jax.experimental.pallas import tpu as pltpu

F32 = jnp.float32
BF16 = jnp.bfloat16

D_MODEL = 1024
D_RNN = 512
RG_HEADS = 8
RG_CONV = 4
RG_C = 8.0
CF_W = 256
CF_CONV = 31
S5_W = 256
S5_GROUPS = 16
S5_GROUP_CH = 16
S5_STATE = 64
S5_N = S5_GROUPS * S5_STATE
P_IN = 2 * D_RNN + 2 * CF_W + S5_W
N_GROUPS = 4
EXP_PER_GROUP = 4
N_EXPERTS = 16
EXPERT_FF = 256
EPS = 1e-6

ROUTER_LANES = 128
VMEM_LIMIT_BYTES = 60000 * 1024
ROW_BLOCK = 64
CHUNK_ROWS = 512
MOE_UNIT = 16
MOE_TILE_UNITS = CHUNK_ROWS // MOE_UNIT + N_GROUPS
MOE_TILE_SLOTS = MOE_TILE_UNITS * MOE_UNIT
MOE_MTILE_UNITS = 16


def _rms(x, g):
    return x * lax.rsqrt(jnp.mean(x * x, axis=-1, keepdims=True) + EPS) * g


def _softplus(z):
    return jnp.maximum(z, 0.0) + jnp.log1p(jnp.exp(-jnp.abs(z)))


def _sigmoid(z):
    return 0.5 * jnp.tanh(0.5 * z) + 0.5


def _row_blocks(rows, rb, fn, unroll=False):
    if unroll:
        for i in range(rows // rb):
            fn(i * rb)
        return

    def body(i, carry):
        fn(pl.multiple_of(i * rb, rb))
        return carry
    lax.fori_loop(0, rows // rb, body, 0)


def _s5_prep_kernel(are_ref, aim_ref, ldt_ref, bre_ref, bim_ref,
                    abre_ref, abim_ref, bbre_ref, bbim_ref):
    a_re = are_ref[...]
    a_im = aim_ref[...]
    dt = jnp.exp(ldt_ref[...])
    mag = jnp.exp(a_re * dt)
    ab_re = mag * jnp.cos(a_im * dt)
    ab_im = mag * jnp.sin(a_im * dt)
    den = a_re * a_re + a_im * a_im
    nr = ab_re - 1.0
    ni = ab_im
    k_re = (nr * a_re + ni * a_im) / den
    k_im = (ni * a_re - nr * a_im) / den
    abre_ref[...] = ab_re
    abim_ref[...] = ab_im
    b_re = bre_ref[...]
    b_im = bim_ref[...]
    bbre_ref[...] = k_re[:, None, :] * b_re - k_im[:, None, :] * b_im
    bbim_ref[...] = k_re[:, None, :] * b_im + k_im[:, None, :] * b_re


def _s5_prep(a_re, a_im, log_dt, b_re_ghp, b_im_ghp):
    L = a_re.shape[0]
    gp = pl.BlockSpec((None, S5_GROUPS, S5_STATE), lambda l: (l, 0, 0))
    ghp = pl.BlockSpec((None, S5_GROUPS, S5_GROUP_CH, S5_STATE), lambda l: (l, 0, 0, 0))
    return pl.pallas_call(
        _s5_prep_kernel,
        grid=(L,),
        in_specs=[gp, gp, pl.BlockSpec((None, S5_GROUPS, 1), lambda l: (l, 0, 0)), ghp, ghp],
        out_specs=[gp, gp, ghp, ghp],
        out_shape=[jax.ShapeDtypeStruct((L, S5_GROUPS, S5_STATE), F32)] * 2
        + [jax.ShapeDtypeStruct((L, S5_GROUPS, S5_GROUP_CH, S5_STATE), F32)] * 2,
        name="s5_prep",
    )(a_re, a_im, log_dt[..., None], b_re_ghp, b_im_ghp)


def _mixer_kernel(bt, tc, n_chunks,
                  x_ref, h0_ref, rb_ref, fb_ref, sre_ref, sim_ref,
                  n1_ref, win_ref, rgw_ref, rgb_ref, wg_ref, bg_ref, lam_ref,
                  cfw_ref, cfb_ref, lng_ref, lnb_ref,
                  abre_ref, abim_ref, bbre_ref, bbim_ref, cre_ref, cim_ref, d_ref,
                  wglu_ref, bglu_ref, gnrg_ref, gncf_ref, gns5_ref, wout_ref,
                  x1_ref, hl_ref, rbn_ref, fbn_ref, sren_ref, simn_ref,
                  xpad, upad, h_sc, sre_sc, sim_sc, hn_sc, proj_sc, xc_sc, gate_sc,
                  a_sc, b_sc, v_sc, y_sc, zb_sc, gl_sc, bre_sc, bim_sc, mix_sc):
    rows = tc * bt
    rb = ROW_BLOCK
    rg_hist = (RG_CONV - 1) * bt
    cf_hist = (CF_CONV - 1) * bt
    c = pl.program_id(0)

    @pl.when(c == 0)
    def _init():
        xpad[0:rg_hist, :] = rb_ref[...]
        upad[0:cf_hist, :] = fb_ref[...]
        h_sc[...] = h0_ref[...]
        sre_sc[...] = sre_ref[...]
        sim_sc[...] = sim_ref[...]

    o_cf = 2 * D_RNN
    o_s5 = o_cf + 2 * CF_W

    def norm_in(r0):
        x = x_ref[r0:r0 + rb, :]
        hn_sc[r0:r0 + rb, :] = _rms(x, n1_ref[...]).astype(BF16)
    _row_blocks(rows, rb, norm_in, unroll=True)
    proj_sc[...] = jnp.dot(hn_sc[...], win_ref[...], preferred_element_type=F32)

    xpad[rg_hist:rg_hist + rows, :] = proj_sc[:, 0:D_RNN]

    def rg_conv(r0):
        acc = rgb_ref[...] + rgw_ref[0:1, :] * xpad[r0:r0 + rb, :]
        for k in range(1, RG_CONV):
            acc = acc + rgw_ref[k:k + 1, :] * xpad[r0 + k * bt:r0 + k * bt + rb, :]
        xc_sc[r0:r0 + rb, :] = acc
    _row_blocks(rows, rb, rg_conv, unroll=True)

    gate_sc[...] = jnp.dot(xc_sc[...].astype(BF16), wg_ref[...],
                           preferred_element_type=F32) + bg_ref[...]

    su_bf = proj_sc[:, o_s5:o_s5 + S5_W].astype(BF16)
    bre_sc[...] = jnp.dot(su_bf, bbre_ref[...], preferred_element_type=F32)
    bim_sc[...] = jnp.dot(su_bf, bbim_ref[...], preferred_element_type=F32)

    neg_c_sp = -RG_C * _softplus(-lam_ref[...])

    def rg_coeffs(r0):
        g = gate_sc[r0:r0 + rb, :]
        r = _sigmoid(g[:, 0:D_RNN])
        i = _sigmoid(g[:, D_RNN:2 * D_RNN])
        a = jnp.exp(neg_c_sp * r)
        a_sc[r0:r0 + rb, :] = a
        b_sc[r0:r0 + rb, :] = jnp.sqrt(1.0 - a * a) * (i * xc_sc[r0:r0 + rb, :])
    _row_blocks(rows, rb, rg_coeffs, unroll=True)

    def cf_glu(r0):
        ca = proj_sc[r0:r0 + rb, o_cf:o_cf + CF_W]
        cg = proj_sc[r0:r0 + rb, o_cf + CF_W:o_cf + 2 * CF_W]
        upad[cf_hist + r0:cf_hist + r0 + rb, :] = ca * _sigmoid(cg)
    _row_blocks(rows, rb, cf_glu, unroll=True)

    def rg_step(t, h):
        r0 = pl.multiple_of(t * bt, bt)
        h = a_sc[pl.ds(r0, bt), :] * h + b_sc[pl.ds(r0, bt), :]
        b_sc[pl.ds(r0, bt), :] = h
        return h
    h_last = lax.fori_loop(0, tc, rg_step, h_sc[...], unroll=min(tc, 8))
    h_sc[...] = h_last
    hl_ref[...] = h_last

    a_r = jnp.broadcast_to(abre_ref[...], (bt, S5_N))
    a_i = jnp.broadcast_to(abim_ref[...], (bt, S5_N))

    def s5_step(t, carry):
        s_r, s_i = carry
        r0 = pl.multiple_of(t * bt, bt)
        n_r = a_r * s_r - a_i * s_i + bre_sc[pl.ds(r0, bt), :]
        n_i = a_r * s_i + a_i * s_r + bim_sc[pl.ds(r0, bt), :]
        bre_sc[pl.ds(r0, bt), :] = n_r
        bim_sc[pl.ds(r0, bt), :] = n_i
        return n_r, n_i
    s_r, s_i = lax.fori_loop(0, tc, s5_step, (sre_sc[...], sim_sc[...]), unroll=min(tc, 4))
    sre_sc[...] = s_r
    sim_sc[...] = s_i
    sren_ref[...] = s_r
    simn_ref[...] = s_i

    def cf_conv(r0):
        acc = cfb_ref[...] + cfw_ref[0:1, :] * upad[pl.ds(r0, rb), :]
        for k in range(1, CF_CONV):
            acc = acc + cfw_ref[k:k + 1, :] * upad[pl.ds(r0 + k * bt, rb), :]
        v_sc[pl.ds(r0, rb), :] = acc
    _row_blocks(rows, rb, cf_conv)

    def rg_out(r0):
        h = b_sc[r0:r0 + rb, :]
        yb = proj_sc[r0:r0 + rb, D_RNN:2 * D_RNN]
        mix_sc[r0:r0 + rb, 0:D_RNN] = _rms(h * jax.nn.gelu(yb), gnrg_ref[...]).astype(BF16)
    _row_blocks(rows, rb, rg_out, unroll=True)

    def cf_out(r0):
        v = v_sc[r0:r0 + rb, :]
        mu = jnp.mean(v, axis=-1, keepdims=True)
        cen = v - mu
        var = jnp.mean(cen * cen, axis=-1, keepdims=True)
        vn = cen * lax.rsqrt(var + EPS) * lng_ref[...] + lnb_ref[...]
        mix_sc[r0:r0 + rb, D_RNN:D_RNN + CF_W] = _rms(vn * _sigmoid(vn), gncf_ref[...]).astype(BF16)
    _row_blocks(rows, rb, cf_out, unroll=True)

    y_sc[...] = (jnp.dot(bre_sc[...].astype(BF16), cre_ref[...], preferred_element_type=F32)
                 - jnp.dot(bim_sc[...].astype(BF16), cim_ref[...], preferred_element_type=F32))

    def s5_gelu(r0):
        su = proj_sc[r0:r0 + rb, o_s5:o_s5 + S5_W]
        z = jax.nn.gelu(y_sc[r0:r0 + rb, :] + d_ref[...] * su)
        y_sc[r0:r0 + rb, :] = z
        zb_sc[r0:r0 + rb, :] = z.astype(BF16)
    _row_blocks(rows, rb, s5_gelu, unroll=True)
    gl_sc[...] = jnp.dot(zb_sc[...], wglu_ref[...], preferred_element_type=F32) + bglu_ref[...]

    def s5_out(r0):
        oc = y_sc[r0:r0 + rb, :] * _sigmoid(gl_sc[r0:r0 + rb, :])
        mix_sc[r0:r0 + rb, D_RNN + CF_W:D_MODEL] = _rms(oc, gns5_ref[...]).astype(BF16)
    _row_blocks(rows, rb, s5_out, unroll=True)

    x1_ref[...] = x_ref[...] + jnp.dot(mix_sc[...], wout_ref[...], preferred_element_type=F32)

    rbn_ref[...] = xpad[rows:rows + rg_hist, :]
    fbn_ref[...] = upad[rows:rows + cf_hist, :]
    if n_chunks > 1:
        xpad[0:rg_hist, :] = xpad[rows:rows + rg_hist, :]
        upad[0:cf_hist, :] = upad[rows:rows + cf_hist, :]


def _const_spec(shape):
    nd = len(shape)
    return pl.BlockSpec(shape, lambda *_: (0,) * nd)


def _layer_spec(a, l):
    nd = a.ndim - 1
    return pl.BlockSpec((None,) + a.shape[1:], lambda *_: (l,) + (0,) * nd)


def _mixer(x_tm, bt, tc, state, w, l):
    n_rows = x_tm.shape[0]
    rows = tc * bt
    n_chunks = n_rows // rows
    assert n_chunks * rows == n_rows and rows % ROW_BLOCK == 0 and bt % 8 == 0
    assert n_chunks == 1 or tc >= CF_CONV - 1
    h0, rbuf, fbuf, s_re, s_im = state
    weights = [w["norm1"], w["w_in"], w["rg_conv_w"], w["rg_conv_b"], w["rg_wg"], w["rg_bg"],
               w["rg_lambda"], w["cf_conv_w"], w["cf_conv_b"], w["cf_ln_g"], w["cf_ln_b"],
               w["s5_ab_re"], w["s5_ab_im"], w["s5_bb_re"], w["s5_bb_im"], w["s5_c_re"], w["s5_c_im"],
               w["s5_d"], w["s5_w_glu"], w["s5_b_glu"], w["gn_rg"], w["gn_cf"], w["gn_s5"], w["w_out"]]
    states = [h0, rbuf, fbuf, s_re, s_im]
    row_spec = pl.BlockSpec((rows, D_MODEL), lambda c: (c, 0))
    in_specs = ([row_spec] + [_const_spec(a.shape) for a in states]
                + [_layer_spec(a, l) for a in weights])
    out_shape = [jax.ShapeDtypeStruct((n_rows, D_MODEL), F32)] + [
        jax.ShapeDtypeStruct(a.shape, F32) for a in states]
    out_specs = [row_spec] + [_const_spec(a.shape) for a in states]
    scratch = [
        pltpu.VMEM(((tc + RG_CONV - 1) * bt, D_RNN), F32),
        pltpu.VMEM(((tc + CF_CONV - 1) * bt, CF_W), F32),
        pltpu.VMEM((bt, D_RNN), F32),
        pltpu.VMEM((bt, S5_N), F32),
        pltpu.VMEM((bt, S5_N), F32),
        pltpu.VMEM((rows, D_MODEL), BF16),
        pltpu.VMEM((rows, P_IN), F32),
        pltpu.VMEM((rows, D_RNN), F32),
        pltpu.VMEM((rows, 2 * D_RNN), F32),
        pltpu.VMEM((rows, D_RNN), F32),
        pltpu.VMEM((rows, D_RNN), F32),
        pltpu.VMEM((rows, CF_W), F32),
        pltpu.VMEM((rows, S5_W), F32),
        pltpu.VMEM((rows, S5_W), BF16),
        pltpu.VMEM((rows, S5_W), F32),
        pltpu.VMEM((rows, S5_N), F32),
        pltpu.VMEM((rows, S5_N), F32),
        pltpu.VMEM((rows, D_MODEL), BF16),
    ]
    return pl.pallas_call(
        functools.partial(_mixer_kernel, bt, tc, n_chunks),
        grid=(n_chunks,),
        in_specs=in_specs,
        out_specs=out_specs,
        out_shape=out_shape,
        scratch_shapes=scratch,
        compiler_params=pltpu.CompilerParams(dimension_semantics=("arbitrary",),
                                             vmem_limit_bytes=VMEM_LIMIT_BYTES),
        name="mixer",
    )(x_tm, *states, *weights)


def _route(logits):
    shape = logits.shape
    lane = lax.broadcasted_iota(jnp.int32, shape, 1)
    neg = jnp.float32(-jnp.inf)
    is_grp = lane < N_GROUPS
    gl = jnp.where(is_grp, logits, neg)
    gmax = jnp.max(gl, axis=-1, keepdims=True)
    gidx = jnp.min(jnp.where(gl == gmax, lane, ROUTER_LANES), axis=-1, keepdims=True)
    p_sel = 1.0 / jnp.sum(jnp.where(is_grp, jnp.exp(logits - gmax), 0.0), axis=-1, keepdims=True)
    lo = N_GROUPS + EXP_PER_GROUP * gidx
    in_grp = jnp.logical_and(lane >= lo, lane < lo + EXP_PER_GROUP)
    e1 = jnp.where(in_grp, logits, neg)
    v1 = jnp.max(e1, axis=-1, keepdims=True)
    i1 = jnp.min(jnp.where(e1 == v1, lane, ROUTER_LANES), axis=-1, keepdims=True)
    e2 = jnp.where(lane == i1, neg, e1)
    v2 = jnp.max(e2, axis=-1, keepdims=True)
    i2 = jnp.min(jnp.where(e2 == v2, lane, ROUTER_LANES), axis=-1, keepdims=True)
    ex = jnp.exp(v2 - v1)
    w1 = 1.0 / (1.0 + ex)
    w2 = ex * w1
    gate = jnp.where(lane == i1, p_sel * w1, jnp.where(lane == i2, p_sel * w2, 0.0))
    return gate, gidx


def _moe_route_kernel(x_ref, n2_ref, wr_ref, br_ref, ts_ref, gs_ref, slot_ref, cnt_ref):
    rows = x_ref.shape[0]
    x = x_ref[...]
    t = _rms(x, n2_ref[...])
    t_hi = t.astype(BF16)
    t_lo = (t - t_hi.astype(F32)).astype(BF16)
    wr = wr_ref[...]
    wr_hi = wr.astype(BF16)
    wr_lo = (wr - wr_hi.astype(F32)).astype(BF16)
    logits = (jnp.dot(t_hi, wr_hi, preferred_element_type=F32)
              + jnp.dot(t_lo, wr_hi, preferred_element_type=F32)
              + jnp.dot(t_hi, wr_lo, preferred_element_type=F32)) + br_ref[...]
    gate, gidx = _route(logits)

    lane = lax.broadcasted_iota(jnp.int32, (rows, ROUTER_LANES), 1)
    onehot = jnp.where(lane == gidx, 1.0, 0.0)
    r_i = lax.broadcasted_iota(jnp.int32, (rows, rows), 0)
    c_i = lax.broadcasted_iota(jnp.int32, (rows, rows), 1)
    tri = jnp.where(c_i < r_i, 1.0, 0.0).astype(BF16)
    before = jnp.dot(tri, onehot.astype(BF16), preferred_element_type=F32)
    rank = jnp.sum(before * onehot, axis=-1, keepdims=True)
    counts = jnp.sum(onehot, axis=0, keepdims=True)

    units = jnp.ceil(counts * (1.0 / MOE_UNIT))
    first = MOE_UNIT * (pltpu.roll(units, 1, 1) + pltpu.roll(units, 2, 1) + pltpu.roll(units, 3, 1))
    slot = jnp.sum(onehot * first, axis=-1, keepdims=True) + rank
    slot_b = jnp.broadcast_to(slot, (rows, ROUTER_LANES))
    slot_t = jnp.transpose(slot_b)[0:1, :]
    s_sub = lax.broadcasted_iota(jnp.int32, (MOE_TILE_SLOTS, rows), 0).astype(F32)
    perm = jnp.where(s_sub == slot_t, 1.0, 0.0).astype(BF16)

    ts_ref[...] = jnp.dot(perm, t_hi, preferred_element_type=F32).astype(BF16)
    gate4 = jnp.zeros_like(gate)
    for gi in range(N_GROUPS):
        shift = ROUTER_LANES - (N_GROUPS + EXP_PER_GROUP * gi)
        gate4 = gate4 + jnp.where(gidx == gi, pltpu.roll(gate, shift, 1), 0.0)
    g_hi = gate4.astype(BF16).astype(F32)
    packed = (g_hi + pltpu.roll(gate4 - g_hi, ROUTER_LANES // 2, 1)).astype(BF16)
    gs = jnp.dot(perm, packed, preferred_element_type=F32)
    gs_ref[...] = gs + pltpu.roll(gs, ROUTER_LANES // 2, 1)
    slot_ref[...] = slot_b
    cnt_ref[...] = jnp.broadcast_to(counts, cnt_ref.shape)


def _moe_expert_kernel(src_ref, grp_ref, nt_ref, *refs):
    del src_ref, grp_ref
    ts_units = refs[0:MOE_MTILE_UNITS]
    gs_units = refs[MOE_MTILE_UNITS:2 * MOE_MTILE_UNITS]
    wg_ref, wu_ref, wd_ref, out_ref = refs[2 * MOE_MTILE_UNITS:]
    j = pl.program_id(0)

    @pl.when(j < nt_ref[0])
    def _compute():
        ts = jnp.concatenate([u[...] for u in ts_units], axis=0)
        gs = jnp.concatenate([u[...] for u in gs_units], axis=0)
        acc = None
        for e in range(EXP_PER_GROUP):
            hg = jnp.dot(ts, wg_ref[e], preferred_element_type=F32)
            hu = jnp.dot(ts, wu_ref[e], preferred_element_type=F32)
            act = (hg * _sigmoid(hg) * hu * gs[:, e:e + 1]).astype(BF16)
            o = jnp.dot(act, wd_ref[e], preferred_element_type=F32)
            acc = o if acc is None else acc + o
        out_ref[...] = acc.astype(BF16)

    @pl.when(j >= nt_ref[0])
    def _idle():
        out_ref[...] = jnp.zeros_like(out_ref)


def _moe_combine_kernel(final, pos_ref, x_ref, slot_ref, *refs):
    del pos_ref
    units = refs[0:MOE_TILE_UNITS]
    fn_ref, o_ref = refs[MOE_TILE_UNITS:]
    rows = x_ref.shape[0]
    os_local = jnp.concatenate([u[...] for u in units], axis=0)
    s_lane = lax.broadcasted_iota(jnp.int32, (rows, MOE_TILE_SLOTS), 1).astype(F32)
    perm_t = jnp.where(s_lane == slot_ref[:, 0:1], 1.0, 0.0).astype(BF16)
    y = x_ref[...] + jnp.dot(perm_t, os_local, preferred_element_type=F32)
    if final:
        y = _rms(y, fn_ref[...])
    o_ref[...] = y


def _moe_plan(counts, n_tiles, n_mtiles):
    i32 = jnp.int32
    units = (counts + (MOE_UNIT - 1)) // MOE_UNIT
    loc_off = jnp.cumsum(units, axis=1) - units
    grp_tiles = (jnp.sum(units, axis=0) + (MOE_MTILE_UNITS - 1)) // MOE_MTILE_UNITS
    grp_end = jnp.cumsum(grp_tiles)
    tile_start = grp_end - grp_tiles
    before_c = jnp.cumsum(units, axis=0) - units
    u = jnp.arange(MOE_TILE_UNITS, dtype=i32)
    g_of = jnp.sum((u[None, :, None] >= (loc_off + units)[:, None, :]).astype(i32), axis=-1)
    valid = g_of < N_GROUPS
    g_c = jnp.minimum(g_of, N_GROUPS - 1)
    k = u[None, :] - jnp.take_along_axis(loc_off, g_c, axis=1)
    pos = MOE_MTILE_UNITS * tile_start[g_c] + jnp.take_along_axis(before_c, g_c, axis=1) + k
    pos = jnp.where(valid, pos, 0).astype(i32)
    local_id = (jnp.arange(n_tiles, dtype=i32)[:, None] * MOE_TILE_UNITS + u[None, :])
    n_pos = n_mtiles * MOE_MTILE_UNITS
    src = jnp.zeros((n_pos,), i32).at[jnp.where(valid, pos, n_pos).reshape(-1)].set(
        local_id.reshape(-1), mode="drop")
    j = jnp.arange(n_mtiles, dtype=i32)
    grp = jnp.minimum(jnp.sum((j[:, None] >= grp_end[None, :]).astype(i32), axis=-1), N_GROUPS - 1)
    return src, grp.astype(i32), grp_end[-1:].astype(i32), pos.reshape(-1)


def _moe(x1, w, l, final):
    n_rows = x1.shape[0]
    rows = CHUNK_ROWS
    assert n_rows % rows == 0
    n_tiles = n_rows // rows
    n_mtiles = -(-(n_tiles * MOE_TILE_UNITS) // MOE_MTILE_UNITS) + N_GROUPS
    mrows = MOE_MTILE_UNITS * MOE_UNIT
    cparams = pltpu.CompilerParams(dimension_semantics=("arbitrary",), vmem_limit_bytes=VMEM_LIMIT_BYTES)

    row_spec = pl.BlockSpec((rows, D_MODEL), lambda i: (i, 0))
    ts, gs, slot, cnt = pl.pallas_call(
        _moe_route_kernel,
        grid=(n_tiles,),
        in_specs=[row_spec, _layer_spec(w["norm2"], l), _layer_spec(w["moe_w_r"], l),
                  _layer_spec(w["moe_b_r"], l)],
        out_specs=[pl.BlockSpec((MOE_TILE_SLOTS, D_MODEL), lambda i: (i, 0)),
                   pl.BlockSpec((MOE_TILE_SLOTS, ROUTER_LANES), lambda i: (i, 0)),
                   pl.BlockSpec((rows, ROUTER_LANES), lambda i: (i, 0)),
                   pl.BlockSpec((8, ROUTER_LANES), lambda i: (i, 0))],
        out_shape=[jax.ShapeDtypeStruct((n_tiles * MOE_TILE_SLOTS, D_MODEL), BF16),
                   jax.ShapeDtypeStruct((n_tiles * MOE_TILE_SLOTS, ROUTER_LANES), F32),
                   jax.ShapeDtypeStruct((n_rows, ROUTER_LANES), F32),
                   jax.ShapeDtypeStruct((n_tiles * 8, ROUTER_LANES), F32)],
        compiler_params=cparams,
        name="moe_route",
    )(x1, w["norm2"], w["moe_w_r"], w["moe_b_r"])

    counts = cnt.reshape(n_tiles, 8, ROUTER_LANES)[:, 0, :N_GROUPS].astype(jnp.int32)
    src, grp, n_used, pos = _moe_plan(counts, n_tiles, n_mtiles)

    def unit_in(width, k):
        return pl.BlockSpec((MOE_UNIT, width), lambda j, src, grp, nt: (src[j * MOE_MTILE_UNITS + k], 0))

    def grp_w(shape):
        return pl.BlockSpec((None, EXP_PER_GROUP) + shape, lambda j, src, grp, nt: (l, grp[j], 0, 0))
    os_sorted = pl.pallas_call(
        _moe_expert_kernel,
        grid_spec=pltpu.PrefetchScalarGridSpec(
            num_scalar_prefetch=3,
            grid=(n_mtiles,),
            in_specs=[unit_in(D_MODEL, k) for k in range(MOE_MTILE_UNITS)]
            + [unit_in(ROUTER_LANES, k) for k in range(MOE_MTILE_UNITS)]
            + [grp_w((D_MODEL, EXPERT_FF)), grp_w((D_MODEL, EXPERT_FF)), grp_w((EXPERT_FF, D_MODEL))],
            out_specs=pl.BlockSpec((mrows, D_MODEL), lambda j, src, grp, nt: (j, 0)),
        ),
        out_shape=jax.ShapeDtypeStruct((n_mtiles * mrows, D_MODEL), BF16),
        compiler_params=cparams,
        name="moe_experts",
    )(src, grp, n_used, *([ts] * MOE_MTILE_UNITS), *([gs] * MOE_MTILE_UNITS),
      w["moe_w_gate"], w["moe_w_up"], w["moe_w_down"])

    def unit_out(u):
        return pl.BlockSpec((MOE_UNIT, D_MODEL), lambda i, pos: (pos[i * MOE_TILE_UNITS + u], 0))
    return pl.pallas_call(
        functools.partial(_moe_combine_kernel, final),
        grid_spec=pltpu.PrefetchScalarGridSpec(
            num_scalar_prefetch=1,
            grid=(n_tiles,),
            in_specs=[pl.BlockSpec((rows, D_MODEL), lambda i, pos: (i, 0)),
                      pl.BlockSpec((rows, ROUTER_LANES), lambda i, pos: (i, 0))]
            + [unit_out(u) for u in range(MOE_TILE_UNITS)]
            + [pl.BlockSpec(w["final_norm"].shape, lambda i, pos: (0, 0))],
            out_specs=pl.BlockSpec((rows, D_MODEL), lambda i, pos: (i, 0)),
        ),
        out_shape=jax.ShapeDtypeStruct((n_rows, D_MODEL), F32),
        compiler_params=cparams,
        name="moe_combine",
    )(pos, x1, slot, *([os_sorted] * MOE_TILE_UNITS), w["final_norm"])


def _block_diag(blocks):
    L, n, r, c = blocks.shape
    eye = jnp.eye(n, dtype=blocks.dtype)
    return (blocks[:, :, :, None, :] * eye[None, :, None, :, None]).reshape(L, n * r, n * c)


def _to_tm(a):
    a = jnp.swapaxes(a, 0, 1)
    return a.reshape((a.shape[0] * a.shape[1],) + a.shape[2:])


def _from_tm(a, b):
    return jnp.swapaxes(a.reshape((a.shape[0] // b, b) + a.shape[1:]), 0, 1)


def kernel(x_prompt, x_sample, state_rglru_h, state_rglru_conv, state_conf_conv, state_s5_re, state_s5_im, norm1, w_in, rg_conv_w, rg_conv_b, rg_wa, rg_ba, rg_wx, rg_bx, rg_lambda, cf_conv_w, cf_conv_b, cf_ln_g, cf_ln_b, s5_a_re, s5_a_im, s5_log_dt, s5_b_re, s5_b_im, s5_c_re, s5_c_im, s5_d, s5_w_glu, s5_b_glu, gn_rg, gn_cf, gn_s5, w_out, norm2, moe_w_grp, moe_b_grp, moe_w_exp, moe_b_exp, moe_w_gate, moe_w_up, moe_w_down, final_norm):
    L = w_in.shape[0]
    B, T, _ = x_prompt.shape
    DB, DT, _ = x_sample.shape

    ab_re, ab_im, bb_re, bb_im = _s5_prep(s5_a_re, s5_a_im, s5_log_dt,
                                          jnp.swapaxes(s5_b_re, 2, 3), jnp.swapaxes(s5_b_im, 2, 3))

    def row(a):
        return a.reshape(L, 1, -1)

    W = dict(
        norm1=row(norm1), w_in=w_in.astype(BF16),
        rg_conv_w=rg_conv_w, rg_conv_b=row(rg_conv_b),
        rg_wg=jnp.concatenate([_block_diag(rg_wa), _block_diag(rg_wx)], axis=-1).astype(BF16),
        rg_bg=jnp.concatenate([row(rg_ba), row(rg_bx)], axis=-1),
        rg_lambda=row(rg_lambda),
        cf_conv_w=cf_conv_w, cf_conv_b=row(cf_conv_b), cf_ln_g=row(cf_ln_g), cf_ln_b=row(cf_ln_b),
        s5_ab_re=row(ab_re), s5_ab_im=row(ab_im),
        s5_bb_re=_block_diag(bb_re).astype(BF16), s5_bb_im=_block_diag(bb_im).astype(BF16),
        s5_c_re=_block_diag(jnp.swapaxes(s5_c_re, 2, 3)).astype(BF16),
        s5_c_im=_block_diag(jnp.swapaxes(s5_c_im, 2, 3)).astype(BF16),
        s5_d=row(s5_d), s5_w_glu=s5_w_glu.astype(BF16), s5_b_glu=row(s5_b_glu),
        gn_rg=row(gn_rg), gn_cf=row(gn_cf), gn_s5=row(gn_s5), w_out=w_out.astype(BF16),
        norm2=row(norm2),
        moe_w_r=jnp.concatenate(
            [moe_w_grp, moe_w_exp.reshape(L, D_MODEL, N_EXPERTS),
             jnp.zeros((L, D_MODEL, ROUTER_LANES - N_GROUPS - N_EXPERTS), F32)], axis=-1),
        moe_b_r=jnp.concatenate(
            [moe_b_grp, moe_b_exp.reshape(L, N_EXPERTS),
             jnp.zeros((L, ROUTER_LANES - N_GROUPS - N_EXPERTS), F32)], axis=-1).reshape(L, 1, ROUTER_LANES),
        moe_w_gate=moe_w_gate.astype(BF16), moe_w_up=moe_w_up.astype(BF16),
        moe_w_down=moe_w_down.astype(BF16),
    )
    fnorm = final_norm.reshape(1, D_MODEL)

    xp = _to_tm(x_prompt)
    xs = _to_tm(x_sample)
    tc_p = CHUNK_ROWS // B
    p_states, s_states = [], []
    W["final_norm"] = final_norm.reshape(1, D_MODEL)
    for l in range(L):
        st_p = (jnp.zeros((B, D_RNN), F32), jnp.zeros(((RG_CONV - 1) * B, D_RNN), F32),
                jnp.zeros(((CF_CONV - 1) * B, CF_W), F32), jnp.zeros((B, S5_N), F32),
                jnp.zeros((B, S5_N), F32))
        st_s = (state_rglru_h[l], _to_tm(state_rglru_conv[l]), _to_tm(state_conf_conv[l]),
                state_s5_re[l].reshape(DB, S5_N), state_s5_im[l].reshape(DB, S5_N))
        xp, *np_l = _mixer(xp, B, tc_p, st_p, W, l)
        xs, *ns_l = _mixer(xs, DB, DT, st_s, W, l)
        final = l == L - 1
        xp = _moe(xp, W, l, final)
        xs = _moe(xs, W, l, final)
        p_states.append(np_l)
        s_states.append(ns_l)

    def assemble(states, b):
        h = jnp.stack([s[0] for s in states])
        rc = jnp.stack([_from_tm(s[1], b) for s in states])
        cf = jnp.stack([_from_tm(s[2], b) for s in states])
        sre = jnp.stack([s[3].reshape(b, S5_GROUPS, S5_STATE) for s in states])
        sim = jnp.stack([s[4].reshape(b, S5_GROUPS, S5_STATE) for s in states])
        return h, rc, cf, sre, sim

    y_prompt = _from_tm(xp, B)
    y_sample = _from_tm(xs, DB)
    return (y_prompt, y_sample) + assemble(p_states, B) + assemble(s_states, DB)
```

```python
import functools

import jax
import jax.numpy as jnp
from jax import lax
from jax.experimental import pallas as pl
from jax.experimental.pallas import tpu as pltpu

F32 = jnp.float32
BF16 = jnp.bfloat16

D_MODEL = 1024
D_RNN = 512
RG_HEADS = 8
RG_CONV = 4
RG_C = 8.0
CF_W = 256
CF_CONV = 31
S5_W = 256
S5_GROUPS = 16
S5_GROUP_CH = 16
S5_STATE = 64
S5_N = S5_GROUPS * S5_STATE
P_IN = 2 * D_RNN + 2 * CF_W + S5_W
N_GROUPS = 4
EXP_PER_GROUP = 4
N_EXPERTS = 16
EXPERT_FF = 256
EPS = 1e-6

ROUTER_LANES = 128
VMEM_LIMIT_BYTES = 60000 * 1024
ROW_BLOCK = 64
CHUNK_ROWS = 512
MOE_UNIT = 32
MOE_TILE_UNITS = CHUNK_ROWS // MOE_UNIT + N_GROUPS
MOE_TILE_SLOTS = MOE_TILE_UNITS * MOE_UNIT
MOE_MTILE_UNITS = 8
MOE_ROW_W = D_MODEL + ROUTER_LANES


def _rms(x, g):
    return x * lax.rsqrt(jnp.mean(x * x, axis=-1, keepdims=True) + EPS) * g


def _softplus(z):
    return jnp.maximum(z, 0.0) + jnp.log1p(jnp.exp(-jnp.abs(z)))


def _sigmoid(z):
    return 0.5 * jnp.tanh(0.5 * z) + 0.5


def _row_blocks(rows, rb, fn, unroll=False):
    if unroll:
        for i in range(rows // rb):
            fn(i * rb)
        return

    def body(i, carry):
        fn(pl.multiple_of(i * rb, rb))
        return carry
    lax.fori_loop(0, rows // rb, body, 0)


def _s5_prep_kernel(are_ref, aim_ref, ldt_ref, bre_ref, bim_ref,
                    abre_ref, abim_ref, bbre_ref, bbim_ref):
    a_re = are_ref[...]
    a_im = aim_ref[...]
    dt = jnp.exp(ldt_ref[...])
    mag = jnp.exp(a_re * dt)
    ab_re = mag * jnp.cos(a_im * dt)
    ab_im = mag * jnp.sin(a_im * dt)
    den = a_re * a_re + a_im * a_im
    nr = ab_re - 1.0
    ni = ab_im
    k_re = (nr * a_re + ni * a_im) / den
    k_im = (ni * a_re - nr * a_im) / den
    abre_ref[...] = ab_re
    abim_ref[...] = ab_im
    b_re = bre_ref[...]
    b_im = bim_ref[...]
    bbre_ref[...] = k_re[:, None, :] * b_re - k_im[:, None, :] * b_im
    bbim_ref[...] = k_re[:, None, :] * b_im + k_im[:, None, :] * b_re


def _s5_prep(a_re, a_im, log_dt, b_re_ghp, b_im_ghp):
    L = a_re.shape[0]
    gp = pl.BlockSpec((None, S5_GROUPS, S5_STATE), lambda l: (l, 0, 0))
    ghp = pl.BlockSpec((None, S5_GROUPS, S5_GROUP_CH, S5_STATE), lambda l: (l, 0, 0, 0))
    return pl.pallas_call(
        _s5_prep_kernel,
        grid=(L,),
        in_specs=[gp, gp, pl.BlockSpec((None, S5_GROUPS, 1), lambda l: (l, 0, 0)), ghp, ghp],
        out_specs=[gp, gp, ghp, ghp],
        out_shape=[jax.ShapeDtypeStruct((L, S5_GROUPS, S5_STATE), F32)] * 2
        + [jax.ShapeDtypeStruct((L, S5_GROUPS, S5_GROUP_CH, S5_STATE), F32)] * 2,
        name="s5_prep",
    )(a_re, a_im, log_dt[..., None], b_re_ghp, b_im_ghp)


def _mixer_kernel(bt, tc, n_chunks, x_bm,
                  x_ref, h0_ref, rb_ref, fb_ref, sre_ref, sim_ref,
                  n1_ref, win_ref, rgw_ref, rgb_ref, wg_ref, bg_ref, lam_ref,
                  cfw_ref, cfb_ref, lng_ref, lnb_ref,
                  abre_ref, abim_ref, bbre_ref, bbim_ref, cre_ref, cim_ref, d_ref,
                  wglu_ref, bglu_ref, gnrg_ref, gncf_ref, gns5_ref, wout_ref,
                  x1_ref, hl_ref, rbn_ref, fbn_ref, sren_ref, simn_ref,
                  xpad, upad, h_sc, sre_sc, sim_sc, hn_sc, proj_sc, xc_sc, gate_sc,
                  a_sc, b_sc, v_sc, y_sc, zb_sc, gl_sc, bre_sc, bim_sc, mix_sc, *maybe_xtm):
    rows = tc * bt
    rb = ROW_BLOCK
    rg_hist = (RG_CONV - 1) * bt
    cf_hist = (CF_CONV - 1) * bt
    c = pl.program_id(0)

    @pl.when(c == 0)
    def _init():
        xpad[0:rg_hist, :] = rb_ref[...]
        upad[0:cf_hist, :] = fb_ref[...]
        h_sc[...] = h0_ref[...]
        sre_sc[...] = sre_ref[...]
        sim_sc[...] = sim_ref[...]

    o_cf = 2 * D_RNN
    o_s5 = o_cf + 2 * CF_W

    if x_bm:
        (x_src,) = maybe_xtm
        for t in range(tc):
            x_src[t * bt:(t + 1) * bt, :] = x_ref[:, t, :]
    else:
        x_src = x_ref

    def norm_in(r0):
        x = x_src[r0:r0 + rb, :]
        hn_sc[r0:r0 + rb, :] = _rms(x, n1_ref[...]).astype(BF16)
    _row_blocks(rows, rb, norm_in, unroll=True)
    proj_sc[...] = jnp.dot(hn_sc[...], win_ref[...], preferred_element_type=F32)

    xpad[rg_hist:rg_hist + rows, :] = proj_sc[:, 0:D_RNN]

    def rg_conv(r0):
        acc = rgb_ref[...] + rgw_ref[0:1, :] * xpad[r0:r0 + rb, :]
        for k in range(1, RG_CONV):
            acc = acc + rgw_ref[k:k + 1, :] * xpad[r0 + k * bt:r0 + k * bt + rb, :]
        xc_sc[r0:r0 + rb, :] = acc
    _row_blocks(rows, rb, rg_conv, unroll=True)

    gate_sc[...] = jnp.dot(xc_sc[...].astype(BF16), wg_ref[...],
                           preferred_element_type=F32) + bg_ref[...]

    su_bf = proj_sc[:, o_s5:o_s5 + S5_W].astype(BF16)
    bre_sc[...] = jnp.dot(su_bf, bbre_ref[...], preferred_element_type=F32)
    bim_sc[...] = jnp.dot(su_bf, bbim_ref[...], preferred_element_type=F32)

    neg_c_sp = -RG_C * _softplus(-lam_ref[...])

    def rg_coeffs(r0):
        g = gate_sc[r0:r0 + rb, :]
        r = _sigmoid(g[:, 0:D_RNN])
        i = _sigmoid(g[:, D_RNN:2 * D_RNN])
        a = jnp.exp(neg_c_sp * r)
        a_sc[r0:r0 + rb, :] = a
        b_sc[r0:r0 + rb, :] = jnp.sqrt(1.0 - a * a) * (i * xc_sc[r0:r0 + rb, :])
    _row_blocks(rows, rb, rg_coeffs, unroll=True)

    def cf_glu(r0):
        ca = proj_sc[r0:r0 + rb, o_cf:o_cf + CF_W]
        cg = proj_sc[r0:r0 + rb, o_cf + CF_W:o_cf + 2 * CF_W]
        upad[cf_hist + r0:cf_hist + r0 + rb, :] = ca * _sigmoid(cg)
    _row_blocks(rows, rb, cf_glu, unroll=True)

    def rg_step(t, h):
        r0 = pl.multiple_of(t * bt, bt)
        h = a_sc[pl.ds(r0, bt), :] * h + b_sc[pl.ds(r0, bt), :]
        b_sc[pl.ds(r0, bt), :] = h
        return h
    h_last = lax.fori_loop(0, tc, rg_step, h_sc[...], unroll=min(tc, 8))
    h_sc[...] = h_last
    hl_ref[...] = h_last

    a_r = jnp.broadcast_to(abre_ref[...], (bt, S5_N))
    a_i = jnp.broadcast_to(abim_ref[...], (bt, S5_N))

    def s5_step(t, carry):
        s_r, s_i = carry
        r0 = pl.multiple_of(t * bt, bt)
        n_r = a_r * s_r - a_i * s_i + bre_sc[pl.ds(r0, bt), :]
        n_i = a_r * s_i + a_i * s_r + bim_sc[pl.ds(r0, bt), :]
        bre_sc[pl.ds(r0, bt), :] = n_r
        bim_sc[pl.ds(r0, bt), :] = n_i
        return n_r, n_i
    s_r, s_i = lax.fori_loop(0, tc, s5_step, (sre_sc[...], sim_sc[...]), unroll=min(tc, 4))
    sre_sc[...] = s_r
    sim_sc[...] = s_i
    sren_ref[...] = s_r
    simn_ref[...] = s_i

    def cf_conv(r0):
        acc = cfb_ref[...] + cfw_ref[0:1, :] * upad[pl.ds(r0, rb), :]
        for k in range(1, CF_CONV):
            acc = acc + cfw_ref[k:k + 1, :] * upad[pl.ds(r0 + k * bt, rb), :]
        v_sc[pl.ds(r0, rb), :] = acc
    _row_blocks(rows, rb, cf_conv)

    def rg_out(r0):
        h = b_sc[r0:r0 + rb, :]
        yb = proj_sc[r0:r0 + rb, D_RNN:2 * D_RNN]
        mix_sc[r0:r0 + rb, 0:D_RNN] = _rms(h * jax.nn.gelu(yb), gnrg_ref[...]).astype(BF16)
    _row_blocks(rows, rb, rg_out, unroll=True)

    def cf_out(r0):
        v = v_sc[r0:r0 + rb, :]
        mu = jnp.mean(v, axis=-1, keepdims=True)
        cen = v - mu
        var = jnp.mean(cen * cen, axis=-1, keepdims=True)
        vn = cen * lax.rsqrt(var + EPS) * lng_ref[...] + lnb_ref[...]
        mix_sc[r0:r0 + rb, D_RNN:D_RNN + CF_W] = _rms(vn * _sigmoid(vn), gncf_ref[...]).astype(BF16)
    _row_blocks(rows, rb, cf_out, unroll=True)

    y_sc[...] = (jnp.dot(bre_sc[...].astype(BF16), cre_ref[...], preferred_element_type=F32)
                 - jnp.dot(bim_sc[...].astype(BF16), cim_ref[...], preferred_element_type=F32))

    def s5_gelu(r0):
        su = proj_sc[r0:r0 + rb, o_s5:o_s5 + S5_W]
        z = jax.nn.gelu(y_sc[r0:r0 + rb, :] + d_ref[...] * su)
        y_sc[r0:r0 + rb, :] = z
        zb_sc[r0:r0 + rb, :] = z.astype(BF16)
    _row_blocks(rows, rb, s5_gelu, unroll=True)
    gl_sc[...] = jnp.dot(zb_sc[...], wglu_ref[...], preferred_element_type=F32) + bglu_ref[...]

    def s5_out(r0):
        oc = y_sc[r0:r0 + rb, :] * _sigmoid(gl_sc[r0:r0 + rb, :])
        mix_sc[r0:r0 + rb, D_RNN + CF_W:D_MODEL] = _rms(oc, gns5_ref[...]).astype(BF16)
    _row_blocks(rows, rb, s5_out, unroll=True)

    x1_ref[...] = x_src[...] + jnp.dot(mix_sc[...], wout_ref[...], preferred_element_type=F32)

    rbn_ref[...] = xpad[rows:rows + rg_hist, :]
    fbn_ref[...] = upad[rows:rows + cf_hist, :]
    if n_chunks > 1:
        xpad[0:rg_hist, :] = xpad[rows:rows + rg_hist, :]
        upad[0:cf_hist, :] = upad[rows:rows + cf_hist, :]


def _const_spec(shape):
    nd = len(shape)
    return pl.BlockSpec(shape, lambda *_: (0,) * nd)


def _layer_spec(a, l):
    nd = a.ndim - 1
    return pl.BlockSpec((None,) + a.shape[1:], lambda *_: (l,) + (0,) * nd)


def _mixer(x, bt, tc, state, w, l):
    x_bm = x.ndim == 3
    n_rows = x.shape[0] * x.shape[1] if x_bm else x.shape[0]
    rows = tc * bt
    n_chunks = n_rows // rows
    assert n_chunks * rows == n_rows and rows % ROW_BLOCK == 0 and bt % 8 == 0
    assert n_chunks == 1 or tc >= CF_CONV - 1
    h0, rbuf, fbuf, s_re, s_im = state
    weights = [w["norm1"], w["w_in"], w["rg_conv_w"], w["rg_conv_b"], w["rg_wg"], w["rg_bg"],
               w["rg_lambda"], w["cf_conv_w"], w["cf_conv_b"], w["cf_ln_g"], w["cf_ln_b"],
               w["s5_ab_re"], w["s5_ab_im"], w["s5_bb_re"], w["s5_bb_im"], w["s5_c_re"], w["s5_c_im"],
               w["s5_d"], w["s5_w_glu"], w["s5_b_glu"], w["gn_rg"], w["gn_cf"], w["gn_s5"], w["w_out"]]
    states = [h0, rbuf, fbuf, s_re, s_im]
    row_spec = pl.BlockSpec((rows, D_MODEL), lambda c: (c, 0))
    x_spec = pl.BlockSpec((bt, tc, D_MODEL), lambda c: (0, c, 0)) if x_bm else row_spec
    in_specs = ([x_spec] + [_const_spec(a.shape) for a in states]
                + [_layer_spec(a, l) for a in weights])
    out_shape = [jax.ShapeDtypeStruct((n_rows, D_MODEL), F32)] + [
        jax.ShapeDtypeStruct(a.shape, F32) for a in states]
    out_specs = [row_spec] + [_const_spec(a.shape) for a in states]
    scratch = [
        pltpu.VMEM(((tc + RG_CONV - 1) * bt, D_RNN), F32),
        pltpu.VMEM(((tc + CF_CONV - 1) * bt, CF_W), F32),
        pltpu.VMEM((bt, D_RNN), F32),
        pltpu.VMEM((bt, S5_N), F32),
        pltpu.VMEM((bt, S5_N), F32),
        pltpu.VMEM((rows, D_MODEL), BF16),
        pltpu.VMEM((rows, P_IN), F32),
        pltpu.VMEM((rows, D_RNN), F32),
        pltpu.VMEM((rows, 2 * D_RNN), F32),
        pltpu.VMEM((rows, D_RNN), F32),
        pltpu.VMEM((rows, D_RNN), F32),
        pltpu.VMEM((rows, CF_W), F32),
        pltpu.VMEM((rows, S5_W), F32),
        pltpu.VMEM((rows, S5_W), BF16),
        pltpu.VMEM((rows, S5_W), F32),
        pltpu.VMEM((rows, S5_N), F32),
        pltpu.VMEM((rows, S5_N), F32),
        pltpu.VMEM((rows, D_MODEL), BF16),
    ]
    if x_bm:
        scratch.append(pltpu.VMEM((rows, D_MODEL), F32))
    return pl.pallas_call(
        functools.partial(_mixer_kernel, bt, tc, n_chunks, x_bm),
        grid=(n_chunks,),
        in_specs=in_specs,
        out_specs=out_specs,
        out_shape=out_shape,
        scratch_shapes=scratch,
        compiler_params=pltpu.CompilerParams(dimension_semantics=("arbitrary",),
                                             vmem_limit_bytes=VMEM_LIMIT_BYTES),
        name="mixer",
    )(x, *states, *weights)


def _route(logits):
    shape = logits.shape
    lane = lax.broadcasted_iota(jnp.int32, shape, 1)
    neg = jnp.float32(-jnp.inf)
    is_grp = lane < N_GROUPS
    gl = jnp.where(is_grp, logits, neg)
    gmax = jnp.max(gl, axis=-1, keepdims=True)
    gidx = jnp.min(jnp.where(gl == gmax, lane, ROUTER_LANES), axis=-1, keepdims=True)
    p_sel = 1.0 / jnp.sum(jnp.where(is_grp, jnp.exp(logits - gmax), 0.0), axis=-1, keepdims=True)
    lo = N_GROUPS + EXP_PER_GROUP * gidx
    in_grp = jnp.logical_and(lane >= lo, lane < lo + EXP_PER_GROUP)
    e1 = jnp.where(in_grp, logits, neg)
    v1 = jnp.max(e1, axis=-1, keepdims=True)
    i1 = jnp.min(jnp.where(e1 == v1, lane, ROUTER_LANES), axis=-1, keepdims=True)
    e2 = jnp.where(lane == i1, neg, e1)
    v2 = jnp.max(e2, axis=-1, keepdims=True)
    i2 = jnp.min(jnp.where(e2 == v2, lane, ROUTER_LANES), axis=-1, keepdims=True)
    ex = jnp.exp(v2 - v1)
    w1 = 1.0 / (1.0 + ex)
    w2 = ex * w1
    gate = jnp.where(lane == i1, p_sel * w1, jnp.where(lane == i2, p_sel * w2, 0.0))
    return gate, gidx


def _moe_route_kernel(x_ref, n2_ref, wr_ref, br_ref, ts_ref, slot_ref, cnt_ref):
    rows = x_ref.shape[0]
    x = x_ref[...]
    t = _rms(x, n2_ref[...])
    t_hi = t.astype(BF16)
    t_lo = (t - t_hi.astype(F32)).astype(BF16)
    wr = wr_ref[...]
    wr_hi = wr.astype(BF16)
    wr_lo = (wr - wr_hi.astype(F32)).astype(BF16)
    logits = (jnp.dot(t_hi, wr_hi, preferred_element_type=F32)
              + jnp.dot(t_lo, wr_hi, preferred_element_type=F32)
              + jnp.dot(t_hi, wr_lo, preferred_element_type=F32)) + br_ref[...]
    gate, gidx = _route(logits)

    lane = lax.broadcasted_iota(jnp.int32, (rows, ROUTER_LANES), 1)
    onehot = jnp.where(lane == gidx, 1.0, 0.0)
    r_i = lax.broadcasted_iota(jnp.int32, (rows, rows), 0)
    c_i = lax.broadcasted_iota(jnp.int32, (rows, rows), 1)
    tri = jnp.where(c_i < r_i, 1.0, 0.0).astype(BF16)
    before = jnp.dot(tri, onehot.astype(BF16), preferred_element_type=F32)
    rank = jnp.sum(before * onehot, axis=-1, keepdims=True)
    counts = jnp.sum(onehot, axis=0, keepdims=True)

    units = jnp.ceil(counts * (1.0 / MOE_UNIT))
    first = MOE_UNIT * (pltpu.roll(units, 1, 1) + pltpu.roll(units, 2, 1) + pltpu.roll(units, 3, 1))
    slot = jnp.sum(onehot * first, axis=-1, keepdims=True) + rank
    slot_b = jnp.broadcast_to(slot, (rows, ROUTER_LANES))
    slot_t = jnp.transpose(slot_b)[0:1, :]
    s_sub = lax.broadcasted_iota(jnp.int32, (MOE_TILE_SLOTS, rows), 0).astype(F32)
    perm = jnp.where(s_sub == slot_t, 1.0, 0.0).astype(BF16)

    ts_ref[:, 0:D_MODEL] = jnp.dot(perm, t_hi, preferred_element_type=F32).astype(BF16)
    gate4 = jnp.zeros_like(gate)
    for gi in range(N_GROUPS):
        shift = ROUTER_LANES - (N_GROUPS + EXP_PER_GROUP * gi)
        gate4 = gate4 + jnp.where(gidx == gi, pltpu.roll(gate, shift, 1), 0.0)
    g_hi = gate4.astype(BF16).astype(F32)
    packed = (g_hi + pltpu.roll(gate4 - g_hi, ROUTER_LANES // 2, 1)).astype(BF16)
    ts_ref[:, D_MODEL:MOE_ROW_W] = jnp.dot(perm, packed, preferred_element_type=F32).astype(BF16)
    slot_ref[...] = slot_b
    cnt_ref[...] = jnp.broadcast_to(counts, cnt_ref.shape)


def _moe_expert_kernel(src_ref, grp_ref, nt_ref, *refs):
    units = refs[0:MOE_MTILE_UNITS]
    wg_ref, wu_ref, wd_ref, out_ref, wg_sc, wu_sc, wd_sc = refs[MOE_MTILE_UNITS:]
    del src_ref
    j = pl.program_id(0)
    active = j < nt_ref[0]

    @pl.when(jnp.logical_and(active, jnp.logical_or(j == 0, grp_ref[j] != grp_ref[jnp.maximum(j - 1, 0)])))
    def _cast_weights():
        for e in range(EXP_PER_GROUP):
            wg_sc[e] = wg_ref[e].astype(BF16)
            wu_sc[e] = wu_ref[e].astype(BF16)
            wd_sc[e] = wd_ref[e].astype(BF16)

    @pl.when(active)
    def _compute():
        tsg = jnp.concatenate([u[...] for u in units], axis=0)
        ts = tsg[:, 0:D_MODEL]
        gp = tsg[:, D_MODEL:MOE_ROW_W].astype(F32)
        gs = gp + pltpu.roll(gp, ROUTER_LANES // 2, 1)
        acc = None
        for e in range(EXP_PER_GROUP):
            hg = jnp.dot(ts, wg_sc[e], preferred_element_type=F32)
            hu = jnp.dot(ts, wu_sc[e], preferred_element_type=F32)
            act = (hg * _sigmoid(hg) * hu * gs[:, e:e + 1]).astype(BF16)
            o = jnp.dot(act, wd_sc[e], preferred_element_type=F32)
            acc = o if acc is None else acc + o
        out_ref[...] = acc.astype(BF16)

    @pl.when(jnp.logical_not(active))
    def _idle():
        out_ref[...] = jnp.zeros_like(out_ref)


def _moe_combine_kernel(final, bt_out, pos_ref, x_ref, slot_ref, *refs):
    del pos_ref
    units = refs[0:MOE_TILE_UNITS]
    fn_ref, o_ref = refs[MOE_TILE_UNITS:]
    rows = x_ref.shape[0]
    os_local = jnp.concatenate([u[...] for u in units], axis=0)
    s_lane = lax.broadcasted_iota(jnp.int32, (rows, MOE_TILE_SLOTS), 1).astype(F32)
    perm_t = jnp.where(s_lane == slot_ref[:, 0:1], 1.0, 0.0).astype(BF16)
    y = x_ref[...] + jnp.dot(perm_t, os_local, preferred_element_type=F32)
    if final:
        y = _rms(y, fn_ref[...])
    if bt_out:
        tc = rows // bt_out
        for t in range(tc):
            o_ref[:, t, :] = y[t * bt_out:(t + 1) * bt_out, :]
    else:
        o_ref[...] = y


def _moe_plan(counts, n_tiles, n_mtiles):
    i32 = jnp.int32
    units = (counts + (MOE_UNIT - 1)) // MOE_UNIT
    loc_off = jnp.cumsum(units, axis=1) - units
    grp_tiles = (jnp.sum(units, axis=0) + (MOE_MTILE_UNITS - 1)) // MOE_MTILE_UNITS
    grp_end = jnp.cumsum(grp_tiles)
    tile_start = grp_end - grp_tiles
    before_c = jnp.cumsum(units, axis=0) - units
    u = jnp.arange(MOE_TILE_UNITS, dtype=i32)
    g_of = jnp.sum((u[None, :, None] >= (loc_off + units)[:, None, :]).astype(i32), axis=-1)
    valid = g_of < N_GROUPS
    g_c = jnp.minimum(g_of, N_GROUPS - 1)
    k = u[None, :] - jnp.take_along_axis(loc_off, g_c, axis=1)
    pos = MOE_MTILE_UNITS * tile_start[g_c] + jnp.take_along_axis(before_c, g_c, axis=1) + k
    pos = jnp.where(valid, pos, 0).astype(i32)
    local_id = (jnp.arange(n_tiles, dtype=i32)[:, None] * MOE_TILE_UNITS + u[None, :])
    n_pos = n_mtiles * MOE_MTILE_UNITS
    src = jnp.zeros((n_pos,), i32).at[jnp.where(valid, pos, n_pos).reshape(-1)].set(
        local_id.reshape(-1), mode="drop")
    j = jnp.arange(n_mtiles, dtype=i32)
    grp = jnp.minimum(jnp.sum((j[:, None] >= grp_end[None, :]).astype(i32), axis=-1), N_GROUPS - 1)
    return src, grp.astype(i32), grp_end[-1:].astype(i32), pos.reshape(-1)


def _moe(x1, w, l, final, bt_out=0):
    n_rows = x1.shape[0]
    rows = CHUNK_ROWS
    assert n_rows % rows == 0
    n_tiles = n_rows // rows
    n_mtiles = -(-(n_tiles * MOE_TILE_UNITS) // MOE_MTILE_UNITS) + N_GROUPS
    mrows = MOE_MTILE_UNITS * MOE_UNIT
    cparams = pltpu.CompilerParams(dimension_semantics=("arbitrary",), vmem_limit_bytes=VMEM_LIMIT_BYTES)

    row_spec = pl.BlockSpec((rows, D_MODEL), lambda i: (i, 0))
    ts, slot, cnt = pl.pallas_call(
        _moe_route_kernel,
        grid=(n_tiles,),
        in_specs=[row_spec, _layer_spec(w["norm2"], l), _layer_spec(w["moe_w_r"], l),
                  _layer_spec(w["moe_b_r"], l)],
        out_specs=[pl.BlockSpec((MOE_TILE_SLOTS, MOE_ROW_W), lambda i: (i, 0)),
                   pl.BlockSpec((rows, ROUTER_LANES), lambda i: (i, 0)),
                   pl.BlockSpec((8, ROUTER_LANES), lambda i: (i, 0))],
        out_shape=[jax.ShapeDtypeStruct((n_tiles * MOE_TILE_SLOTS, MOE_ROW_W), BF16),
                   jax.ShapeDtypeStruct((n_rows, ROUTER_LANES), F32),
                   jax.ShapeDtypeStruct((n_tiles * 8, ROUTER_LANES), F32)],
        compiler_params=cparams,
        name="moe_route",
    )(x1, w["norm2"], w["moe_w_r"], w["moe_b_r"])

    counts = cnt.reshape(n_tiles, 8, ROUTER_LANES)[:, 0, :N_GROUPS].astype(jnp.int32)
    src, grp, n_used, pos = _moe_plan(counts, n_tiles, n_mtiles)

    def unit_in(k):
        return pl.BlockSpec((MOE_UNIT, MOE_ROW_W), lambda j, src, grp, nt: (src[j * MOE_MTILE_UNITS + k], 0))

    def grp_w(shape):
        return pl.BlockSpec((None, EXP_PER_GROUP) + shape, lambda j, src, grp, nt: (l, grp[j], 0, 0))
    os_sorted = pl.pallas_call(
        _moe_expert_kernel,
        grid_spec=pltpu.PrefetchScalarGridSpec(
            num_scalar_prefetch=3,
            grid=(n_mtiles,),
            in_specs=[unit_in(k) for k in range(MOE_MTILE_UNITS)]
            + [grp_w((D_MODEL, EXPERT_FF)), grp_w((D_MODEL, EXPERT_FF)), grp_w((EXPERT_FF, D_MODEL))],
            out_specs=pl.BlockSpec((mrows, D_MODEL), lambda j, src, grp, nt: (j, 0)),
            scratch_shapes=[pltpu.VMEM((EXP_PER_GROUP, D_MODEL, EXPERT_FF), BF16),
                            pltpu.VMEM((EXP_PER_GROUP, D_MODEL, EXPERT_FF), BF16),
                            pltpu.VMEM((EXP_PER_GROUP, EXPERT_FF, D_MODEL), BF16)],
        ),
        out_shape=jax.ShapeDtypeStruct((n_mtiles * mrows, D_MODEL), BF16),
        compiler_params=cparams,
        name="moe_experts",
    )(src, grp, n_used, *([ts] * MOE_MTILE_UNITS), w["moe_w_gate"], w["moe_w_up"], w["moe_w_down"])

    def unit_out(u):
        return pl.BlockSpec((MOE_UNIT, D_MODEL), lambda i, pos: (pos[i * MOE_TILE_UNITS + u], 0))
    if bt_out:
        tc = rows // bt_out
        out_spec = pl.BlockSpec((bt_out, tc, D_MODEL), lambda i, pos: (0, i, 0))
        out_shape = jax.ShapeDtypeStruct((bt_out, n_tiles * tc, D_MODEL), F32)
    else:
        out_spec = pl.BlockSpec((rows, D_MODEL), lambda i, pos: (i, 0))
        out_shape = jax.ShapeDtypeStruct((n_rows, D_MODEL), F32)
    return pl.pallas_call(
        functools.partial(_moe_combine_kernel, final, bt_out),
        grid_spec=pltpu.PrefetchScalarGridSpec(
            num_scalar_prefetch=1,
            grid=(n_tiles,),
            in_specs=[pl.BlockSpec((rows, D_MODEL), lambda i, pos: (i, 0)),
                      pl.BlockSpec((rows, ROUTER_LANES), lambda i, pos: (i, 0))]
            + [unit_out(u) for u in range(MOE_TILE_UNITS)]
            + [pl.BlockSpec(w["final_norm"].shape, lambda i, pos: (0, 0))],
            out_specs=out_spec,
        ),
        out_shape=out_shape,
        compiler_params=cparams,
        name="moe_combine",
    )(pos, x1, slot, *([os_sorted] * MOE_TILE_UNITS), w["final_norm"])


def _block_diag(blocks):
    L, n, r, c = blocks.shape
    eye = jnp.eye(n, dtype=blocks.dtype)
    return (blocks[:, :, :, None, :] * eye[None, :, None, :, None]).reshape(L, n * r, n * c)


def _to_tm(a):
    a = jnp.swapaxes(a, 0, 1)
    return a.reshape((a.shape[0] * a.shape[1],) + a.shape[2:])


def _from_tm(a, b):
    return jnp.swapaxes(a.reshape((a.shape[0] // b, b) + a.shape[1:]), 0, 1)


def kernel(x_prompt, x_sample, state_rglru_h, state_rglru_conv, state_conf_conv, state_s5_re, state_s5_im, norm1, w_in, rg_conv_w, rg_conv_b, rg_wa, rg_ba, rg_wx, rg_bx, rg_lambda, cf_conv_w, cf_conv_b, cf_ln_g, cf_ln_b, s5_a_re, s5_a_im, s5_log_dt, s5_b_re, s5_b_im, s5_c_re, s5_c_im, s5_d, s5_w_glu, s5_b_glu, gn_rg, gn_cf, gn_s5, w_out, norm2, moe_w_grp, moe_b_grp, moe_w_exp, moe_b_exp, moe_w_gate, moe_w_up, moe_w_down, final_norm):
    L = w_in.shape[0]
    B, T, _ = x_prompt.shape
    DB, DT, _ = x_sample.shape

    ab_re, ab_im, bb_re, bb_im = _s5_prep(s5_a_re, s5_a_im, s5_log_dt,
                                          jnp.swapaxes(s5_b_re, 2, 3), jnp.swapaxes(s5_b_im, 2, 3))

    def row(a):
        return a.reshape(L, 1, -1)

    W = dict(
        norm1=row(norm1), w_in=w_in.astype(BF16),
        rg_conv_w=rg_conv_w, rg_conv_b=row(rg_conv_b),
        rg_wg=jnp.concatenate([_block_diag(rg_wa), _block_diag(rg_wx)], axis=-1).astype(BF16),
        rg_bg=jnp.concatenate([row(rg_ba), row(rg_bx)], axis=-1),
        rg_lambda=row(rg_lambda),
        cf_conv_w=cf_conv_w, cf_conv_b=row(cf_conv_b), cf_ln_g=row(cf_ln_g), cf_ln_b=row(cf_ln_b),
        s5_ab_re=row(ab_re), s5_ab_im=row(ab_im),
        s5_bb_re=_block_diag(bb_re).astype(BF16), s5_bb_im=_block_diag(bb_im).astype(BF16),
        s5_c_re=_block_diag(jnp.swapaxes(s5_c_re, 2, 3)).astype(BF16),
        s5_c_im=_block_diag(jnp.swapaxes(s5_c_im, 2, 3)).astype(BF16),
        s5_d=row(s5_d), s5_w_glu=s5_w_glu.astype(BF16), s5_b_glu=row(s5_b_glu),
        gn_rg=row(gn_rg), gn_cf=row(gn_cf), gn_s5=row(gn_s5), w_out=w_out.astype(BF16),
        norm2=row(norm2),
        moe_w_r=jnp.concatenate(
            [moe_w_grp, moe_w_exp.reshape(L, D_MODEL, N_EXPERTS),
             jnp.zeros((L, D_MODEL, ROUTER_LANES - N_GROUPS - N_EXPERTS), F32)], axis=-1),
        moe_b_r=jnp.concatenate(
            [moe_b_grp, moe_b_exp.reshape(L, N_EXPERTS),
             jnp.zeros((L, ROUTER_LANES - N_GROUPS - N_EXPERTS), F32)], axis=-1).reshape(L, 1, ROUTER_LANES),
        moe_w_gate=moe_w_gate, moe_w_up=moe_w_up, moe_w_down=moe_w_down,
    )

    xp = x_prompt
    xs = _to_tm(x_sample)
    tc_p = CHUNK_ROWS // B
    p_states, s_states = [], []
    W["final_norm"] = final_norm.reshape(1, D_MODEL)
    for l in range(L):
        st_p = (jnp.zeros((B, D_RNN), F32), jnp.zeros(((RG_CONV - 1) * B, D_RNN), F32),
                jnp.zeros(((CF_CONV - 1) * B, CF_W), F32), jnp.zeros((B, S5_N), F32),
                jnp.zeros((B, S5_N), F32))
        st_s = (state_rglru_h[l], _to_tm(state_rglru_conv[l]), _to_tm(state_conf_conv[l]),
                state_s5_re[l].reshape(DB, S5_N), state_s5_im[l].reshape(DB, S5_N))
        xp, *np_l = _mixer(xp, B, tc_p, st_p, W, l)
        xs, *ns_l = _mixer(xs, DB, DT, st_s, W, l)
        final = l == L - 1
        xp = _moe(xp, W, l, final, bt_out=B if final else 0)
        xs = _moe(xs, W, l, final)
        p_states.append(np_l)
        s_states.append(ns_l)

    def assemble(states, b):
        h = jnp.stack([s[0] for s in states])
        rc = jnp.stack([_from_tm(s[1], b) for s in states])
        cf = jnp.stack([_from_tm(s[2], b) for s in states])
        sre = jnp.stack([s[3].reshape(b, S5_GROUPS, S5_STATE) for s in states])
        sim = jnp.stack([s[4].reshape(b, S5_GROUPS, S5_STATE) for s in states])
        return h, rc, cf, sre, sim

    y_sample = _from_tm(xs, DB)
    return (xp, y_sample) + assemble(p_states, B) + assemble(s_states, DB)
```

```python
import functools

import jax
import jax.numpy as jnp
from jax import lax
from jax.experimental import pallas as pl
from jax.experimental.pallas import tpu as pltpu

F32 = jnp.float32
BF16 = jnp.bfloat16

D_MODEL = 1024
D_RNN = 512
RG_HEADS = 8
RG_CONV = 4
RG_C = 8.0
CF_W = 256
CF_CONV = 31
S5_W = 256
S5_GROUPS = 16
S5_GROUP_CH = 16
S5_STATE = 64
S5_N = S5_GROUPS * S5_STATE
P_IN = 2 * D_RNN + 2 * CF_W + S5_W
N_GROUPS = 4
EXP_PER_GROUP = 4
N_EXPERTS = 16
EXPERT_FF = 256
EPS = 1e-6

ROUTER_LANES = 128
VMEM_LIMIT_BYTES = 60000 * 1024
ROW_BLOCK = 64
CHUNK_ROWS = 512
MOE_UNIT = 32
MOE_TILE_UNITS = CHUNK_ROWS // MOE_UNIT + N_GROUPS
MOE_TILE_SLOTS = MOE_TILE_UNITS * MOE_UNIT
MOE_MTILE_UNITS = 16
MOE_ROW_W = D_MODEL + ROUTER_LANES


def _rms(x, g):
    return x * lax.rsqrt(jnp.mean(x * x, axis=-1, keepdims=True) + EPS) * g


def _softplus(z):
    return jnp.maximum(z, 0.0) + jnp.log1p(jnp.exp(-jnp.abs(z)))


def _sigmoid(z):
    return 0.5 * jnp.tanh(0.5 * z) + 0.5


def _row_blocks(rows, rb, fn, unroll=False):
    if unroll:
        for i in range(rows // rb):
            fn(i * rb)
        return

    def body(i, carry):
        fn(pl.multiple_of(i * rb, rb))
        return carry
    lax.fori_loop(0, rows // rb, body, 0)


def _s5_prep_kernel(are_ref, aim_ref, ldt_ref, bre_ref, bim_ref,
                    abre_ref, abim_ref, bbre_ref, bbim_ref):
    a_re = are_ref[...]
    a_im = aim_ref[...]
    dt = jnp.exp(ldt_ref[...])
    mag = jnp.exp(a_re * dt)
    ab_re = mag * jnp.cos(a_im * dt)
    ab_im = mag * jnp.sin(a_im * dt)
    den = a_re * a_re + a_im * a_im
    nr = ab_re - 1.0
    ni = ab_im
    k_re = (nr * a_re + ni * a_im) / den
    k_im = (ni * a_re - nr * a_im) / den
    abre_ref[...] = ab_re
    abim_ref[...] = ab_im
    b_re = bre_ref[...]
    b_im = bim_ref[...]
    bbre_ref[...] = k_re[:, None, :] * b_re - k_im[:, None, :] * b_im
    bbim_ref[...] = k_re[:, None, :] * b_im + k_im[:, None, :] * b_re


def _s5_prep(a_re, a_im, log_dt, b_re_ghp, b_im_ghp):
    L = a_re.shape[0]
    gp = pl.BlockSpec((None, S5_GROUPS, S5_STATE), lambda l: (l, 0, 0))
    ghp = pl.BlockSpec((None, S5_GROUPS, S5_GROUP_CH, S5_STATE), lambda l: (l, 0, 0, 0))
    return pl.pallas_call(
        _s5_prep_kernel,
        grid=(L,),
        in_specs=[gp, gp, pl.BlockSpec((None, S5_GROUPS, 1), lambda l: (l, 0, 0)), ghp, ghp],
        out_specs=[gp, gp, ghp, ghp],
        out_shape=[jax.ShapeDtypeStruct((L, S5_GROUPS, S5_STATE), F32)] * 2
        + [jax.ShapeDtypeStruct((L, S5_GROUPS, S5_GROUP_CH, S5_STATE), F32)] * 2,
        name="s5_prep",
    )(a_re, a_im, log_dt[..., None], b_re_ghp, b_im_ghp)


def _mixer_kernel(bt, tc, n_chunks, x_bm,
                  x_ref, h0_ref, rb_ref, fb_ref, sre_ref, sim_ref,
                  n1_ref, win_ref, rgw_ref, rgb_ref, wg_ref, bg_ref, lam_ref,
                  cfw_ref, cfb_ref, lng_ref, lnb_ref,
                  abre_ref, abim_ref, bbre_ref, bbim_ref, cre_ref, cim_ref, d_ref,
                  wglu_ref, bglu_ref, gnrg_ref, gncf_ref, gns5_ref, wout_ref,
                  x1_ref, hl_ref, rbn_ref, fbn_ref, sren_ref, simn_ref,
                  xpad, upad, h_sc, sre_sc, sim_sc, hn_sc, proj_sc, xc_sc, gate_sc,
                  a_sc, b_sc, v_sc, y_sc, zb_sc, gl_sc, bre_sc, bim_sc, mix_sc, *maybe_xtm):
    rows = tc * bt
    rb = ROW_BLOCK
    rg_hist = (RG_CONV - 1) * bt
    cf_hist = (CF_CONV - 1) * bt
    c = pl.program_id(0)

    @pl.when(c == 0)
    def _init():
        xpad[0:rg_hist, :] = rb_ref[...]
        upad[0:cf_hist, :] = fb_ref[...]
        h_sc[...] = h0_ref[...]
        sre_sc[...] = sre_ref[...]
        sim_sc[...] = sim_ref[...]

    o_cf = 2 * D_RNN
    o_s5 = o_cf + 2 * CF_W

    if x_bm:
        (x_src,) = maybe_xtm
        for t in range(tc):
            x_src[t * bt:(t + 1) * bt, :] = x_ref[:, t, :]
    else:
        x_src = x_ref

    def norm_in(r0):
        x = x_src[r0:r0 + rb, :]
        hn_sc[r0:r0 + rb, :] = _rms(x, n1_ref[...]).astype(BF16)
    _row_blocks(rows, rb, norm_in, unroll=True)
    proj_sc[...] = jnp.dot(hn_sc[...], win_ref[...], preferred_element_type=F32)

    xpad[rg_hist:rg_hist + rows, :] = proj_sc[:, 0:D_RNN]

    def rg_conv(r0):
        acc = rgb_ref[...] + rgw_ref[0:1, :] * xpad[r0:r0 + rb, :]
        for k in range(1, RG_CONV):
            acc = acc + rgw_ref[k:k + 1, :] * xpad[r0 + k * bt:r0 + k * bt + rb, :]
        xc_sc[r0:r0 + rb, :] = acc
    _row_blocks(rows, rb, rg_conv, unroll=True)

    gate_sc[...] = jnp.dot(xc_sc[...].astype(BF16), wg_ref[...],
                           preferred_element_type=F32) + bg_ref[...]

    su_bf = proj_sc[:, o_s5:o_s5 + S5_W].astype(BF16)
    bre_sc[...] = jnp.dot(su_bf, bbre_ref[...], preferred_element_type=F32)
    bim_sc[...] = jnp.dot(su_bf, bbim_ref[...], preferred_element_type=F32)

    neg_c_sp = -RG_C * _softplus(-lam_ref[...])

    def rg_coeffs(r0):
        g = gate_sc[r0:r0 + rb, :]
        r = _sigmoid(g[:, 0:D_RNN])
        i = _sigmoid(g[:, D_RNN:2 * D_RNN])
        a = jnp.exp(neg_c_sp * r)
        a_sc[r0:r0 + rb, :] = a
        b_sc[r0:r0 + rb, :] = jnp.sqrt(1.0 - a * a) * (i * xc_sc[r0:r0 + rb, :])
    _row_blocks(rows, rb, rg_coeffs, unroll=True)

    def cf_glu(r0):
        ca = proj_sc[r0:r0 + rb, o_cf:o_cf + CF_W]
        cg = proj_sc[r0:r0 + rb, o_cf + CF_W:o_cf + 2 * CF_W]
        upad[cf_hist + r0:cf_hist + r0 + rb, :] = ca * _sigmoid(cg)
    _row_blocks(rows, rb, cf_glu, unroll=True)

    h = h_sc[...]
    for t in range(tc):
        r0 = t * bt
        h = a_sc[r0:r0 + bt, :] * h + b_sc[r0:r0 + bt, :]
        b_sc[r0:r0 + bt, :] = h
    h_sc[...] = h
    hl_ref[...] = h

    a_r = jnp.broadcast_to(abre_ref[...], (bt, S5_N))
    a_i = jnp.broadcast_to(abim_ref[...], (bt, S5_N))
    s_r = sre_sc[...]
    s_i = sim_sc[...]
    for t in range(tc):
        r0 = t * bt
        n_r = a_r * s_r - a_i * s_i + bre_sc[r0:r0 + bt, :]
        n_i = a_r * s_i + a_i * s_r + bim_sc[r0:r0 + bt, :]
        bre_sc[r0:r0 + bt, :] = n_r
        bim_sc[r0:r0 + bt, :] = n_i
        s_r, s_i = n_r, n_i
    sre_sc[...] = s_r
    sim_sc[...] = s_i
    sren_ref[...] = s_r
    simn_ref[...] = s_i

    def cf_conv(r0):
        acc = cfb_ref[...] + cfw_ref[0:1, :] * upad[r0:r0 + rb, :]
        for k in range(1, CF_CONV):
            acc = acc + cfw_ref[k:k + 1, :] * upad[r0 + k * bt:r0 + k * bt + rb, :]
        v_sc[r0:r0 + rb, :] = acc
    _row_blocks(rows, rb, cf_conv, unroll=True)

    def rg_out(r0):
        h = b_sc[r0:r0 + rb, :]
        yb = proj_sc[r0:r0 + rb, D_RNN:2 * D_RNN]
        mix_sc[r0:r0 + rb, 0:D_RNN] = _rms(h * jax.nn.gelu(yb), gnrg_ref[...]).astype(BF16)
    _row_blocks(rows, rb, rg_out, unroll=True)

    def cf_out(r0):
        v = v_sc[r0:r0 + rb, :]
        mu = jnp.mean(v, axis=-1, keepdims=True)
        cen = v - mu
        var = jnp.mean(cen * cen, axis=-1, keepdims=True)
        vn = cen * lax.rsqrt(var + EPS) * lng_ref[...] + lnb_ref[...]
        mix_sc[r0:r0 + rb, D_RNN:D_RNN + CF_W] = _rms(vn * _sigmoid(vn), gncf_ref[...]).astype(BF16)
    _row_blocks(rows, rb, cf_out, unroll=True)

    y_sc[...] = (jnp.dot(bre_sc[...].astype(BF16), cre_ref[...], preferred_element_type=F32)
                 - jnp.dot(bim_sc[...].astype(BF16), cim_ref[...], preferred_element_type=F32))

    def s5_gelu(r0):
        su = proj_sc[r0:r0 + rb, o_s5:o_s5 + S5_W]
        z = jax.nn.gelu(y_sc[r0:r0 + rb, :] + d_ref[...] * su)
        y_sc[r0:r0 + rb, :] = z
        zb_sc[r0:r0 + rb, :] = z.astype(BF16)
    _row_blocks(rows, rb, s5_gelu, unroll=True)
    gl_sc[...] = jnp.dot(zb_sc[...], wglu_ref[...], preferred_element_type=F32) + bglu_ref[...]

    def s5_out(r0):
        oc = y_sc[r0:r0 + rb, :] * _sigmoid(gl_sc[r0:r0 + rb, :])
        mix_sc[r0:r0 + rb, D_RNN + CF_W:D_MODEL] = _rms(oc, gns5_ref[...]).astype(BF16)
    _row_blocks(rows, rb, s5_out, unroll=True)

    x1_ref[...] = x_src[...] + jnp.dot(mix_sc[...], wout_ref[...], preferred_element_type=F32)

    rbn_ref[...] = xpad[rows:rows + rg_hist, :]
    fbn_ref[...] = upad[rows:rows + cf_hist, :]
    if n_chunks > 1:
        xpad[0:rg_hist, :] = xpad[rows:rows + rg_hist, :]
        upad[0:cf_hist, :] = upad[rows:rows + cf_hist, :]


def _const_spec(shape):
    nd = len(shape)
    return pl.BlockSpec(shape, lambda *_: (0,) * nd)


def _layer_spec(a, l):
    nd = a.ndim - 1
    return pl.BlockSpec((None,) + a.shape[1:], lambda *_: (l,) + (0,) * nd)


def _mixer(x, bt, tc, state, w, l):
    x_bm = x.ndim == 3
    n_rows = x.shape[0] * x.shape[1] if x_bm else x.shape[0]
    rows = tc * bt
    n_chunks = n_rows // rows
    assert n_chunks * rows == n_rows and rows % ROW_BLOCK == 0 and bt % 8 == 0
    assert n_chunks == 1 or tc >= CF_CONV - 1
    h0, rbuf, fbuf, s_re, s_im = state
    weights = [w["norm1"], w["w_in"], w["rg_conv_w"], w["rg_conv_b"], w["rg_wg"], w["rg_bg"],
               w["rg_lambda"], w["cf_conv_w"], w["cf_conv_b"], w["cf_ln_g"], w["cf_ln_b"],
               w["s5_ab_re"], w["s5_ab_im"], w["s5_bb_re"], w["s5_bb_im"], w["s5_c_re"], w["s5_c_im"],
               w["s5_d"], w["s5_w_glu"], w["s5_b_glu"], w["gn_rg"], w["gn_cf"], w["gn_s5"], w["w_out"]]
    states = [h0, rbuf, fbuf, s_re, s_im]
    row_spec = pl.BlockSpec((rows, D_MODEL), lambda c: (c, 0))
    x_spec = pl.BlockSpec((bt, tc, D_MODEL), lambda c: (0, c, 0)) if x_bm else row_spec
    in_specs = ([x_spec] + [_const_spec(a.shape) for a in states]
                + [_layer_spec(a, l) for a in weights])
    out_shape = [jax.ShapeDtypeStruct((n_rows, D_MODEL), F32)] + [
        jax.ShapeDtypeStruct(a.shape, F32) for a in states]
    out_specs = [row_spec] + [_const_spec(a.shape) for a in states]
    scratch = [
        pltpu.VMEM(((tc + RG_CONV - 1) * bt, D_RNN), F32),
        pltpu.VMEM(((tc + CF_CONV - 1) * bt, CF_W), F32),
        pltpu.VMEM((bt, D_RNN), F32),
        pltpu.VMEM((bt, S5_N), F32),
        pltpu.VMEM((bt, S5_N), F32),
        pltpu.VMEM((rows, D_MODEL), BF16),
        pltpu.VMEM((rows, P_IN), F32),
        pltpu.VMEM((rows, D_RNN), F32),
        pltpu.VMEM((rows, 2 * D_RNN), F32),
        pltpu.VMEM((rows, D_RNN), F32),
        pltpu.VMEM((rows, D_RNN), F32),
        pltpu.VMEM((rows, CF_W), F32),
        pltpu.VMEM((rows, S5_W), F32),
        pltpu.VMEM((rows, S5_W), BF16),
        pltpu.VMEM((rows, S5_W), F32),
        pltpu.VMEM((rows, S5_N), F32),
        pltpu.VMEM((rows, S5_N), F32),
        pltpu.VMEM((rows, D_MODEL), BF16),
    ]
    if x_bm:
        scratch.append(pltpu.VMEM((rows, D_MODEL), F32))
    return pl.pallas_call(
        functools.partial(_mixer_kernel, bt, tc, n_chunks, x_bm),
        grid=(n_chunks,),
        in_specs=in_specs,
        out_specs=out_specs,
        out_shape=out_shape,
        scratch_shapes=scratch,
        compiler_params=pltpu.CompilerParams(dimension_semantics=("arbitrary",),
                                             vmem_limit_bytes=VMEM_LIMIT_BYTES),
        name="mixer",
    )(x, *states, *weights)


def _route(logits):
    shape = logits.shape
    lane = lax.broadcasted_iota(jnp.int32, shape, 1)
    neg = jnp.float32(-jnp.inf)
    is_grp = lane < N_GROUPS
    gl = jnp.where(is_grp, logits, neg)
    gmax = jnp.max(gl, axis=-1, keepdims=True)
    gidx = jnp.min(jnp.where(gl == gmax, lane, ROUTER_LANES), axis=-1, keepdims=True)
    p_sel = 1.0 / jnp.sum(jnp.where(is_grp, jnp.exp(logits - gmax), 0.0), axis=-1, keepdims=True)
    lo = N_GROUPS + EXP_PER_GROUP * gidx
    in_grp = jnp.logical_and(lane >= lo, lane < lo + EXP_PER_GROUP)
    e1 = jnp.where(in_grp, logits, neg)
    v1 = jnp.max(e1, axis=-1, keepdims=True)
    i1 = jnp.min(jnp.where(e1 == v1, lane, ROUTER_LANES), axis=-1, keepdims=True)
    e2 = jnp.where(lane == i1, neg, e1)
    v2 = jnp.max(e2, axis=-1, keepdims=True)
    i2 = jnp.min(jnp.where(e2 == v2, lane, ROUTER_LANES), axis=-1, keepdims=True)
    ex = jnp.exp(v2 - v1)
    w1 = 1.0 / (1.0 + ex)
    w2 = ex * w1
    gate = jnp.where(lane == i1, p_sel * w1, jnp.where(lane == i2, p_sel * w2, 0.0))
    return gate, gidx


def _moe_route_kernel(x_ref, n2_ref, wr_ref, br_ref, ts_ref, slot_ref, cnt_ref, tri_sc):
    rows = x_ref.shape[0]
    x = x_ref[...]
    t = _rms(x, n2_ref[...])
    t_hi = t.astype(BF16)
    t_lo = (t - t_hi.astype(F32)).astype(BF16)
    wr = wr_ref[...]
    wr_hi = wr.astype(BF16)
    wr_lo = (wr - wr_hi.astype(F32)).astype(BF16)
    logits = (jnp.dot(t_hi, wr_hi, preferred_element_type=F32)
              + jnp.dot(t_lo, wr_hi, preferred_element_type=F32)
              + jnp.dot(t_hi, wr_lo, preferred_element_type=F32)) + br_ref[...]
    gate, gidx = _route(logits)

    lane = lax.broadcasted_iota(jnp.int32, (rows, ROUTER_LANES), 1)
    onehot = jnp.where(lane == gidx, 1.0, 0.0)
    @pl.when(pl.program_id(0) == 0)
    def _build_tri():
        r_i = lax.broadcasted_iota(jnp.int32, (rows, rows), 0)
        c_i = lax.broadcasted_iota(jnp.int32, (rows, rows), 1)
        tri_sc[...] = jnp.where(c_i < r_i, 1.0, 0.0).astype(BF16)
    before = jnp.dot(tri_sc[...], onehot.astype(BF16), preferred_element_type=F32)
    rank = jnp.sum(before * onehot, axis=-1, keepdims=True)
    counts = jnp.sum(onehot, axis=0, keepdims=True)

    units = jnp.ceil(counts * (1.0 / MOE_UNIT))
    first = MOE_UNIT * (pltpu.roll(units, 1, 1) + pltpu.roll(units, 2, 1) + pltpu.roll(units, 3, 1))
    slot = jnp.sum(onehot * first, axis=-1, keepdims=True) + rank
    slot_b = jnp.broadcast_to(slot, (rows, ROUTER_LANES))
    slot_t = jnp.transpose(slot_b)[0:1, :]
    s_sub = lax.broadcasted_iota(jnp.int32, (MOE_TILE_SLOTS, rows), 0).astype(F32)
    perm = jnp.where(s_sub == slot_t, 1.0, 0.0).astype(BF16)

    ts_ref[:, 0:D_MODEL] = jnp.dot(perm, t_hi, preferred_element_type=F32).astype(BF16)
    gate4 = jnp.zeros_like(gate)
    for gi in range(N_GROUPS):
        shift = ROUTER_LANES - (N_GROUPS + EXP_PER_GROUP * gi)
        gate4 = gate4 + jnp.where(gidx == gi, pltpu.roll(gate, shift, 1), 0.0)
    g_hi = gate4.astype(BF16).astype(F32)
    packed = (g_hi + pltpu.roll(gate4 - g_hi, ROUTER_LANES // 2, 1)).astype(BF16)
    ts_ref[:, D_MODEL:MOE_ROW_W] = jnp.dot(perm, packed, preferred_element_type=F32).astype(BF16)
    slot_ref[...] = slot_b
    cnt_ref[...] = jnp.broadcast_to(counts, cnt_ref.shape)


def _moe_expert_kernel(src_ref, grp_ref, nt_ref, *refs):
    units = refs[0:MOE_MTILE_UNITS]
    wg_ref, wu_ref, wd_ref, out_ref, wg_sc, wu_sc, wd_sc = refs[MOE_MTILE_UNITS:]
    del src_ref
    j = pl.program_id(0)
    active = j < nt_ref[0]

    @pl.when(jnp.logical_and(active, jnp.logical_or(j == 0, grp_ref[j] != grp_ref[jnp.maximum(j - 1, 0)])))
    def _cast_weights():
        for e in range(EXP_PER_GROUP):
            wg_sc[e] = wg_ref[e].astype(BF16)
            wu_sc[e] = wu_ref[e].astype(BF16)
            wd_sc[e] = wd_ref[e].astype(BF16)

    @pl.when(active)
    def _compute():
        tsg = jnp.concatenate([u[...] for u in units], axis=0)
        ts = tsg[:, 0:D_MODEL]
        gp = tsg[:, D_MODEL:MOE_ROW_W].astype(F32)
        gs = gp + pltpu.roll(gp, ROUTER_LANES // 2, 1)
        acc = None
        for e in range(EXP_PER_GROUP):
            hg = jnp.dot(ts, wg_sc[e], preferred_element_type=F32)
            hu = jnp.dot(ts, wu_sc[e], preferred_element_type=F32)
            act = (hg * _sigmoid(hg) * hu * gs[:, e:e + 1]).astype(BF16)
            o = jnp.dot(act, wd_sc[e], preferred_element_type=F32)
            acc = o if acc is None else acc + o
        out_ref[...] = acc.astype(BF16)

    @pl.when(jnp.logical_not(active))
    def _idle():
        out_ref[...] = jnp.zeros_like(out_ref)


def _moe_combine_kernel(final, bt_out, pos_ref, x_ref, slot_ref, *refs):
    del pos_ref
    units = refs[0:MOE_TILE_UNITS]
    fn_ref, o_ref = refs[MOE_TILE_UNITS:]
    rows = x_ref.shape[0]
    os_local = jnp.concatenate([u[...] for u in units], axis=0)
    s_lane = lax.broadcasted_iota(jnp.int32, (rows, MOE_TILE_SLOTS), 1).astype(F32)
    perm_t = jnp.where(s_lane == slot_ref[:, 0:1], 1.0, 0.0).astype(BF16)
    y = x_ref[...] + jnp.dot(perm_t, os_local, preferred_element_type=F32)
    if final:
        y = _rms(y, fn_ref[...])
    if bt_out:
        tc = rows // bt_out
        for t in range(tc):
            o_ref[:, t, :] = y[t * bt_out:(t + 1) * bt_out, :]
    else:
        o_ref[...] = y


def _moe_plan(counts, n_tiles, n_mtiles):
    i32 = jnp.int32
    units = (counts + (MOE_UNIT - 1)) // MOE_UNIT
    loc_off = jnp.cumsum(units, axis=1) - units
    grp_tiles = (jnp.sum(units, axis=0) + (MOE_MTILE_UNITS - 1)) // MOE_MTILE_UNITS
    grp_end = jnp.cumsum(grp_tiles)
    tile_start = grp_end - grp_tiles
    before_c = jnp.cumsum(units, axis=0) - units
    u = jnp.arange(MOE_TILE_UNITS, dtype=i32)
    g_of = jnp.sum((u[None, :, None] >= (loc_off + units)[:, None, :]).astype(i32), axis=-1)
    valid = g_of < N_GROUPS
    g_c = jnp.minimum(g_of, N_GROUPS - 1)
    k = u[None, :] - jnp.take_along_axis(loc_off, g_c, axis=1)
    pos = MOE_MTILE_UNITS * tile_start[g_c] + jnp.take_along_axis(before_c, g_c, axis=1) + k
    pos = jnp.where(valid, pos, 0).astype(i32)
    local_id = (jnp.arange(n_tiles, dtype=i32)[:, None] * MOE_TILE_UNITS + u[None, :])
    n_pos = n_mtiles * MOE_MTILE_UNITS
    src = jnp.zeros((n_pos,), i32).at[jnp.where(valid, pos, n_pos).reshape(-1)].set(
        local_id.reshape(-1), mode="drop")
    j = jnp.arange(n_mtiles, dtype=i32)
    grp = jnp.minimum(jnp.sum((j[:, None] >= grp_end[None, :]).astype(i32), axis=-1), N_GROUPS - 1)
    return src, grp.astype(i32), grp_end[-1:].astype(i32), pos.reshape(-1)


def _moe(x1, w, l, final, bt_out=0):
    n_rows = x1.shape[0]
    rows = CHUNK_ROWS
    assert n_rows % rows == 0
    n_tiles = n_rows // rows
    n_mtiles = -(-(n_tiles * MOE_TILE_UNITS) // MOE_MTILE_UNITS) + N_GROUPS
    mrows = MOE_MTILE_UNITS * MOE_UNIT
    cparams = pltpu.CompilerParams(dimension_semantics=("arbitrary",), vmem_limit_bytes=VMEM_LIMIT_BYTES)

    row_spec = pl.BlockSpec((rows, D_MODEL), lambda i: (i, 0))
    ts, slot, cnt = pl.pallas_call(
        _moe_route_kernel,
        grid=(n_tiles,),
        in_specs=[row_spec, _layer_spec(w["norm2"], l), _layer_spec(w["moe_w_r"], l),
                  _layer_spec(w["moe_b_r"], l)],
        out_specs=[pl.BlockSpec((MOE_TILE_SLOTS, MOE_ROW_W), lambda i: (i, 0)),
                   pl.BlockSpec((rows, ROUTER_LANES), lambda i: (i, 0)),
                   pl.BlockSpec((8, ROUTER_LANES), lambda i: (i, 0))],
        out_shape=[jax.ShapeDtypeStruct((n_tiles * MOE_TILE_SLOTS, MOE_ROW_W), BF16),
                   jax.ShapeDtypeStruct((n_rows, ROUTER_LANES), F32),
                   jax.ShapeDtypeStruct((n_tiles * 8, ROUTER_LANES), F32)],
        scratch_shapes=[pltpu.VMEM((rows, rows), BF16)],
        compiler_params=cparams,
        name="moe_route",
    )(x1, w["norm2"], w["moe_w_r"], w["moe_b_r"])

    counts = cnt.reshape(n_tiles, 8, ROUTER_LANES)[:, 0, :N_GROUPS].astype(jnp.int32)
    src, grp, n_used, pos = _moe_plan(counts, n_tiles, n_mtiles)

    def unit_in(k):
        return pl.BlockSpec((MOE_UNIT, MOE_ROW_W), lambda j, src, grp, nt: (src[j * MOE_MTILE_UNITS + k], 0))

    def grp_w(shape):
        return pl.BlockSpec((None, EXP_PER_GROUP) + shape, lambda j, src, grp, nt: (l, grp[j], 0, 0))
    os_sorted = pl.pallas_call(
        _moe_expert_kernel,
        grid_spec=pltpu.PrefetchScalarGridSpec(
            num_scalar_prefetch=3,
            grid=(n_mtiles,),
            in_specs=[unit_in(k) for k in range(MOE_MTILE_UNITS)]
            + [grp_w((D_MODEL, EXPERT_FF)), grp_w((D_MODEL, EXPERT_FF)), grp_w((EXPERT_FF, D_MODEL))],
            out_specs=pl.BlockSpec((mrows, D_MODEL), lambda j, src, grp, nt: (j, 0)),
            scratch_shapes=[pltpu.VMEM((EXP_PER_GROUP, D_MODEL, EXPERT_FF), BF16),
                            pltpu.VMEM((EXP_PER_GROUP, D_MODEL, EXPERT_FF), BF16),
                            pltpu.VMEM((EXP_PER_GROUP, EXPERT_FF, D_MODEL), BF16)],
        ),
        out_shape=jax.ShapeDtypeStruct((n_mtiles * mrows, D_MODEL), BF16),
        compiler_params=cparams,
        name="moe_experts",
    )(src, grp, n_used, *([ts] * MOE_MTILE_UNITS), w["moe_w_gate"], w["moe_w_up"], w["moe_w_down"])

    def unit_out(u):
        return pl.BlockSpec((MOE_UNIT, D_MODEL), lambda i, pos: (pos[i * MOE_TILE_UNITS + u], 0))
    if bt_out:
        tc = rows // bt_out
        out_spec = pl.BlockSpec((bt_out, tc, D_MODEL), lambda i, pos: (0, i, 0))
        out_shape = jax.ShapeDtypeStruct((bt_out, n_tiles * tc, D_MODEL), F32)
    else:
        out_spec = pl.BlockSpec((rows, D_MODEL), lambda i, pos: (i, 0))
        out_shape = jax.ShapeDtypeStruct((n_rows, D_MODEL), F32)
    return pl.pallas_call(
        functools.partial(_moe_combine_kernel, final, bt_out),
        grid_spec=pltpu.PrefetchScalarGridSpec(
            num_scalar_prefetch=1,
            grid=(n_tiles,),
            in_specs=[pl.BlockSpec((rows, D_MODEL), lambda i, pos: (i, 0)),
                      pl.BlockSpec((rows, ROUTER_LANES), lambda i, pos: (i, 0))]
            + [unit_out(u) for u in range(MOE_TILE_UNITS)]
            + [pl.BlockSpec(w["final_norm"].shape, lambda i, pos: (0, 0))],
            out_specs=out_spec,
        ),
        out_shape=out_shape,
        compiler_params=cparams,
        name="moe_combine",
    )(pos, x1, slot, *([os_sorted] * MOE_TILE_UNITS), w["final_norm"])


def _block_diag(blocks):
    L, n, r, c = blocks.shape
    eye = jnp.eye(n, dtype=blocks.dtype)
    return (blocks[:, :, :, None, :] * eye[None, :, None, :, None]).reshape(L, n * r, n * c)


def _to_tm(a):
    a = jnp.swapaxes(a, 0, 1)
    return a.reshape((a.shape[0] * a.shape[1],) + a.shape[2:])


def _from_tm(a, b):
    return jnp.swapaxes(a.reshape((a.shape[0] // b, b) + a.shape[1:]), 0, 1)


def kernel(x_prompt, x_sample, state_rglru_h, state_rglru_conv, state_conf_conv, state_s5_re, state_s5_im, norm1, w_in, rg_conv_w, rg_conv_b, rg_wa, rg_ba, rg_wx, rg_bx, rg_lambda, cf_conv_w, cf_conv_b, cf_ln_g, cf_ln_b, s5_a_re, s5_a_im, s5_log_dt, s5_b_re, s5_b_im, s5_c_re, s5_c_im, s5_d, s5_w_glu, s5_b_glu, gn_rg, gn_cf, gn_s5, w_out, norm2, moe_w_grp, moe_b_grp, moe_w_exp, moe_b_exp, moe_w_gate, moe_w_up, moe_w_down, final_norm):
    L = w_in.shape[0]
    B, T, _ = x_prompt.shape
    DB, DT, _ = x_sample.shape

    ab_re, ab_im, bb_re, bb_im = _s5_prep(s5_a_re, s5_a_im, s5_log_dt,
                                          jnp.swapaxes(s5_b_re, 2, 3), jnp.swapaxes(s5_b_im, 2, 3))

    def row(a):
        return a.reshape(L, 1, -1)

    W = dict(
        norm1=row(norm1), w_in=w_in.astype(BF16),
        rg_conv_w=rg_conv_w, rg_conv_b=row(rg_conv_b),
        rg_wg=jnp.concatenate([_block_diag(rg_wa), _block_diag(rg_wx)], axis=-1).astype(BF16),
        rg_bg=jnp.concatenate([row(rg_ba), row(rg_bx)], axis=-1),
        rg_lambda=row(rg_lambda),
        cf_conv_w=cf_conv_w, cf_conv_b=row(cf_conv_b), cf_ln_g=row(cf_ln_g), cf_ln_b=row(cf_ln_b),
        s5_ab_re=row(ab_re), s5_ab_im=row(ab_im),
        s5_bb_re=_block_diag(bb_re).astype(BF16), s5_bb_im=_block_diag(bb_im).astype(BF16),
        s5_c_re=_block_diag(jnp.swapaxes(s5_c_re, 2, 3)).astype(BF16),
        s5_c_im=_block_diag(jnp.swapaxes(s5_c_im, 2, 3)).astype(BF16),
        s5_d=row(s5_d), s5_w_glu=s5_w_glu.astype(BF16), s5_b_glu=row(s5_b_glu),
        gn_rg=row(gn_rg), gn_cf=row(gn_cf), gn_s5=row(gn_s5), w_out=w_out.astype(BF16),
        norm2=row(norm2),
        moe_w_r=jnp.concatenate(
            [moe_w_grp, moe_w_exp.reshape(L, D_MODEL, N_EXPERTS),
             jnp.zeros((L, D_MODEL, ROUTER_LANES - N_GROUPS - N_EXPERTS), F32)], axis=-1),
        moe_b_r=jnp.concatenate(
            [moe_b_grp, moe_b_exp.reshape(L, N_EXPERTS),
             jnp.zeros((L, ROUTER_LANES - N_GROUPS - N_EXPERTS), F32)], axis=-1).reshape(L, 1, ROUTER_LANES),
        moe_w_gate=moe_w_gate, moe_w_up=moe_w_up, moe_w_down=moe_w_down,
    )

    xp = x_prompt
    xs = _to_tm(x_sample)
    tc_p = CHUNK_ROWS // B
    p_states, s_states = [], []
    W["final_norm"] = final_norm.reshape(1, D_MODEL)
    for l in range(L):
        st_p = (jnp.zeros((B, D_RNN), F32), jnp.zeros(((RG_CONV - 1) * B, D_RNN), F32),
                jnp.zeros(((CF_CONV - 1) * B, CF_W), F32), jnp.zeros((B, S5_N), F32),
                jnp.zeros((B, S5_N), F32))
        st_s = (state_rglru_h[l], _to_tm(state_rglru_conv[l]), _to_tm(state_conf_conv[l]),
                state_s5_re[l].reshape(DB, S5_N), state_s5_im[l].reshape(DB, S5_N))
        xp, *np_l = _mixer(xp, B, tc_p, st_p, W, l)
        xs, *ns_l = _mixer(xs, DB, DT, st_s, W, l)
        final = l == L - 1
        xp = _moe(xp, W, l, final, bt_out=B if final else 0)
        xs = _moe(xs, W, l, final)
        p_states.append(np_l)
        s_states.append(ns_l)

    def assemble(states, b):
        h = jnp.stack([s[0] for s in states])
        rc = jnp.stack([_from_tm(s[1], b) for s in states])
        cf = jnp.stack([_from_tm(s[2], b) for s in states])
        sre = jnp.stack([s[3].reshape(b, S5_GROUPS, S5_STATE) for s in states])
        sim = jnp.stack([s[4].reshape(b, S5_GROUPS, S5_STATE) for s in states])
        return h, rc, cf, sre, sim

    y_sample = _from_tm(xs, DB)
    return (xp, y_sample) + assemble(p_states, B) + assemble(s_states, DB)
```

```python
import functools

import jax
import jax.numpy as jnp
from jax import lax
from jax.experimental import pallas as pl
from jax.experimental.pallas import tpu as pltpu

F32 = jnp.float32
BF16 = jnp.bfloat16

D_MODEL = 1024
D_RNN = 512
RG_HEADS = 8
RG_CONV = 4
RG_C = 8.0
CF_W = 256
CF_CONV = 31
S5_W = 256
S5_GROUPS = 16
S5_GROUP_CH = 16
S5_STATE = 64
S5_N = S5_GROUPS * S5_STATE
P_IN = 2 * D_RNN + 2 * CF_W + S5_W
N_GROUPS = 4
EXP_PER_GROUP = 4
N_EXPERTS = 16
EXPERT_FF = 256
EPS = 1e-6

ROUTER_LANES = 128
VMEM_LIMIT_BYTES = 60000 * 1024
ROW_BLOCK = 64
CHUNK_ROWS = 512
MOE_UNIT = 32
MOE_TILE_UNITS = CHUNK_ROWS // MOE_UNIT + N_GROUPS
MOE_TILE_SLOTS = MOE_TILE_UNITS * MOE_UNIT
MOE_MTILE_UNITS = 16
MOE_ROW_W = D_MODEL + ROUTER_LANES


def _rms(x, g):
    return x * lax.rsqrt(jnp.mean(x * x, axis=-1, keepdims=True) + EPS) * g


def _softplus(z):
    return jnp.maximum(z, 0.0) + jnp.log1p(jnp.exp(-jnp.abs(z)))


def _sigmoid(z):
    return 0.5 * jnp.tanh(0.5 * z) + 0.5


def _row_blocks(rows, rb, fn, unroll=False):
    if unroll:
        for i in range(rows // rb):
            fn(i * rb)
        return

    def body(i, carry):
        fn(pl.multiple_of(i * rb, rb))
        return carry
    lax.fori_loop(0, rows // rb, body, 0)


def _s5_prep_kernel(are_ref, aim_ref, ldt_ref, bre_ref, bim_ref,
                    abre_ref, abim_ref, bbre_ref, bbim_ref):
    a_re = are_ref[...]
    a_im = aim_ref[...]
    dt = jnp.exp(ldt_ref[...])
    mag = jnp.exp(a_re * dt)
    ab_re = mag * jnp.cos(a_im * dt)
    ab_im = mag * jnp.sin(a_im * dt)
    den = a_re * a_re + a_im * a_im
    nr = ab_re - 1.0
    ni = ab_im
    k_re = (nr * a_re + ni * a_im) / den
    k_im = (ni * a_re - nr * a_im) / den
    abre_ref[...] = ab_re
    abim_ref[...] = ab_im
    b_re = bre_ref[...]
    b_im = bim_ref[...]
    bbre_ref[...] = k_re[:, None, :] * b_re - k_im[:, None, :] * b_im
    bbim_ref[...] = k_re[:, None, :] * b_im + k_im[:, None, :] * b_re


def _s5_prep(a_re, a_im, log_dt, b_re_ghp, b_im_ghp):
    L = a_re.shape[0]
    gp = pl.BlockSpec((None, S5_GROUPS, S5_STATE), lambda l: (l, 0, 0))
    ghp = pl.BlockSpec((None, S5_GROUPS, S5_GROUP_CH, S5_STATE), lambda l: (l, 0, 0, 0))
    return pl.pallas_call(
        _s5_prep_kernel,
        grid=(L,),
        in_specs=[gp, gp, pl.BlockSpec((None, S5_GROUPS, 1), lambda l: (l, 0, 0)), ghp, ghp],
        out_specs=[gp, gp, ghp, ghp],
        out_shape=[jax.ShapeDtypeStruct((L, S5_GROUPS, S5_STATE), F32)] * 2
        + [jax.ShapeDtypeStruct((L, S5_GROUPS, S5_GROUP_CH, S5_STATE), F32)] * 2,
        name="s5_prep",
    )(a_re, a_im, log_dt[..., None], b_re_ghp, b_im_ghp)


def _mixer_kernel(bt, tc, n_chunks, x_bm,
                  x_ref, h0_ref, rb_ref, fb_ref, sre_ref, sim_ref,
                  n1_ref, win_ref, rgw_ref, rgb_ref, wg_ref, bg_ref, lam_ref,
                  cfw_ref, cfb_ref, lng_ref, lnb_ref,
                  abre_ref, abim_ref, bbre_ref, bbim_ref, cre_ref, cim_ref, d_ref,
                  wglu_ref, bglu_ref, gnrg_ref, gncf_ref, gns5_ref, wout_ref,
                  x1_ref, hl_ref, rbn_ref, fbn_ref, sren_ref, simn_ref,
                  xpad, upad, h_sc, sre_sc, sim_sc, hn_sc, proj_sc, xc_sc, gate_sc,
                  a_sc, b_sc, v_sc, y_sc, zb_sc, gl_sc, bre_sc, bim_sc, mix_sc, *maybe_xtm):
    rows = tc * bt
    rb = ROW_BLOCK
    rg_hist = (RG_CONV - 1) * bt
    cf_hist = (CF_CONV - 1) * bt
    c = pl.program_id(0)

    @pl.when(c == 0)
    def _init():
        xpad[0:rg_hist, :] = rb_ref[...]
        upad[0:cf_hist, :] = fb_ref[...]
        h_sc[...] = h0_ref[...]
        sre_sc[...] = sre_ref[...]
        sim_sc[...] = sim_ref[...]

    o_cf = 2 * D_RNN
    o_s5 = o_cf + 2 * CF_W

    if x_bm:
        (x_src,) = maybe_xtm
        for t in range(tc):
            x_src[t * bt:(t + 1) * bt, :] = x_ref[:, t, :]
    else:
        x_src = x_ref

    def norm_in(r0):
        x = x_src[r0:r0 + rb, :]
        hn_sc[r0:r0 + rb, :] = _rms(x, n1_ref[...]).astype(BF16)
    _row_blocks(rows, rb, norm_in, unroll=True)
    proj_sc[...] = jnp.dot(hn_sc[...], win_ref[...], preferred_element_type=F32)

    xpad[rg_hist:rg_hist + rows, :] = proj_sc[:, 0:D_RNN]

    def rg_conv(r0):
        acc = rgb_ref[...] + rgw_ref[0:1, :] * xpad[r0:r0 + rb, :]
        for k in range(1, RG_CONV):
            acc = acc + rgw_ref[k:k + 1, :] * xpad[r0 + k * bt:r0 + k * bt + rb, :]
        xc_sc[r0:r0 + rb, :] = acc
    _row_blocks(rows, rb, rg_conv, unroll=True)

    gate_sc[...] = jnp.dot(xc_sc[...].astype(BF16), wg_ref[...],
                           preferred_element_type=F32) + bg_ref[...]

    su_bf = proj_sc[:, o_s5:o_s5 + S5_W].astype(BF16)
    bre_sc[...] = jnp.dot(su_bf, bbre_ref[...], preferred_element_type=F32)
    bim_sc[...] = jnp.dot(su_bf, bbim_ref[...], preferred_element_type=F32)

    neg_c_sp = -RG_C * _softplus(-lam_ref[...])

    def rg_coeffs(r0):
        g = gate_sc[r0:r0 + rb, :]
        r = _sigmoid(g[:, 0:D_RNN])
        i = _sigmoid(g[:, D_RNN:2 * D_RNN])
        a = jnp.exp(neg_c_sp * r)
        a_sc[r0:r0 + rb, :] = a
        b_sc[r0:r0 + rb, :] = jnp.sqrt(1.0 - a * a) * (i * xc_sc[r0:r0 + rb, :])
    _row_blocks(rows, rb, rg_coeffs, unroll=True)

    def cf_glu(r0):
        ca = proj_sc[r0:r0 + rb, o_cf:o_cf + CF_W]
        cg = proj_sc[r0:r0 + rb, o_cf + CF_W:o_cf + 2 * CF_W]
        upad[cf_hist + r0:cf_hist + r0 + rb, :] = ca * _sigmoid(cg)
    _row_blocks(rows, rb, cf_glu, unroll=True)

    h = h_sc[...]
    for t in range(tc):
        r0 = t * bt
        h = a_sc[r0:r0 + bt, :] * h + b_sc[r0:r0 + bt, :]
        b_sc[r0:r0 + bt, :] = h
    h_sc[...] = h
    hl_ref[...] = h

    a_r = jnp.broadcast_to(abre_ref[...], (bt, S5_N))
    a_i = jnp.broadcast_to(abim_ref[...], (bt, S5_N))
    s_r = sre_sc[...]
    s_i = sim_sc[...]
    for t in range(tc):
        r0 = t * bt
        n_r = a_r * s_r - a_i * s_i + bre_sc[r0:r0 + bt, :]
        n_i = a_r * s_i + a_i * s_r + bim_sc[r0:r0 + bt, :]
        bre_sc[r0:r0 + bt, :] = n_r
        bim_sc[r0:r0 + bt, :] = n_i
        s_r, s_i = n_r, n_i
    sre_sc[...] = s_r
    sim_sc[...] = s_i
    sren_ref[...] = s_r
    simn_ref[...] = s_i

    def cf_conv(r0):
        acc = cfb_ref[...] + cfw_ref[0:1, :] * upad[r0:r0 + rb, :]
        for k in range(1, CF_CONV):
            acc = acc + cfw_ref[k:k + 1, :] * upad[r0 + k * bt:r0 + k * bt + rb, :]
        v_sc[r0:r0 + rb, :] = acc
    _row_blocks(rows, rb, cf_conv, unroll=True)

    def rg_out(r0):
        h = b_sc[r0:r0 + rb, :]
        yb = proj_sc[r0:r0 + rb, D_RNN:2 * D_RNN]
        mix_sc[r0:r0 + rb, 0:D_RNN] = _rms(h * jax.nn.gelu(yb), gnrg_ref[...]).astype(BF16)
    _row_blocks(rows, rb, rg_out, unroll=True)

    def cf_out(r0):
        v = v_sc[r0:r0 + rb, :]
        mu = jnp.mean(v, axis=-1, keepdims=True)
        cen = v - mu
        var = jnp.mean(cen * cen, axis=-1, keepdims=True)
        vn = cen * lax.rsqrt(var + EPS) * lng_ref[...] + lnb_ref[...]
        mix_sc[r0:r0 + rb, D_RNN:D_RNN + CF_W] = _rms(vn * _sigmoid(vn), gncf_ref[...]).astype(BF16)
    _row_blocks(rows, rb, cf_out, unroll=True)

    y_sc[...] = (jnp.dot(bre_sc[...].astype(BF16), cre_ref[...], preferred_element_type=F32)
                 - jnp.dot(bim_sc[...].astype(BF16), cim_ref[...], preferred_element_type=F32))

    def s5_gelu(r0):
        su = proj_sc[r0:r0 + rb, o_s5:o_s5 + S5_W]
        z = jax.nn.gelu(y_sc[r0:r0 + rb, :] + d_ref[...] * su)
        y_sc[r0:r0 + rb, :] = z
        zb_sc[r0:r0 + rb, :] = z.astype(BF16)
    _row_blocks(rows, rb, s5_gelu, unroll=True)
    gl_sc[...] = jnp.dot(zb_sc[...], wglu_ref[...], preferred_element_type=F32) + bglu_ref[...]

    def s5_out(r0):
        oc = y_sc[r0:r0 + rb, :] * _sigmoid(gl_sc[r0:r0 + rb, :])
        mix_sc[r0:r0 + rb, D_RNN + CF_W:D_MODEL] = _rms(oc, gns5_ref[...]).astype(BF16)
    _row_blocks(rows, rb, s5_out, unroll=True)

    x1_ref[...] = x_src[...] + jnp.dot(mix_sc[...], wout_ref[...], preferred_element_type=F32)

    rbn_ref[...] = xpad[rows:rows + rg_hist, :]
    fbn_ref[...] = upad[rows:rows + cf_hist, :]
    if n_chunks > 1:
        xpad[0:rg_hist, :] = xpad[rows:rows + rg_hist, :]
        upad[0:cf_hist, :] = upad[rows:rows + cf_hist, :]


def _const_spec(shape):
    nd = len(shape)
    return pl.BlockSpec(shape, lambda *_: (0,) * nd)


def _layer_spec(a, l):
    nd = a.ndim - 1
    return pl.BlockSpec((None,) + a.shape[1:], lambda *_: (l,) + (0,) * nd)


def _mixer(x, bt, tc, n_chunks, state, w, l, x_blk0=0, out_rows=None, out=None, out_blk0=0):
    x_bm = x.ndim == 3
    rows = tc * bt
    assert rows % ROW_BLOCK == 0 and bt % 8 == 0
    assert n_chunks == 1 or tc >= CF_CONV - 1
    h0, rbuf, fbuf, s_re, s_im = state
    weights = [w["norm1"], w["w_in"], w["rg_conv_w"], w["rg_conv_b"], w["rg_wg"], w["rg_bg"],
               w["rg_lambda"], w["cf_conv_w"], w["cf_conv_b"], w["cf_ln_g"], w["cf_ln_b"],
               w["s5_ab_re"], w["s5_ab_im"], w["s5_bb_re"], w["s5_bb_im"], w["s5_c_re"], w["s5_c_im"],
               w["s5_d"], w["s5_w_glu"], w["s5_b_glu"], w["gn_rg"], w["gn_cf"], w["gn_s5"], w["w_out"]]
    states = [h0, rbuf, fbuf, s_re, s_im]
    if x_bm:
        x_spec = pl.BlockSpec((bt, tc, D_MODEL), lambda c: (0, c, 0))
    else:
        x_spec = pl.BlockSpec((rows, D_MODEL), lambda c: (x_blk0 + c, 0))
    in_specs = ([x_spec] + [_const_spec(a.shape) for a in states]
                + [_layer_spec(a, l) for a in weights])
    operands = [x, *states, *weights]
    aliases = {}
    if out is not None:
        in_specs.append(pl.BlockSpec(memory_space=pl.ANY))
        aliases = {len(operands): 0}
        operands.append(out)
        out_rows = out.shape[0]
    n_in = len(operands)
    out_shape = [jax.ShapeDtypeStruct((out_rows, D_MODEL), F32)] + [
        jax.ShapeDtypeStruct(a.shape, F32) for a in states]
    out_specs = ([pl.BlockSpec((rows, D_MODEL), lambda c: (out_blk0 + c, 0))]
                 + [_const_spec(a.shape) for a in states])

    def body(*refs):
        keep = refs[:n_in - len(aliases)] + refs[n_in:]
        _mixer_kernel(bt, tc, n_chunks, x_bm, *keep)
    scratch = [
        pltpu.VMEM(((tc + RG_CONV - 1) * bt, D_RNN), F32),
        pltpu.VMEM(((tc + CF_CONV - 1) * bt, CF_W), F32),
        pltpu.VMEM((bt, D_RNN), F32),
        pltpu.VMEM((bt, S5_N), F32),
        pltpu.VMEM((bt, S5_N), F32),
        pltpu.VMEM((rows, D_MODEL), BF16),
        pltpu.VMEM((rows, P_IN), F32),
        pltpu.VMEM((rows, D_RNN), F32),
        pltpu.VMEM((rows, 2 * D_RNN), F32),
        pltpu.VMEM((rows, D_RNN), F32),
        pltpu.VMEM((rows, D_RNN), F32),
        pltpu.VMEM((rows, CF_W), F32),
        pltpu.VMEM((rows, S5_W), F32),
        pltpu.VMEM((rows, S5_W), BF16),
        pltpu.VMEM((rows, S5_W), F32),
        pltpu.VMEM((rows, S5_N), F32),
        pltpu.VMEM((rows, S5_N), F32),
        pltpu.VMEM((rows, D_MODEL), BF16),
    ]
    if x_bm:
        scratch.append(pltpu.VMEM((rows, D_MODEL), F32))
    return pl.pallas_call(
        body,
        grid=(n_chunks,),
        in_specs=in_specs,
        out_specs=out_specs,
        out_shape=out_shape,
        scratch_shapes=scratch,
        input_output_aliases=aliases,
        compiler_params=pltpu.CompilerParams(dimension_semantics=("arbitrary",),
                                             vmem_limit_bytes=VMEM_LIMIT_BYTES),
        name="mixer",
    )(*operands)


def _route(logits):
    shape = logits.shape
    lane = lax.broadcasted_iota(jnp.int32, shape, 1).astype(F32)
    neg = jnp.float32(-jnp.inf)
    is_grp = lane < N_GROUPS
    gl = jnp.where(is_grp, logits, neg)
    gmax = jnp.max(gl, axis=-1, keepdims=True)
    gidx = jnp.min(jnp.where(gl == gmax, lane, ROUTER_LANES), axis=-1, keepdims=True)
    p_sel = 1.0 / jnp.sum(jnp.where(is_grp, jnp.exp(logits - gmax), 0.0), axis=-1, keepdims=True)
    lo = N_GROUPS + EXP_PER_GROUP * gidx
    in_grp = jnp.logical_and(lane >= lo, lane < lo + EXP_PER_GROUP)
    e1 = jnp.where(in_grp, logits, neg)
    v1 = jnp.max(e1, axis=-1, keepdims=True)
    i1 = jnp.min(jnp.where(e1 == v1, lane, ROUTER_LANES), axis=-1, keepdims=True)
    e2 = jnp.where(lane == i1, neg, e1)
    v2 = jnp.max(e2, axis=-1, keepdims=True)
    i2 = jnp.min(jnp.where(e2 == v2, lane, ROUTER_LANES), axis=-1, keepdims=True)
    ex = jnp.exp(v2 - v1)
    w1 = 1.0 / (1.0 + ex)
    w2 = ex * w1
    gate = jnp.where(lane == i1, p_sel * w1, jnp.where(lane == i2, p_sel * w2, 0.0))
    return gate, gidx


def _moe_route_kernel(x_ref, n2_ref, wr_ref, br_ref, ts_ref, slot_ref, cnt_ref, tri_sc):
    rows = x_ref.shape[0]
    x = x_ref[...]
    t = _rms(x, n2_ref[...])
    t_hi = t.astype(BF16)
    t_lo = (t - t_hi.astype(F32)).astype(BF16)
    wr = wr_ref[...]
    wr_hi = wr.astype(BF16)
    wr_lo = (wr - wr_hi.astype(F32)).astype(BF16)
    logits = (jnp.dot(t_hi, wr_hi, preferred_element_type=F32)
              + jnp.dot(t_lo, wr_hi, preferred_element_type=F32)
              + jnp.dot(t_hi, wr_lo, preferred_element_type=F32)) + br_ref[...]
    gate, gidx = _route(logits)

    lane = lax.broadcasted_iota(jnp.int32, (rows, ROUTER_LANES), 1).astype(F32)
    onehot = jnp.where(lane == gidx, 1.0, 0.0)

    @pl.when(pl.program_id(0) == 0)
    def _build_tri():
        r_i = lax.broadcasted_iota(jnp.int32, (rows, rows), 0)
        c_i = lax.broadcasted_iota(jnp.int32, (rows, rows), 1)
        tri_sc[...] = jnp.where(c_i < r_i, 1.0, 0.0).astype(BF16)
    before = jnp.dot(tri_sc[...], onehot.astype(BF16), preferred_element_type=F32)
    rank = jnp.sum(before * onehot, axis=-1, keepdims=True)
    counts = jnp.sum(onehot, axis=0, keepdims=True)

    units = jnp.ceil(counts * (1.0 / MOE_UNIT))
    first = MOE_UNIT * (pltpu.roll(units, 1, 1) + pltpu.roll(units, 2, 1) + pltpu.roll(units, 3, 1))
    slot = jnp.sum(onehot * first, axis=-1, keepdims=True) + rank
    slot_b = jnp.broadcast_to(slot, (rows, ROUTER_LANES))
    slot_t = jnp.transpose(slot_b)[0:1, :]
    s_sub = lax.broadcasted_iota(jnp.int32, (MOE_TILE_SLOTS, rows), 0).astype(F32)
    perm = jnp.where(s_sub == slot_t, 1.0, 0.0).astype(BF16)

    ts_ref[:, 0:D_MODEL] = jnp.dot(perm, t_hi, preferred_element_type=F32).astype(BF16)
    gate4 = jnp.zeros_like(gate)
    for gi in range(N_GROUPS):
        shift = ROUTER_LANES - (N_GROUPS + EXP_PER_GROUP * gi)
        gate4 = gate4 + jnp.where(gidx == gi, pltpu.roll(gate, shift, 1), 0.0)
    g_hi = gate4.astype(BF16).astype(F32)
    packed = (g_hi + pltpu.roll(gate4 - g_hi, ROUTER_LANES // 2, 1)).astype(BF16)
    ts_ref[:, D_MODEL:MOE_ROW_W] = jnp.dot(perm, packed, preferred_element_type=F32).astype(BF16)
    slot_ref[...] = slot_b
    cnt_ref[...] = jnp.broadcast_to(counts, cnt_ref.shape)


def _moe_expert_kernel(src_ref, grp_ref, nt_ref, *refs):
    units = refs[0:MOE_MTILE_UNITS]
    wg_ref, wu_ref, wd_ref, out_ref, wg_sc, wu_sc, wd_sc = refs[MOE_MTILE_UNITS:]
    del src_ref
    j = pl.program_id(0)
    active = j < nt_ref[0]

    @pl.when(jnp.logical_and(active, jnp.logical_or(j == 0, grp_ref[j] != grp_ref[jnp.maximum(j - 1, 0)])))
    def _cast_weights():
        for e in range(EXP_PER_GROUP):
            wg_sc[e] = wg_ref[e].astype(BF16)
            wu_sc[e] = wu_ref[e].astype(BF16)
            wd_sc[e] = wd_ref[e].astype(BF16)

    @pl.when(active)
    def _compute():
        tsg = jnp.concatenate([u[...] for u in units], axis=0)
        ts = tsg[:, 0:D_MODEL]
        gp = tsg[:, D_MODEL:MOE_ROW_W].astype(F32)
        gs = gp + pltpu.roll(gp, ROUTER_LANES // 2, 1)
        acc = None
        for e in range(EXP_PER_GROUP):
            hg = jnp.dot(ts, wg_sc[e], preferred_element_type=F32)
            hu = jnp.dot(ts, wu_sc[e], preferred_element_type=F32)
            act = (hg * _sigmoid(hg) * hu * gs[:, e:e + 1]).astype(BF16)
            o = jnp.dot(act, wd_sc[e], preferred_element_type=F32)
            acc = o if acc is None else acc + o
        out_ref[...] = acc.astype(BF16)

    @pl.when(jnp.logical_not(active))
    def _idle():
        out_ref[...] = jnp.zeros_like(out_ref)


def _moe_combine_kernel(final, bt_out, n_bm_tiles, pos_ref, x_ref, slot_ref, *refs):
    del pos_ref
    units = refs[0:MOE_TILE_UNITS]
    fn_ref = refs[MOE_TILE_UNITS]
    outs = refs[MOE_TILE_UNITS + 1:]
    rows = x_ref.shape[0]
    os_local = jnp.concatenate([u[...] for u in units], axis=0)
    s_lane = lax.broadcasted_iota(jnp.int32, (rows, MOE_TILE_SLOTS), 1).astype(F32)
    perm_t = jnp.where(s_lane == slot_ref[:, 0:1], 1.0, 0.0).astype(BF16)
    y = x_ref[...] + jnp.dot(perm_t, os_local, preferred_element_type=F32)
    if final:
        y = _rms(y, fn_ref[...])
    if bt_out:
        o_bm, o_rest = outs
        i = pl.program_id(0)

        @pl.when(i < n_bm_tiles)
        def _batch_major():
            for t in range(rows // bt_out):
                o_bm[:, t, :] = y[t * bt_out:(t + 1) * bt_out, :]

        @pl.when(i >= n_bm_tiles)
        def _rest():
            o_rest[...] = y
    else:
        (o_ref,) = outs
        o_ref[...] = y


def _moe_plan(counts, n_tiles, n_mtiles):
    i32 = jnp.int32
    units = (counts + (MOE_UNIT - 1)) // MOE_UNIT
    loc_off = jnp.cumsum(units, axis=1) - units
    grp_tiles = (jnp.sum(units, axis=0) + (MOE_MTILE_UNITS - 1)) // MOE_MTILE_UNITS
    grp_end = jnp.cumsum(grp_tiles)
    tile_start = grp_end - grp_tiles
    before_c = jnp.cumsum(units, axis=0) - units
    u = jnp.arange(MOE_TILE_UNITS, dtype=i32)
    g_of = jnp.sum((u[None, :, None] >= (loc_off + units)[:, None, :]).astype(i32), axis=-1)
    valid = g_of < N_GROUPS
    g_c = jnp.minimum(g_of, N_GROUPS - 1)
    k = u[None, :] - jnp.take_along_axis(loc_off, g_c, axis=1)
    pos = MOE_MTILE_UNITS * tile_start[g_c] + jnp.take_along_axis(before_c, g_c, axis=1) + k
    pos = jnp.where(valid, pos, 0).astype(i32)
    local_id = (jnp.arange(n_tiles, dtype=i32)[:, None] * MOE_TILE_UNITS + u[None, :])
    n_pos = n_mtiles * MOE_MTILE_UNITS
    src = jnp.zeros((n_pos,), i32).at[jnp.where(valid, pos, n_pos).reshape(-1)].set(
        local_id.reshape(-1), mode="drop")
    j = jnp.arange(n_mtiles, dtype=i32)
    grp = jnp.minimum(jnp.sum((j[:, None] >= grp_end[None, :]).astype(i32), axis=-1), N_GROUPS - 1)
    return src, grp.astype(i32), grp_end[-1:].astype(i32), pos.reshape(-1)


def _moe(x1, w, l, final, bt_out=0, n_bm_tiles=0):
    n_rows = x1.shape[0]
    rows = CHUNK_ROWS
    assert n_rows % rows == 0
    n_tiles = n_rows // rows
    n_mtiles = -(-(n_tiles * MOE_TILE_UNITS) // MOE_MTILE_UNITS) + N_GROUPS
    mrows = MOE_MTILE_UNITS * MOE_UNIT
    cparams = pltpu.CompilerParams(dimension_semantics=("arbitrary",), vmem_limit_bytes=VMEM_LIMIT_BYTES)

    row_spec = pl.BlockSpec((rows, D_MODEL), lambda i: (i, 0))
    ts, slot, cnt = pl.pallas_call(
        _moe_route_kernel,
        grid=(n_tiles,),
        in_specs=[row_spec, _layer_spec(w["norm2"], l), _layer_spec(w["moe_w_r"], l),
                  _layer_spec(w["moe_b_r"], l)],
        out_specs=[pl.BlockSpec((MOE_TILE_SLOTS, MOE_ROW_W), lambda i: (i, 0)),
                   pl.BlockSpec((rows, ROUTER_LANES), lambda i: (i, 0)),
                   pl.BlockSpec((8, ROUTER_LANES), lambda i: (i, 0))],
        out_shape=[jax.ShapeDtypeStruct((n_tiles * MOE_TILE_SLOTS, MOE_ROW_W), BF16),
                   jax.ShapeDtypeStruct((n_rows, ROUTER_LANES), F32),
                   jax.ShapeDtypeStruct((n_tiles * 8, ROUTER_LANES), F32)],
        scratch_shapes=[pltpu.VMEM((rows, rows), BF16)],
        compiler_params=cparams,
        name="moe_route",
    )(x1, w["norm2"], w["moe_w_r"], w["moe_b_r"])

    counts = cnt.reshape(n_tiles, 8, ROUTER_LANES)[:, 0, :N_GROUPS].astype(jnp.int32)
    src, grp, n_used, pos = _moe_plan(counts, n_tiles, n_mtiles)

    def unit_in(k):
        return pl.BlockSpec((MOE_UNIT, MOE_ROW_W), lambda j, src, grp, nt: (src[j * MOE_MTILE_UNITS + k], 0))

    def grp_w(shape):
        return pl.BlockSpec((None, EXP_PER_GROUP) + shape, lambda j, src, grp, nt: (l, grp[j], 0, 0))
    os_sorted = pl.pallas_call(
        _moe_expert_kernel,
        grid_spec=pltpu.PrefetchScalarGridSpec(
            num_scalar_prefetch=3,
            grid=(n_mtiles,),
            in_specs=[unit_in(k) for k in range(MOE_MTILE_UNITS)]
            + [grp_w((D_MODEL, EXPERT_FF)), grp_w((D_MODEL, EXPERT_FF)), grp_w((EXPERT_FF, D_MODEL))],
            out_specs=pl.BlockSpec((mrows, D_MODEL), lambda j, src, grp, nt: (j, 0)),
            scratch_shapes=[pltpu.VMEM((EXP_PER_GROUP, D_MODEL, EXPERT_FF), BF16),
                            pltpu.VMEM((EXP_PER_GROUP, D_MODEL, EXPERT_FF), BF16),
                            pltpu.VMEM((EXP_PER_GROUP, EXPERT_FF, D_MODEL), BF16)],
        ),
        out_shape=jax.ShapeDtypeStruct((n_mtiles * mrows, D_MODEL), BF16),
        compiler_params=cparams,
        name="moe_experts",
    )(src, grp, n_used, *([ts] * MOE_MTILE_UNITS), w["moe_w_gate"], w["moe_w_up"], w["moe_w_down"])

    def unit_out(u):
        return pl.BlockSpec((MOE_UNIT, D_MODEL), lambda i, pos: (pos[i * MOE_TILE_UNITS + u], 0))
    if bt_out:
        tc = rows // bt_out
        last = n_bm_tiles - 1
        out_spec = [pl.BlockSpec((bt_out, tc, D_MODEL), lambda i, pos: (0, jnp.minimum(i, last), 0)),
                    pl.BlockSpec((rows, D_MODEL), lambda i, pos: (jnp.maximum(i - n_bm_tiles, 0), 0))]
        out_shape = [jax.ShapeDtypeStruct((bt_out, n_bm_tiles * tc, D_MODEL), F32),
                     jax.ShapeDtypeStruct(((n_tiles - n_bm_tiles) * rows, D_MODEL), F32)]
    else:
        out_spec = pl.BlockSpec((rows, D_MODEL), lambda i, pos: (i, 0))
        out_shape = jax.ShapeDtypeStruct((n_rows, D_MODEL), F32)
    return pl.pallas_call(
        functools.partial(_moe_combine_kernel, final, bt_out, n_bm_tiles),
        grid_spec=pltpu.PrefetchScalarGridSpec(
            num_scalar_prefetch=1,
            grid=(n_tiles,),
            in_specs=[pl.BlockSpec((rows, D_MODEL), lambda i, pos: (i, 0)),
                      pl.BlockSpec((rows, ROUTER_LANES), lambda i, pos: (i, 0))]
            + [unit_out(u) for u in range(MOE_TILE_UNITS)]
            + [pl.BlockSpec(w["final_norm"].shape, lambda i, pos: (0, 0))],
            out_specs=out_spec,
        ),
        out_shape=out_shape,
        compiler_params=cparams,
        name="moe_combine",
    )(pos, x1, slot, *([os_sorted] * MOE_TILE_UNITS), w["final_norm"])


def _block_diag(blocks):
    L, n, r, c = blocks.shape
    eye = jnp.eye(n, dtype=blocks.dtype)
    return (blocks[:, :, :, None, :] * eye[None, :, None, :, None]).reshape(L, n * r, n * c)


def _to_tm(a):
    a = jnp.swapaxes(a, 0, 1)
    return a.reshape((a.shape[0] * a.shape[1],) + a.shape[2:])


def _from_tm(a, b):
    return jnp.swapaxes(a.reshape((a.shape[0] // b, b) + a.shape[1:]), 0, 1)


def kernel(x_prompt, x_sample, state_rglru_h, state_rglru_conv, state_conf_conv, state_s5_re, state_s5_im, norm1, w_in, rg_conv_w, rg_conv_b, rg_wa, rg_ba, rg_wx, rg_bx, rg_lambda, cf_conv_w, cf_conv_b, cf_ln_g, cf_ln_b, s5_a_re, s5_a_im, s5_log_dt, s5_b_re, s5_b_im, s5_c_re, s5_c_im, s5_d, s5_w_glu, s5_b_glu, gn_rg, gn_cf, gn_s5, w_out, norm2, moe_w_grp, moe_b_grp, moe_w_exp, moe_b_exp, moe_w_gate, moe_w_up, moe_w_down, final_norm):
    L = w_in.shape[0]
    B, T, _ = x_prompt.shape
    DB, DT, _ = x_sample.shape

    ab_re, ab_im, bb_re, bb_im = _s5_prep(s5_a_re, s5_a_im, s5_log_dt,
                                          jnp.swapaxes(s5_b_re, 2, 3), jnp.swapaxes(s5_b_im, 2, 3))

    def row(a):
        return a.reshape(L, 1, -1)

    W = dict(
        norm1=row(norm1), w_in=w_in.astype(BF16),
        rg_conv_w=rg_conv_w, rg_conv_b=row(rg_conv_b),
        rg_wg=jnp.concatenate([_block_diag(rg_wa), _block_diag(rg_wx)], axis=-1).astype(BF16),
        rg_bg=jnp.concatenate([row(rg_ba), row(rg_bx)], axis=-1),
        rg_lambda=row(rg_lambda),
        cf_conv_w=cf_conv_w, cf_conv_b=row(cf_conv_b), cf_ln_g=row(cf_ln_g), cf_ln_b=row(cf_ln_b),
        s5_ab_re=row(ab_re), s5_ab_im=row(ab_im),
        s5_bb_re=_block_diag(bb_re).astype(BF16), s5_bb_im=_block_diag(bb_im).astype(BF16),
        s5_c_re=_block_diag(jnp.swapaxes(s5_c_re, 2, 3)).astype(BF16),
        s5_c_im=_block_diag(jnp.swapaxes(s5_c_im, 2, 3)).astype(BF16),
        s5_d=row(s5_d), s5_w_glu=s5_w_glu.astype(BF16), s5_b_glu=row(s5_b_glu),
        gn_rg=row(gn_rg), gn_cf=row(gn_cf), gn_s5=row(gn_s5), w_out=w_out.astype(BF16),
        norm2=row(norm2),
        moe_w_r=jnp.concatenate(
            [moe_w_grp, moe_w_exp.reshape(L, D_MODEL, N_EXPERTS),
             jnp.zeros((L, D_MODEL, ROUTER_LANES - N_GROUPS - N_EXPERTS), F32)], axis=-1),
        moe_b_r=jnp.concatenate(
            [moe_b_grp, moe_b_exp.reshape(L, N_EXPERTS),
             jnp.zeros((L, ROUTER_LANES - N_GROUPS - N_EXPERTS), F32)], axis=-1).reshape(L, 1, ROUTER_LANES),
        moe_w_gate=moe_w_gate, moe_w_up=moe_w_up, moe_w_down=moe_w_down,
    )

    tc_p = CHUNK_ROWS // B
    n_p = (B * T) // CHUNK_ROWS
    n_s = (DB * DT) // CHUNK_ROWS
    assert n_p * CHUNK_ROWS == B * T and n_s == 1 and DB * DT == CHUNK_ROWS
    all_rows = (n_p + n_s) * CHUNK_ROWS
    x_p, x_s, s_blk0 = x_prompt, _to_tm(x_sample), 0
    p_states, s_states = [], []
    W["final_norm"] = final_norm.reshape(1, D_MODEL)
    st_p = (jnp.zeros((B, D_RNN), F32), jnp.zeros(((RG_CONV - 1) * B, D_RNN), F32),
            jnp.zeros(((CF_CONV - 1) * B, CF_W), F32), jnp.zeros((B, S5_N), F32),
            jnp.zeros((B, S5_N), F32))
    for l in range(L):
        st_s = (state_rglru_h[l], _to_tm(state_rglru_conv[l]), _to_tm(state_conf_conv[l]),
                state_s5_re[l].reshape(DB, S5_N), state_s5_im[l].reshape(DB, S5_N))
        x1, *np_l = _mixer(x_p, B, tc_p, n_p, st_p, W, l, out_rows=all_rows)
        x1, *ns_l = _mixer(x_s, DB, DT, n_s, st_s, W, l, x_blk0=s_blk0, out=x1, out_blk0=n_p)
        p_states.append(np_l)
        s_states.append(ns_l)
        if l < L - 1:
            x_p = x_s = _moe(x1, W, l, False)
            s_blk0 = n_p
        else:
            y_prompt, y_s = _moe(x1, W, l, True, bt_out=B, n_bm_tiles=n_p)

    def assemble(states, b):
        h = jnp.stack([s[0] for s in states])
        rc = jnp.stack([_from_tm(s[1], b) for s in states])
        cf = jnp.stack([_from_tm(s[2], b) for s in states])
        sre = jnp.stack([s[3].reshape(b, S5_GROUPS, S5_STATE) for s in states])
        sim = jnp.stack([s[4].reshape(b, S5_GROUPS, S5_STATE) for s in states])
        return h, rc, cf, sre, sim

    y_sample = _from_tm(y_s, DB)
    return (y_prompt, y_sample) + assemble(p_states, B) + assemble(s_states, DB)
```

```python
import functools

import jax
import jax.numpy as jnp
from jax import lax
from jax.experimental import pallas as pl
from jax.experimental.pallas import tpu as pltpu

F32 = jnp.float32
BF16 = jnp.bfloat16

D_MODEL = 1024
D_RNN = 512
RG_HEADS = 8
RG_CONV = 4
RG_C = 8.0
CF_W = 256
CF_CONV = 31
S5_W = 256
S5_GROUPS = 16
S5_GROUP_CH = 16
S5_STATE = 64
S5_N = S5_GROUPS * S5_STATE
P_IN = 2 * D_RNN + 2 * CF_W + S5_W
N_GROUPS = 4
EXP_PER_GROUP = 4
N_EXPERTS = 16
EXPERT_FF = 256
EPS = 1e-6

ROUTER_LANES = 128
VMEM_LIMIT_BYTES = 60000 * 1024
ROW_BLOCK = 64
CHUNK_ROWS = 512
MOE_UNIT = 32
MOE_TILE_UNITS = CHUNK_ROWS // MOE_UNIT + N_GROUPS
MOE_TILE_SLOTS = MOE_TILE_UNITS * MOE_UNIT
MOE_MTILE_UNITS = 16
MOE_ROW_W = D_MODEL + ROUTER_LANES


def _rms(x, g):
    return x * lax.rsqrt(jnp.mean(x * x, axis=-1, keepdims=True) + EPS) * g


def _softplus(z):
    return jnp.maximum(z, 0.0) + jnp.log1p(jnp.exp(-jnp.abs(z)))


def _sigmoid(z):
    return 0.5 * jnp.tanh(0.5 * z) + 0.5


def _row_blocks(rows, rb, fn, unroll=False):
    if unroll:
        for i in range(rows // rb):
            fn(i * rb)
        return

    def body(i, carry):
        fn(pl.multiple_of(i * rb, rb))
        return carry
    lax.fori_loop(0, rows // rb, body, 0)


def _s5_prep_kernel(are_ref, aim_ref, ldt_ref, bre_ref, bim_ref,
                    abre_ref, abim_ref, bbre_ref, bbim_ref):
    a_re = are_ref[...]
    a_im = aim_ref[...]
    dt = jnp.exp(ldt_ref[...])
    mag = jnp.exp(a_re * dt)
    ab_re = mag * jnp.cos(a_im * dt)
    ab_im = mag * jnp.sin(a_im * dt)
    den = a_re * a_re + a_im * a_im
    nr = ab_re - 1.0
    ni = ab_im
    k_re = (nr * a_re + ni * a_im) / den
    k_im = (ni * a_re - nr * a_im) / den
    abre_ref[...] = ab_re
    abim_ref[...] = ab_im
    b_re = bre_ref[...]
    b_im = bim_ref[...]
    bbre_ref[...] = k_re[:, None, :] * b_re - k_im[:, None, :] * b_im
    bbim_ref[...] = k_re[:, None, :] * b_im + k_im[:, None, :] * b_re


def _s5_prep(a_re, a_im, log_dt, b_re_ghp, b_im_ghp):
    L = a_re.shape[0]
    gp = pl.BlockSpec((None, S5_GROUPS, S5_STATE), lambda l: (l, 0, 0))
    ghp = pl.BlockSpec((None, S5_GROUPS, S5_GROUP_CH, S5_STATE), lambda l: (l, 0, 0, 0))
    return pl.pallas_call(
        _s5_prep_kernel,
        grid=(L,),
        in_specs=[gp, gp, pl.BlockSpec((None, S5_GROUPS, 1), lambda l: (l, 0, 0)), ghp, ghp],
        out_specs=[gp, gp, ghp, ghp],
        out_shape=[jax.ShapeDtypeStruct((L, S5_GROUPS, S5_STATE), F32)] * 2
        + [jax.ShapeDtypeStruct((L, S5_GROUPS, S5_GROUP_CH, S5_STATE), F32)] * 2,
        name="s5_prep",
    )(a_re, a_im, log_dt[..., None], b_re_ghp, b_im_ghp)


def _mixer_kernel(bt, tc, n_chunks, x_bm,
                  x_ref, h0_ref, rb_ref, fb_ref, sre_ref, sim_ref,
                  n1_ref, win_ref, rgw_ref, rgb_ref, wg_ref, bg_ref, lam_ref,
                  cfw_ref, cfb_ref, lng_ref, lnb_ref,
                  abre_ref, abim_ref, bbre_ref, bbim_ref, cre_ref, cim_ref, d_ref,
                  wglu_ref, bglu_ref, gnrg_ref, gncf_ref, gns5_ref, wout_ref,
                  x1_ref, hl_ref, rbn_ref, fbn_ref, sren_ref, simn_ref,
                  xpad, upad, h_sc, sre_sc, sim_sc, hn_sc, proj_sc, xc_sc, gate_sc,
                  a_sc, b_sc, v_sc, y_sc, zb_sc, gl_sc, bre_sc, bim_sc, mix_sc, *maybe_xtm):
    rows = tc * bt
    rb = ROW_BLOCK
    rg_hist = (RG_CONV - 1) * bt
    cf_hist = (CF_CONV - 1) * bt
    c = pl.program_id(0)

    @pl.when(c == 0)
    def _init():
        xpad[0:rg_hist, :] = rb_ref[...]
        upad[0:cf_hist, :] = fb_ref[...]
        h_sc[...] = h0_ref[...]
        sre_sc[...] = sre_ref[...]
        sim_sc[...] = sim_ref[...]

    o_cf = 2 * D_RNN
    o_s5 = o_cf + 2 * CF_W

    if x_bm:
        (x_src,) = maybe_xtm
        for t in range(tc):
            x_src[t * bt:(t + 1) * bt, :] = x_ref[:, t, :]
    else:
        x_src = x_ref

    def norm_in(r0):
        x = x_src[r0:r0 + rb, :]
        hn_sc[r0:r0 + rb, :] = _rms(x, n1_ref[...]).astype(BF16)
    _row_blocks(rows, rb, norm_in, unroll=True)
    proj_sc[...] = jnp.dot(hn_sc[...], win_ref[...], preferred_element_type=F32)

    xpad[rg_hist:rg_hist + rows, :] = proj_sc[:, 0:D_RNN]

    def rg_conv(r0):
        acc = rgb_ref[...] + rgw_ref[0:1, :] * xpad[r0:r0 + rb, :]
        for k in range(1, RG_CONV):
            acc = acc + rgw_ref[k:k + 1, :] * xpad[r0 + k * bt:r0 + k * bt + rb, :]
        xc_sc[r0:r0 + rb, :] = acc
    _row_blocks(rows, rb, rg_conv, unroll=True)

    gate_sc[...] = jnp.dot(xc_sc[...].astype(BF16), wg_ref[...],
                           preferred_element_type=F32) + bg_ref[...]

    su_bf = proj_sc[:, o_s5:o_s5 + S5_W].astype(BF16)
    bre_sc[...] = jnp.dot(su_bf, bbre_ref[...], preferred_element_type=F32)
    bim_sc[...] = jnp.dot(su_bf, bbim_ref[...], preferred_element_type=F32)

    neg_c_sp = -RG_C * _softplus(-lam_ref[...])

    def rg_coeffs(r0):
        g = gate_sc[r0:r0 + rb, :]
        r = _sigmoid(g[:, 0:D_RNN])
        i = _sigmoid(g[:, D_RNN:2 * D_RNN])
        a = jnp.exp(neg_c_sp * r)
        a_sc[r0:r0 + rb, :] = a
        b_sc[r0:r0 + rb, :] = jnp.sqrt(1.0 - a * a) * (i * xc_sc[r0:r0 + rb, :])
    _row_blocks(rows, rb, rg_coeffs, unroll=True)

    def cf_glu(r0):
        ca = proj_sc[r0:r0 + rb, o_cf:o_cf + CF_W]
        cg = proj_sc[r0:r0 + rb, o_cf + CF_W:o_cf + 2 * CF_W]
        upad[cf_hist + r0:cf_hist + r0 + rb, :] = ca * _sigmoid(cg)
    _row_blocks(rows, rb, cf_glu, unroll=True)

    h = h_sc[...]
    for t in range(tc):
        r0 = t * bt
        h = a_sc[r0:r0 + bt, :] * h + b_sc[r0:r0 + bt, :]
        b_sc[r0:r0 + bt, :] = h
    h_sc[...] = h
    hl_ref[...] = h

    a_r = jnp.broadcast_to(abre_ref[...], (bt, S5_N))
    a_i = jnp.broadcast_to(abim_ref[...], (bt, S5_N))
    s_r = sre_sc[...]
    s_i = sim_sc[...]
    for t in range(tc):
        r0 = t * bt
        n_r = a_r * s_r - a_i * s_i + bre_sc[r0:r0 + bt, :]
        n_i = a_r * s_i + a_i * s_r + bim_sc[r0:r0 + bt, :]
        bre_sc[r0:r0 + bt, :] = n_r
        bim_sc[r0:r0 + bt, :] = n_i
        s_r, s_i = n_r, n_i
    sre_sc[...] = s_r
    sim_sc[...] = s_i
    sren_ref[...] = s_r
    simn_ref[...] = s_i

    def cf_conv(r0):
        acc = cfb_ref[...] + cfw_ref[0:1, :] * upad[r0:r0 + rb, :]
        for k in range(1, CF_CONV):
            acc = acc + cfw_ref[k:k + 1, :] * upad[r0 + k * bt:r0 + k * bt + rb, :]
        v_sc[r0:r0 + rb, :] = acc
    _row_blocks(rows, rb, cf_conv, unroll=True)

    def rg_out(r0):
        h = b_sc[r0:r0 + rb, :]
        yb = proj_sc[r0:r0 + rb, D_RNN:2 * D_RNN]
        mix_sc[r0:r0 + rb, 0:D_RNN] = _rms(h * jax.nn.gelu(yb), gnrg_ref[...]).astype(BF16)
    _row_blocks(rows, rb, rg_out, unroll=True)

    def cf_out(r0):
        v = v_sc[r0:r0 + rb, :]
        mu = jnp.mean(v, axis=-1, keepdims=True)
        cen = v - mu
        var = jnp.mean(cen * cen, axis=-1, keepdims=True)
        vn = cen * lax.rsqrt(var + EPS) * lng_ref[...] + lnb_ref[...]
        mix_sc[r0:r0 + rb, D_RNN:D_RNN + CF_W] = _rms(vn * _sigmoid(vn), gncf_ref[...]).astype(BF16)
    _row_blocks(rows, rb, cf_out, unroll=True)

    y_sc[...] = (jnp.dot(bre_sc[...].astype(BF16), cre_ref[...], preferred_element_type=F32)
                 - jnp.dot(bim_sc[...].astype(BF16), cim_ref[...], preferred_element_type=F32))

    def s5_gelu(r0):
        su = proj_sc[r0:r0 + rb, o_s5:o_s5 + S5_W]
        z = jax.nn.gelu(y_sc[r0:r0 + rb, :] + d_ref[...] * su)
        y_sc[r0:r0 + rb, :] = z
        zb_sc[r0:r0 + rb, :] = z.astype(BF16)
    _row_blocks(rows, rb, s5_gelu, unroll=True)
    gl_sc[...] = jnp.dot(zb_sc[...], wglu_ref[...], preferred_element_type=F32) + bglu_ref[...]

    def s5_out(r0):
        oc = y_sc[r0:r0 + rb, :] * _sigmoid(gl_sc[r0:r0 + rb, :])
        mix_sc[r0:r0 + rb, D_RNN + CF_W:D_MODEL] = _rms(oc, gns5_ref[...]).astype(BF16)
    _row_blocks(rows, rb, s5_out, unroll=True)

    x1_ref[...] = x_src[...] + jnp.dot(mix_sc[...], wout_ref[...], preferred_element_type=F32)

    rbn_ref[...] = xpad[rows:rows + rg_hist, :]
    fbn_ref[...] = upad[rows:rows + cf_hist, :]
    if n_chunks > 1:
        xpad[0:rg_hist, :] = xpad[rows:rows + rg_hist, :]
        upad[0:cf_hist, :] = upad[rows:rows + cf_hist, :]


def _const_spec(shape):
    nd = len(shape)
    return pl.BlockSpec(shape, lambda *_: (0,) * nd)


def _layer_spec(a, l):
    nd = a.ndim - 1
    return pl.BlockSpec((None,) + a.shape[1:], lambda *_: (l,) + (0,) * nd)


def _mixer(x, bt, tc, n_chunks, state, w, l, x_blk0=0, out_rows=None, out=None, out_blk0=0):
    x_bm = x.ndim == 3
    rows = tc * bt
    assert rows % ROW_BLOCK == 0 and bt % 8 == 0
    assert n_chunks == 1 or tc >= CF_CONV - 1
    h0, rbuf, fbuf, s_re, s_im = state
    weights = [w["norm1"], w["w_in"], w["rg_conv_w"], w["rg_conv_b"], w["rg_wg"], w["rg_bg"],
               w["rg_lambda"], w["cf_conv_w"], w["cf_conv_b"], w["cf_ln_g"], w["cf_ln_b"],
               w["s5_ab_re"], w["s5_ab_im"], w["s5_bb_re"], w["s5_bb_im"], w["s5_c_re"], w["s5_c_im"],
               w["s5_d"], w["s5_w_glu"], w["s5_b_glu"], w["gn_rg"], w["gn_cf"], w["gn_s5"], w["w_out"]]
    states = [h0, rbuf, fbuf, s_re, s_im]
    if x_bm:
        x_spec = pl.BlockSpec((bt, tc, D_MODEL), lambda c: (0, c, 0))
    else:
        x_spec = pl.BlockSpec((rows, D_MODEL), lambda c: (x_blk0 + c, 0))
    in_specs = ([x_spec] + [_const_spec(a.shape) for a in states]
                + [_const_spec(a.shape) if a.ndim == 2 else _layer_spec(a, l) for a in weights])
    row_params = [1 + len(states) + i for i, a in enumerate(weights) if a.ndim == 2]
    operands = [x, *states, *weights]
    aliases = {}
    if out is not None:
        in_specs.append(pl.BlockSpec(memory_space=pl.ANY))
        aliases = {len(operands): 0}
        operands.append(out)
        out_rows = out.shape[0]
    n_in = len(operands)
    out_shape = [jax.ShapeDtypeStruct((out_rows, D_MODEL), F32)] + [
        jax.ShapeDtypeStruct(a.shape, F32) for a in states]
    out_specs = ([pl.BlockSpec((rows, D_MODEL), lambda c: (out_blk0 + c, 0))]
                 + [_const_spec(a.shape) for a in states])

    def body(*refs):
        refs = list(refs)
        for i in row_params:
            refs[i] = refs[i].at[pl.ds(l, 1)]
        keep = refs[:n_in - len(aliases)] + refs[n_in:]
        _mixer_kernel(bt, tc, n_chunks, x_bm, *keep)
    scratch = [
        pltpu.VMEM(((tc + RG_CONV - 1) * bt, D_RNN), F32),
        pltpu.VMEM(((tc + CF_CONV - 1) * bt, CF_W), F32),
        pltpu.VMEM((bt, D_RNN), F32),
        pltpu.VMEM((bt, S5_N), F32),
        pltpu.VMEM((bt, S5_N), F32),
        pltpu.VMEM((rows, D_MODEL), BF16),
        pltpu.VMEM((rows, P_IN), F32),
        pltpu.VMEM((rows, D_RNN), F32),
        pltpu.VMEM((rows, 2 * D_RNN), F32),
        pltpu.VMEM((rows, D_RNN), F32),
        pltpu.VMEM((rows, D_RNN), F32),
        pltpu.VMEM((rows, CF_W), F32),
        pltpu.VMEM((rows, S5_W), F32),
        pltpu.VMEM((rows, S5_W), BF16),
        pltpu.VMEM((rows, S5_W), F32),
        pltpu.VMEM((rows, S5_N), F32),
        pltpu.VMEM((rows, S5_N), F32),
        pltpu.VMEM((rows, D_MODEL), BF16),
    ]
    if x_bm:
        scratch.append(pltpu.VMEM((rows, D_MODEL), F32))
    return pl.pallas_call(
        body,
        grid=(n_chunks,),
        in_specs=in_specs,
        out_specs=out_specs,
        out_shape=out_shape,
        scratch_shapes=scratch,
        input_output_aliases=aliases,
        compiler_params=pltpu.CompilerParams(dimension_semantics=("arbitrary",),
                                             vmem_limit_bytes=VMEM_LIMIT_BYTES),
        name="mixer",
    )(*operands)


def _route(logits):
    shape = logits.shape
    lane = lax.broadcasted_iota(jnp.int32, shape, 1).astype(F32)
    neg = jnp.float32(-jnp.inf)
    is_grp = lane < N_GROUPS
    gl = jnp.where(is_grp, logits, neg)
    gmax = jnp.max(gl, axis=-1, keepdims=True)
    gidx = jnp.min(jnp.where(gl == gmax, lane, ROUTER_LANES), axis=-1, keepdims=True)
    p_sel = 1.0 / jnp.sum(jnp.where(is_grp, jnp.exp(logits - gmax), 0.0), axis=-1, keepdims=True)
    lo = N_GROUPS + EXP_PER_GROUP * gidx
    in_grp = jnp.logical_and(lane >= lo, lane < lo + EXP_PER_GROUP)
    e1 = jnp.where(in_grp, logits, neg)
    v1 = jnp.max(e1, axis=-1, keepdims=True)
    i1 = jnp.min(jnp.where(e1 == v1, lane, ROUTER_LANES), axis=-1, keepdims=True)
    e2 = jnp.where(lane == i1, neg, e1)
    v2 = jnp.max(e2, axis=-1, keepdims=True)
    i2 = jnp.min(jnp.where(e2 == v2, lane, ROUTER_LANES), axis=-1, keepdims=True)
    ex = jnp.exp(v2 - v1)
    w1 = 1.0 / (1.0 + ex)
    w2 = ex * w1
    gate = jnp.where(lane == i1, p_sel * w1, jnp.where(lane == i2, p_sel * w2, 0.0))
    return gate, gidx


def _moe_route_kernel(x_ref, n2_ref, wr_ref, br_ref, ts_ref, slot_ref, cnt_ref, tri_sc):
    rows = x_ref.shape[0]
    x = x_ref[...]
    t = _rms(x, n2_ref[...])
    t_hi = t.astype(BF16)
    t_lo = (t - t_hi.astype(F32)).astype(BF16)
    wr = wr_ref[...]
    wr_hi = wr.astype(BF16)
    wr_lo = (wr - wr_hi.astype(F32)).astype(BF16)
    logits = (jnp.dot(t_hi, wr_hi, preferred_element_type=F32)
              + jnp.dot(t_lo, wr_hi, preferred_element_type=F32)
              + jnp.dot(t_hi, wr_lo, preferred_element_type=F32)) + br_ref[...]
    gate, gidx = _route(logits)

    lane = lax.broadcasted_iota(jnp.int32, (rows, ROUTER_LANES), 1).astype(F32)
    onehot = jnp.where(lane == gidx, 1.0, 0.0)

    @pl.when(pl.program_id(0) == 0)
    def _build_tri():
        r_i = lax.broadcasted_iota(jnp.int32, (rows, rows), 0)
        c_i = lax.broadcasted_iota(jnp.int32, (rows, rows), 1)
        tri_sc[...] = jnp.where(c_i < r_i, 1.0, 0.0).astype(BF16)
    before = jnp.dot(tri_sc[...], onehot.astype(BF16), preferred_element_type=F32)
    rank = jnp.sum(before * onehot, axis=-1, keepdims=True)
    counts = jnp.sum(onehot, axis=0, keepdims=True)

    units = jnp.ceil(counts * (1.0 / MOE_UNIT))
    first = MOE_UNIT * (pltpu.roll(units, 1, 1) + pltpu.roll(units, 2, 1) + pltpu.roll(units, 3, 1))
    slot = jnp.sum(onehot * first, axis=-1, keepdims=True) + rank
    slot_b = jnp.broadcast_to(slot, (rows, ROUTER_LANES))
    slot_t = jnp.transpose(slot_b)[0:1, :]
    s_sub = lax.broadcasted_iota(jnp.int32, (MOE_TILE_SLOTS, rows), 0).astype(F32)
    perm = jnp.where(s_sub == slot_t, 1.0, 0.0).astype(BF16)

    ts_ref[:, 0:D_MODEL] = jnp.dot(perm, t_hi, preferred_element_type=F32).astype(BF16)
    gate4 = jnp.zeros_like(gate)
    for gi in range(N_GROUPS):
        shift = ROUTER_LANES - (N_GROUPS + EXP_PER_GROUP * gi)
        gate4 = gate4 + jnp.where(gidx == gi, pltpu.roll(gate, shift, 1), 0.0)
    g_hi = gate4.astype(BF16).astype(F32)
    packed = (g_hi + pltpu.roll(gate4 - g_hi, ROUTER_LANES // 2, 1)).astype(BF16)
    ts_ref[:, D_MODEL:MOE_ROW_W] = jnp.dot(perm, packed, preferred_element_type=F32).astype(BF16)
    slot_ref[...] = slot_b
    cnt_ref[...] = jnp.broadcast_to(counts, cnt_ref.shape)


def _moe_expert_kernel(src_ref, grp_ref, nt_ref, *refs):
    units = refs[0:MOE_MTILE_UNITS]
    wg_ref, wu_ref, wd_ref, out_ref, wg_sc, wu_sc, wd_sc = refs[MOE_MTILE_UNITS:]
    del src_ref
    j = pl.program_id(0)
    active = j < nt_ref[0]

    @pl.when(jnp.logical_and(active, jnp.logical_or(j == 0, grp_ref[j] != grp_ref[jnp.maximum(j - 1, 0)])))
    def _cast_weights():
        for e in range(EXP_PER_GROUP):
            wg_sc[e] = wg_ref[e].astype(BF16)
            wu_sc[e] = wu_ref[e].astype(BF16)
            wd_sc[e] = wd_ref[e].astype(BF16)

    @pl.when(active)
    def _compute():
        tsg = jnp.concatenate([u[...] for u in units], axis=0)
        ts = tsg[:, 0:D_MODEL]
        gp = tsg[:, D_MODEL:MOE_ROW_W].astype(F32)
        gs = gp + pltpu.roll(gp, ROUTER_LANES // 2, 1)
        acc = None
        for e in range(EXP_PER_GROUP):
            hg = jnp.dot(ts, wg_sc[e], preferred_element_type=F32)
            hu = jnp.dot(ts, wu_sc[e], preferred_element_type=F32)
            act = (hg * _sigmoid(hg) * hu * gs[:, e:e + 1]).astype(BF16)
            o = jnp.dot(act, wd_sc[e], preferred_element_type=F32)
            acc = o if acc is None else acc + o
        out_ref[...] = acc.astype(BF16)

    @pl.when(jnp.logical_not(active))
    def _idle():
        out_ref[...] = jnp.zeros_like(out_ref)


def _moe_combine_kernel(final, bt_out, n_bm_tiles, pos_ref, x_ref, slot_ref, *refs):
    del pos_ref
    units = refs[0:MOE_TILE_UNITS]
    fn_ref = refs[MOE_TILE_UNITS]
    outs = refs[MOE_TILE_UNITS + 1:]
    rows = x_ref.shape[0]
    os_local = jnp.concatenate([u[...] for u in units], axis=0)
    s_lane = lax.broadcasted_iota(jnp.int32, (rows, MOE_TILE_SLOTS), 1).astype(F32)
    perm_t = jnp.where(s_lane == slot_ref[:, 0:1], 1.0, 0.0).astype(BF16)
    y = x_ref[...] + jnp.dot(perm_t, os_local, preferred_element_type=F32)
    if final:
        y = _rms(y, fn_ref[...])
    if bt_out:
        o_bm, o_rest = outs
        i = pl.program_id(0)

        @pl.when(i < n_bm_tiles)
        def _batch_major():
            for t in range(rows // bt_out):
                o_bm[:, t, :] = y[t * bt_out:(t + 1) * bt_out, :]

        @pl.when(i >= n_bm_tiles)
        def _rest():
            o_rest[...] = y
    else:
        (o_ref,) = outs
        o_ref[...] = y


def _moe_plan(counts, n_tiles, n_mtiles):
    i32 = jnp.int32
    g = jnp.arange(N_GROUPS, dtype=i32)
    c = jnp.arange(n_tiles, dtype=i32)
    earlier_g = (g[:, None] < g[None, :]).astype(i32)
    earlier_c = (c[:, None] < c[None, :]).astype(i32)
    units = (counts + (MOE_UNIT - 1)) // MOE_UNIT
    loc_off = jnp.sum(units[:, :, None] * earlier_g[None], axis=1)
    before_c = jnp.sum(units[:, None, :] * earlier_c[:, :, None], axis=0)
    grp_tiles = (jnp.sum(units, axis=0) + (MOE_MTILE_UNITS - 1)) // MOE_MTILE_UNITS
    tile_start = jnp.sum(grp_tiles[:, None] * earlier_g, axis=0)
    grp_end = tile_start + grp_tiles
    u = jnp.arange(MOE_TILE_UNITS, dtype=i32)
    g_of = jnp.sum((u[None, :, None] >= (loc_off + units)[:, None, :]).astype(i32), axis=-1)
    valid = g_of < N_GROUPS
    pick = (g_of[:, :, None] == g[None, None, :]).astype(i32)
    base = MOE_MTILE_UNITS * tile_start[None, :] + before_c - loc_off
    pos = jnp.sum(pick * base[:, None, :], axis=-1) + u[None, :]
    pos = jnp.where(valid, pos, 0).astype(i32)
    local_id = c[:, None] * MOE_TILE_UNITS + u[None, :]
    n_pos = n_mtiles * MOE_MTILE_UNITS
    p = jnp.arange(n_pos, dtype=i32)
    hit = jnp.logical_and(valid.reshape(-1)[None, :], pos.reshape(-1)[None, :] == p[:, None])
    src = jnp.sum(jnp.where(hit, local_id.reshape(-1)[None, :], 0), axis=-1).astype(i32)
    j = jnp.arange(n_mtiles, dtype=i32)
    grp = jnp.minimum(jnp.sum((j[:, None] >= grp_end[None, :]).astype(i32), axis=-1), N_GROUPS - 1)
    n_used = jnp.sum(grp_tiles, keepdims=True)
    return src, grp.astype(i32), n_used.astype(i32), pos.reshape(-1)


def _moe(x1, w, l, final, bt_out=0, n_bm_tiles=0):
    n_rows = x1.shape[0]
    rows = CHUNK_ROWS
    assert n_rows % rows == 0
    n_tiles = n_rows // rows
    n_mtiles = -(-(n_tiles * MOE_TILE_UNITS) // MOE_MTILE_UNITS) + N_GROUPS
    mrows = MOE_MTILE_UNITS * MOE_UNIT
    cparams = pltpu.CompilerParams(dimension_semantics=("arbitrary",), vmem_limit_bytes=VMEM_LIMIT_BYTES)

    row_spec = pl.BlockSpec((rows, D_MODEL), lambda i: (i, 0))
    def route_body(x_ref, n2_ref, wr_ref, br_ref, *rest):
        _moe_route_kernel(x_ref, n2_ref.at[pl.ds(l, 1)], wr_ref, br_ref.at[pl.ds(l, 1)], *rest)
    ts, slot, cnt = pl.pallas_call(
        route_body,
        grid=(n_tiles,),
        in_specs=[row_spec, _const_spec(w["norm2"].shape), _layer_spec(w["moe_w_r"], l),
                  _const_spec(w["moe_b_r"].shape)],
        out_specs=[pl.BlockSpec((MOE_TILE_SLOTS, MOE_ROW_W), lambda i: (i, 0)),
                   pl.BlockSpec((rows, ROUTER_LANES), lambda i: (i, 0)),
                   pl.BlockSpec((8, ROUTER_LANES), lambda i: (i, 0))],
        out_shape=[jax.ShapeDtypeStruct((n_tiles * MOE_TILE_SLOTS, MOE_ROW_W), BF16),
                   jax.ShapeDtypeStruct((n_rows, ROUTER_LANES), F32),
                   jax.ShapeDtypeStruct((n_tiles * 8, ROUTER_LANES), F32)],
        scratch_shapes=[pltpu.VMEM((rows, rows), BF16)],
        compiler_params=cparams,
        name="moe_route",
    )(x1, w["norm2"], w["moe_w_r"], w["moe_b_r"])

    counts = cnt.reshape(n_tiles, 8, ROUTER_LANES)[:, 0, :N_GROUPS].astype(jnp.int32)
    src, grp, n_used, pos = _moe_plan(counts, n_tiles, n_mtiles)

    def unit_in(k):
        return pl.BlockSpec((MOE_UNIT, MOE_ROW_W), lambda j, src, grp, nt: (src[j * MOE_MTILE_UNITS + k], 0))

    def grp_w(shape):
        return pl.BlockSpec((None, EXP_PER_GROUP) + shape, lambda j, src, grp, nt: (l, grp[j], 0, 0))
    os_sorted = pl.pallas_call(
        _moe_expert_kernel,
        grid_spec=pltpu.PrefetchScalarGridSpec(
            num_scalar_prefetch=3,
            grid=(n_mtiles,),
            in_specs=[unit_in(k) for k in range(MOE_MTILE_UNITS)]
            + [grp_w((D_MODEL, EXPERT_FF)), grp_w((D_MODEL, EXPERT_FF)), grp_w((EXPERT_FF, D_MODEL))],
            out_specs=pl.BlockSpec((mrows, D_MODEL), lambda j, src, grp, nt: (j, 0)),
            scratch_shapes=[pltpu.VMEM((EXP_PER_GROUP, D_MODEL, EXPERT_FF), BF16),
                            pltpu.VMEM((EXP_PER_GROUP, D_MODEL, EXPERT_FF), BF16),
                            pltpu.VMEM((EXP_PER_GROUP, EXPERT_FF, D_MODEL), BF16)],
        ),
        out_shape=jax.ShapeDtypeStruct((n_mtiles * mrows, D_MODEL), BF16),
        compiler_params=cparams,
        name="moe_experts",
    )(src, grp, n_used, *([ts] * MOE_MTILE_UNITS), w["moe_w_gate"], w["moe_w_up"], w["moe_w_down"])

    def unit_out(u):
        return pl.BlockSpec((MOE_UNIT, D_MODEL), lambda i, pos: (pos[i * MOE_TILE_UNITS + u], 0))
    if bt_out:
        tc = rows // bt_out
        last = n_bm_tiles - 1
        out_spec = [pl.BlockSpec((bt_out, tc, D_MODEL), lambda i, pos: (0, jnp.minimum(i, last), 0)),
                    pl.BlockSpec((rows, D_MODEL), lambda i, pos: (jnp.maximum(i - n_bm_tiles, 0), 0))]
        out_shape = [jax.ShapeDtypeStruct((bt_out, n_bm_tiles * tc, D_MODEL), F32),
                     jax.ShapeDtypeStruct(((n_tiles - n_bm_tiles) * rows, D_MODEL), F32)]
    else:
        out_spec = pl.BlockSpec((rows, D_MODEL), lambda i, pos: (i, 0))
        out_shape = jax.ShapeDtypeStruct((n_rows, D_MODEL), F32)
    return pl.pallas_call(
        functools.partial(_moe_combine_kernel, final, bt_out, n_bm_tiles),
        grid_spec=pltpu.PrefetchScalarGridSpec(
            num_scalar_prefetch=1,
            grid=(n_tiles,),
            in_specs=[pl.BlockSpec((rows, D_MODEL), lambda i, pos: (i, 0)),
                      pl.BlockSpec((rows, ROUTER_LANES), lambda i, pos: (i, 0))]
            + [unit_out(u) for u in range(MOE_TILE_UNITS)]
            + [pl.BlockSpec(w["final_norm"].shape, lambda i, pos: (0, 0))],
            out_specs=out_spec,
        ),
        out_shape=out_shape,
        compiler_params=cparams,
        name="moe_combine",
    )(pos, x1, slot, *([os_sorted] * MOE_TILE_UNITS), w["final_norm"])


def _block_diag(blocks):
    L, n, r, c = blocks.shape
    eye = jnp.eye(n, dtype=blocks.dtype)
    return (blocks[:, :, :, None, :] * eye[None, :, None, :, None]).reshape(L, n * r, n * c)


def _to_tm(a):
    a = jnp.swapaxes(a, 0, 1)
    return a.reshape((a.shape[0] * a.shape[1],) + a.shape[2:])


def _from_tm(a, b):
    return jnp.swapaxes(a.reshape((a.shape[0] // b, b) + a.shape[1:]), 0, 1)


def kernel(x_prompt, x_sample, state_rglru_h, state_rglru_conv, state_conf_conv, state_s5_re, state_s5_im, norm1, w_in, rg_conv_w, rg_conv_b, rg_wa, rg_ba, rg_wx, rg_bx, rg_lambda, cf_conv_w, cf_conv_b, cf_ln_g, cf_ln_b, s5_a_re, s5_a_im, s5_log_dt, s5_b_re, s5_b_im, s5_c_re, s5_c_im, s5_d, s5_w_glu, s5_b_glu, gn_rg, gn_cf, gn_s5, w_out, norm2, moe_w_grp, moe_b_grp, moe_w_exp, moe_b_exp, moe_w_gate, moe_w_up, moe_w_down, final_norm):
    L = w_in.shape[0]
    B, T, _ = x_prompt.shape
    DB, DT, _ = x_sample.shape

    ab_re, ab_im, bb_re, bb_im = _s5_prep(s5_a_re, s5_a_im, s5_log_dt,
                                          jnp.swapaxes(s5_b_re, 2, 3), jnp.swapaxes(s5_b_im, 2, 3))

    def row(a):
        return a.reshape(L, -1)

    W = dict(
        norm1=row(norm1), w_in=w_in.astype(BF16),
        rg_conv_w=rg_conv_w, rg_conv_b=row(rg_conv_b),
        rg_wg=jnp.concatenate([_block_diag(rg_wa), _block_diag(rg_wx)], axis=-1).astype(BF16),
        rg_bg=jnp.concatenate([row(rg_ba), row(rg_bx)], axis=-1),
        rg_lambda=row(rg_lambda),
        cf_conv_w=cf_conv_w, cf_conv_b=row(cf_conv_b), cf_ln_g=row(cf_ln_g), cf_ln_b=row(cf_ln_b),
        s5_ab_re=row(ab_re), s5_ab_im=row(ab_im),
        s5_bb_re=_block_diag(bb_re).astype(BF16), s5_bb_im=_block_diag(bb_im).astype(BF16),
        s5_c_re=_block_diag(jnp.swapaxes(s5_c_re, 2, 3)).astype(BF16),
        s5_c_im=_block_diag(jnp.swapaxes(s5_c_im, 2, 3)).astype(BF16),
        s5_d=row(s5_d), s5_w_glu=s5_w_glu.astype(BF16), s5_b_glu=row(s5_b_glu),
        gn_rg=row(gn_rg), gn_cf=row(gn_cf), gn_s5=row(gn_s5), w_out=w_out.astype(BF16),
        norm2=row(norm2),
        moe_w_r=jnp.concatenate(
            [moe_w_grp, moe_w_exp.reshape(L, D_MODEL, N_EXPERTS),
             jnp.zeros((L, D_MODEL, ROUTER_LANES - N_GROUPS - N_EXPERTS), F32)], axis=-1),
        moe_b_r=jnp.concatenate(
            [moe_b_grp, moe_b_exp.reshape(L, N_EXPERTS),
             jnp.zeros((L, ROUTER_LANES - N_GROUPS - N_EXPERTS), F32)], axis=-1),
        moe_w_gate=moe_w_gate, moe_w_up=moe_w_up, moe_w_down=moe_w_down,
    )

    tc_p = CHUNK_ROWS // B
    n_p = (B * T) // CHUNK_ROWS
    n_s = (DB * DT) // CHUNK_ROWS
    assert n_p * CHUNK_ROWS == B * T and n_s == 1 and DB * DT == CHUNK_ROWS
    all_rows = (n_p + n_s) * CHUNK_ROWS
    x_p, x_s, s_blk0 = x_prompt, _to_tm(x_sample), 0
    p_states, s_states = [], []
    W["final_norm"] = final_norm.reshape(1, D_MODEL)
    st_p = (jnp.zeros((B, D_RNN), F32), jnp.zeros(((RG_CONV - 1) * B, D_RNN), F32),
            jnp.zeros(((CF_CONV - 1) * B, CF_W), F32), jnp.zeros((B, S5_N), F32),
            jnp.zeros((B, S5_N), F32))
    for l in range(L):
        st_s = (state_rglru_h[l], _to_tm(state_rglru_conv[l]), _to_tm(state_conf_conv[l]),
                state_s5_re[l].reshape(DB, S5_N), state_s5_im[l].reshape(DB, S5_N))
        x1, *np_l = _mixer(x_p, B, tc_p, n_p, st_p, W, l, out_rows=all_rows)
        x1, *ns_l = _mixer(x_s, DB, DT, n_s, st_s, W, l, x_blk0=s_blk0, out=x1, out_blk0=n_p)
        p_states.append(np_l)
        s_states.append(ns_l)
        if l < L - 1:
            x_p = x_s = _moe(x1, W, l, False)
            s_blk0 = n_p
        else:
            y_prompt, y_s = _moe(x1, W, l, True, bt_out=B, n_bm_tiles=n_p)

    def assemble(states, b):
        h, rc, cf, sre, sim = [jnp.stack([s[j] for s in states]) for j in range(5)]
        rc = jnp.swapaxes(rc.reshape(L, RG_CONV - 1, b, D_RNN), 1, 2)
        cf = jnp.swapaxes(cf.reshape(L, CF_CONV - 1, b, CF_W), 1, 2)
        return (h, rc, cf, sre.reshape(L, b, S5_GROUPS, S5_STATE),
                sim.reshape(L, b, S5_GROUPS, S5_STATE))

    y_sample = _from_tm(y_s, DB)
    return (y_prompt, y_sample) + assemble(p_states, B) + assemble(s_states, DB)
```

```python
import functools

import jax
import jax.numpy as jnp
from jax import lax
from jax.experimental import pallas as pl
from jax.experimental.pallas import tpu as pltpu

F32 = jnp.float32
BF16 = jnp.bfloat16

D_MODEL = 1024
D_RNN = 512
RG_HEADS = 8
RG_CONV = 4
RG_C = 8.0
CF_W = 256
CF_CONV = 31
S5_W = 256
S5_GROUPS = 16
S5_GROUP_CH = 16
S5_STATE = 64
S5_N = S5_GROUPS * S5_STATE
P_IN = 2 * D_RNN + 2 * CF_W + S5_W
N_GROUPS = 4
EXP_PER_GROUP = 4
N_EXPERTS = 16
EXPERT_FF = 256
EPS = 1e-6

ROUTER_LANES = 128
ROUTER_ROWS = 32
VMEM_LIMIT_BYTES = 60000 * 1024
ROW_BLOCK = 64
CHUNK_ROWS = 512
MOE_UNIT = 32
MOE_TILE_UNITS = CHUNK_ROWS // MOE_UNIT + N_GROUPS
MOE_TILE_SLOTS = MOE_TILE_UNITS * MOE_UNIT
MOE_MTILE_UNITS = 16
MOE_ROW_W = D_MODEL + ROUTER_LANES


def _rms(x, g):
    return x * lax.rsqrt(jnp.mean(x * x, axis=-1, keepdims=True) + EPS) * g


def _softplus(z):
    return jnp.maximum(z, 0.0) + jnp.log1p(jnp.exp(-jnp.abs(z)))


def _sigmoid(z):
    return 0.5 * jnp.tanh(0.5 * z) + 0.5


def _row_blocks(rows, rb, fn, unroll=False):
    if unroll:
        for i in range(rows // rb):
            fn(i * rb)
        return

    def body(i, carry):
        fn(pl.multiple_of(i * rb, rb))
        return carry
    lax.fori_loop(0, rows // rb, body, 0)


def _s5_prep_kernel(are_ref, aim_ref, ldt_ref, bre_ref, bim_ref,
                    abre_ref, abim_ref, bbre_ref, bbim_ref):
    a_re = are_ref[...]
    a_im = aim_ref[...]
    dt = jnp.exp(ldt_ref[...])
    mag = jnp.exp(a_re * dt)
    ab_re = mag * jnp.cos(a_im * dt)
    ab_im = mag * jnp.sin(a_im * dt)
    den = a_re * a_re + a_im * a_im
    nr = ab_re - 1.0
    ni = ab_im
    k_re = (nr * a_re + ni * a_im) / den
    k_im = (ni * a_re - nr * a_im) / den
    abre_ref[...] = ab_re
    abim_ref[...] = ab_im
    b_re = bre_ref[...]
    b_im = bim_ref[...]
    bbre_ref[...] = k_re[:, None, :] * b_re - k_im[:, None, :] * b_im
    bbim_ref[...] = k_re[:, None, :] * b_im + k_im[:, None, :] * b_re


def _s5_prep(a_re, a_im, log_dt, b_re_ghp, b_im_ghp):
    L = a_re.shape[0]
    gp = pl.BlockSpec((None, S5_GROUPS, S5_STATE), lambda l: (l, 0, 0))
    ghp = pl.BlockSpec((None, S5_GROUPS, S5_GROUP_CH, S5_STATE), lambda l: (l, 0, 0, 0))
    return pl.pallas_call(
        _s5_prep_kernel,
        grid=(L,),
        in_specs=[gp, gp, pl.BlockSpec((None, S5_GROUPS, 1), lambda l: (l, 0, 0)), ghp, ghp],
        out_specs=[gp, gp, ghp, ghp],
        out_shape=[jax.ShapeDtypeStruct((L, S5_GROUPS, S5_STATE), F32)] * 2
        + [jax.ShapeDtypeStruct((L, S5_GROUPS, S5_GROUP_CH, S5_STATE), F32)] * 2,
        name="s5_prep",
    )(a_re, a_im, log_dt[..., None], b_re_ghp, b_im_ghp)


def _mixer_kernel(bt, tc, n_chunks, x_bm,
                  x_ref, h0_ref, rb_ref, fb_ref, sre_ref, sim_ref,
                  n1_ref, win_ref, rgw_ref, rgb_ref, wg_ref, bg_ref, lam_ref,
                  cfw_ref, cfb_ref, lng_ref, lnb_ref,
                  abre_ref, abim_ref, bbre_ref, bbim_ref, cre_ref, cim_ref, d_ref,
                  wglu_ref, bglu_ref, gnrg_ref, gncf_ref, gns5_ref, wout_ref,
                  x1_ref, hl_ref, rbn_ref, fbn_ref, sren_ref, simn_ref,
                  xpad, upad, h_sc, sre_sc, sim_sc, hn_sc, proj_sc, xc_sc, gate_sc,
                  a_sc, b_sc, v_sc, y_sc, zb_sc, gl_sc, bre_sc, bim_sc, mix_sc, *maybe_xtm):
    rows = tc * bt
    rb = ROW_BLOCK
    rg_hist = (RG_CONV - 1) * bt
    cf_hist = (CF_CONV - 1) * bt
    c = pl.program_id(0)

    @pl.when(c == 0)
    def _init():
        xpad[0:rg_hist, :] = rb_ref[...]
        upad[0:cf_hist, :] = fb_ref[...]
        h_sc[...] = h0_ref[...]
        sre_sc[...] = sre_ref[...]
        sim_sc[...] = sim_ref[...]

    o_cf = 2 * D_RNN
    o_s5 = o_cf + 2 * CF_W

    if x_bm:
        (x_src,) = maybe_xtm
        for t in range(tc):
            x_src[t * bt:(t + 1) * bt, :] = x_ref[:, t, :]
    else:
        x_src = x_ref

    def norm_in(r0):
        x = x_src[r0:r0 + rb, :]
        hn_sc[r0:r0 + rb, :] = _rms(x, n1_ref[...]).astype(BF16)
    _row_blocks(rows, rb, norm_in, unroll=True)
    proj_sc[...] = jnp.dot(hn_sc[...], win_ref[...], preferred_element_type=F32)

    xpad[rg_hist:rg_hist + rows, :] = proj_sc[:, 0:D_RNN]

    def rg_conv(r0):
        acc = rgb_ref[...] + rgw_ref[0:1, :] * xpad[r0:r0 + rb, :]
        for k in range(1, RG_CONV):
            acc = acc + rgw_ref[k:k + 1, :] * xpad[r0 + k * bt:r0 + k * bt + rb, :]
        xc_sc[r0:r0 + rb, :] = acc
    _row_blocks(rows, rb, rg_conv, unroll=True)

    gate_sc[...] = jnp.dot(xc_sc[...].astype(BF16), wg_ref[...],
                           preferred_element_type=F32) + bg_ref[...]

    su_bf = proj_sc[:, o_s5:o_s5 + S5_W].astype(BF16)
    bre_sc[...] = jnp.dot(su_bf, bbre_ref[...], preferred_element_type=F32)
    bim_sc[...] = jnp.dot(su_bf, bbim_ref[...], preferred_element_type=F32)

    neg_c_sp = -RG_C * _softplus(-lam_ref[...])

    def rg_coeffs(r0):
        g = gate_sc[r0:r0 + rb, :]
        r = _sigmoid(g[:, 0:D_RNN])
        i = _sigmoid(g[:, D_RNN:2 * D_RNN])
        a = jnp.exp(neg_c_sp * r)
        a_sc[r0:r0 + rb, :] = a
        b_sc[r0:r0 + rb, :] = jnp.sqrt(1.0 - a * a) * (i * xc_sc[r0:r0 + rb, :])
    _row_blocks(rows, rb, rg_coeffs, unroll=True)

    def cf_glu(r0):
        ca = proj_sc[r0:r0 + rb, o_cf:o_cf + CF_W]
        cg = proj_sc[r0:r0 + rb, o_cf + CF_W:o_cf + 2 * CF_W]
        upad[cf_hist + r0:cf_hist + r0 + rb, :] = ca * _sigmoid(cg)
    _row_blocks(rows, rb, cf_glu, unroll=True)

    h = h_sc[...]
    for t in range(tc):
        r0 = t * bt
        h = a_sc[r0:r0 + bt, :] * h + b_sc[r0:r0 + bt, :]
        b_sc[r0:r0 + bt, :] = h
    h_sc[...] = h
    hl_ref[...] = h

    a_r = jnp.broadcast_to(abre_ref[...], (bt, S5_N))
    a_i = jnp.broadcast_to(abim_ref[...], (bt, S5_N))
    s_r = sre_sc[...]
    s_i = sim_sc[...]
    for t in range(tc):
        r0 = t * bt
        n_r = a_r * s_r - a_i * s_i + bre_sc[r0:r0 + bt, :]
        n_i = a_r * s_i + a_i * s_r + bim_sc[r0:r0 + bt, :]
        bre_sc[r0:r0 + bt, :] = n_r
        bim_sc[r0:r0 + bt, :] = n_i
        s_r, s_i = n_r, n_i
    sre_sc[...] = s_r
    sim_sc[...] = s_i
    sren_ref[...] = s_r
    simn_ref[...] = s_i

    def cf_conv(r0):
        acc = cfb_ref[...] + cfw_ref[0:1, :] * upad[r0:r0 + rb, :]
        for k in range(1, CF_CONV):
            acc = acc + cfw_ref[k:k + 1, :] * upad[r0 + k * bt:r0 + k * bt + rb, :]
        v_sc[r0:r0 + rb, :] = acc
    _row_blocks(rows, rb, cf_conv, unroll=True)

    def rg_out(r0):
        h = b_sc[r0:r0 + rb, :]
        yb = proj_sc[r0:r0 + rb, D_RNN:2 * D_RNN]
        mix_sc[r0:r0 + rb, 0:D_RNN] = _rms(h * jax.nn.gelu(yb), gnrg_ref[...]).astype(BF16)
    _row_blocks(rows, rb, rg_out, unroll=True)

    def cf_out(r0):
        v = v_sc[r0:r0 + rb, :]
        mu = jnp.mean(v, axis=-1, keepdims=True)
        cen = v - mu
        var = jnp.mean(cen * cen, axis=-1, keepdims=True)
        vn = cen * lax.rsqrt(var + EPS) * lng_ref[...] + lnb_ref[...]
        mix_sc[r0:r0 + rb, D_RNN:D_RNN + CF_W] = _rms(vn * _sigmoid(vn), gncf_ref[...]).astype(BF16)
    _row_blocks(rows, rb, cf_out, unroll=True)

    y_sc[...] = (jnp.dot(bre_sc[...].astype(BF16), cre_ref[...], preferred_element_type=F32)
                 - jnp.dot(bim_sc[...].astype(BF16), cim_ref[...], preferred_element_type=F32))

    def s5_gelu(r0):
        su = proj_sc[r0:r0 + rb, o_s5:o_s5 + S5_W]
        z = jax.nn.gelu(y_sc[r0:r0 + rb, :] + d_ref[...] * su)
        y_sc[r0:r0 + rb, :] = z
        zb_sc[r0:r0 + rb, :] = z.astype(BF16)
    _row_blocks(rows, rb, s5_gelu, unroll=True)
    gl_sc[...] = jnp.dot(zb_sc[...], wglu_ref[...], preferred_element_type=F32) + bglu_ref[...]

    def s5_out(r0):
        oc = y_sc[r0:r0 + rb, :] * _sigmoid(gl_sc[r0:r0 + rb, :])
        mix_sc[r0:r0 + rb, D_RNN + CF_W:D_MODEL] = _rms(oc, gns5_ref[...]).astype(BF16)
    _row_blocks(rows, rb, s5_out, unroll=True)

    x1_ref[...] = x_src[...] + jnp.dot(mix_sc[...], wout_ref[...], preferred_element_type=F32)

    rbn_ref[...] = xpad[rows:rows + rg_hist, :]
    fbn_ref[...] = upad[rows:rows + cf_hist, :]
    if n_chunks > 1:
        xpad[0:rg_hist, :] = xpad[rows:rows + rg_hist, :]
        upad[0:cf_hist, :] = upad[rows:rows + cf_hist, :]


def _const_spec(shape):
    nd = len(shape)
    return pl.BlockSpec(shape, lambda *_: (0,) * nd)


def _layer_spec(a, l):
    nd = a.ndim - 1
    return pl.BlockSpec((None,) + a.shape[1:], lambda *_: (l,) + (0,) * nd)


def _mixer(x, bt, tc, n_chunks, state, w, l, x_blk0=0, out_rows=None, out=None, out_blk0=0):
    x_bm = x.ndim == 3
    rows = tc * bt
    assert rows % ROW_BLOCK == 0 and bt % 8 == 0
    assert n_chunks == 1 or tc >= CF_CONV - 1
    h0, rbuf, fbuf, s_re, s_im = state
    weights = [w["norm1"], w["w_in"], w["rg_conv_w"], w["rg_conv_b"], w["rg_wg"], w["rg_bg"],
               w["rg_lambda"], w["cf_conv_w"], w["cf_conv_b"], w["cf_ln_g"], w["cf_ln_b"],
               w["s5_ab_re"], w["s5_ab_im"], w["s5_bb_re"], w["s5_bb_im"], w["s5_c_re"], w["s5_c_im"],
               w["s5_d"], w["s5_w_glu"], w["s5_b_glu"], w["gn_rg"], w["gn_cf"], w["gn_s5"], w["w_out"]]
    states = [h0, rbuf, fbuf, s_re, s_im]
    last = n_chunks - 1
    if x_bm:
        x_spec = pl.BlockSpec((bt, tc, D_MODEL), lambda c: (0, jnp.minimum(c, last), 0))
    else:
        x_spec = pl.BlockSpec((rows, D_MODEL), lambda c: (x_blk0 + jnp.minimum(c, last), 0))
    in_specs = ([x_spec] + [_const_spec(a.shape) for a in states]
                + [_const_spec(a.shape) if a.ndim == 2 else _layer_spec(a, l) for a in weights])
    row_params = [1 + len(states) + i for i, a in enumerate(weights) if a.ndim == 2]
    operands = [x, *states, *weights]
    aliases = {}
    if out is not None:
        in_specs.append(pl.BlockSpec(memory_space=pl.ANY))
        aliases = {len(operands): 0}
        operands.append(out)
        out_rows = out.shape[0]
    n_fill = 0 if out is not None else out_rows // rows - n_chunks
    n_in = len(operands)
    out_shape = [jax.ShapeDtypeStruct((out_rows, D_MODEL), F32)] + [
        jax.ShapeDtypeStruct(a.shape, F32) for a in states]
    out_specs = ([pl.BlockSpec((rows, D_MODEL), lambda c: (out_blk0 + c, 0))]
                 + [_const_spec(a.shape) for a in states])

    def body(*refs):
        refs = list(refs)
        for i in row_params:
            refs[i] = refs[i].at[pl.ds(l, 1)]
        keep = refs[:n_in - len(aliases)] + refs[n_in:]
        if not n_fill:
            _mixer_kernel(bt, tc, n_chunks, x_bm, *keep)
            return
        c = pl.program_id(0)
        x1_ref = keep[n_in - len(aliases)]

        @pl.when(c < n_chunks)
        def _chunk():
            _mixer_kernel(bt, tc, n_chunks, x_bm, *keep)

        @pl.when(c >= n_chunks)
        def _fill():
            x1_ref[...] = jnp.zeros_like(x1_ref)
    scratch = [
        pltpu.VMEM(((tc + RG_CONV - 1) * bt, D_RNN), F32),
        pltpu.VMEM(((tc + CF_CONV - 1) * bt, CF_W), F32),
        pltpu.VMEM((bt, D_RNN), F32),
        pltpu.VMEM((bt, S5_N), F32),
        pltpu.VMEM((bt, S5_N), F32),
        pltpu.VMEM((rows, D_MODEL), BF16),
        pltpu.VMEM((rows, P_IN), F32),
        pltpu.VMEM((rows, D_RNN), F32),
        pltpu.VMEM((rows, 2 * D_RNN), F32),
        pltpu.VMEM((rows, D_RNN), F32),
        pltpu.VMEM((rows, D_RNN), F32),
        pltpu.VMEM((rows, CF_W), F32),
        pltpu.VMEM((rows, S5_W), F32),
        pltpu.VMEM((rows, S5_W), BF16),
        pltpu.VMEM((rows, S5_W), F32),
        pltpu.VMEM((rows, S5_N), F32),
        pltpu.VMEM((rows, S5_N), F32),
        pltpu.VMEM((rows, D_MODEL), BF16),
    ]
    if x_bm:
        scratch.append(pltpu.VMEM((rows, D_MODEL), F32))
    return pl.pallas_call(
        body,
        grid=(n_chunks + n_fill,),
        in_specs=in_specs,
        out_specs=out_specs,
        out_shape=out_shape,
        scratch_shapes=scratch,
        input_output_aliases=aliases,
        compiler_params=pltpu.CompilerParams(dimension_semantics=("arbitrary",),
                                             vmem_limit_bytes=VMEM_LIMIT_BYTES),
        name="mixer",
    )(*operands)


def _route(logits):
    shape = logits.shape
    sub = lax.broadcasted_iota(jnp.int32, shape, 0).astype(F32)
    far = float(shape[0])
    neg = jnp.float32(-jnp.inf)
    is_grp = sub < N_GROUPS
    gl = jnp.where(is_grp, logits, neg)
    gmax = jnp.max(gl, axis=0, keepdims=True)
    gidx = jnp.min(jnp.where(gl == gmax, sub, far), axis=0, keepdims=True)
    p_sel = 1.0 / jnp.sum(jnp.where(is_grp, jnp.exp(logits - gmax), 0.0), axis=0, keepdims=True)
    lo = N_GROUPS + EXP_PER_GROUP * gidx
    in_grp = jnp.logical_and(sub >= lo, sub < lo + EXP_PER_GROUP)
    e1 = jnp.where(in_grp, logits, neg)
    v1 = jnp.max(e1, axis=0, keepdims=True)
    i1 = jnp.min(jnp.where(e1 == v1, sub, far), axis=0, keepdims=True)
    e2 = jnp.where(sub == i1, neg, e1)
    v2 = jnp.max(e2, axis=0, keepdims=True)
    i2 = jnp.min(jnp.where(e2 == v2, sub, far), axis=0, keepdims=True)
    ex = jnp.exp(v2 - v1)
    w1 = 1.0 / (1.0 + ex)
    w2 = ex * w1
    sub8 = lax.broadcasted_iota(jnp.int32, (8, shape[1]), 0).astype(F32)
    gate4 = jnp.where(sub8 == i1 - lo, p_sel * w1, jnp.where(sub8 == i2 - lo, p_sel * w2, 0.0))
    return gidx, gate4


def _moe_route_kernel(x_ref, n2_ref, wr_ref, br_ref, ts_ref, slot_ref, cnt_ref, tri_sc):
    rows = x_ref.shape[0]
    x = x_ref[...]
    t = _rms(x, n2_ref[...])
    t_hi = t.astype(BF16)
    t_lo = (t - t_hi.astype(F32)).astype(BF16)
    wr = wr_ref[...]
    wr_hi = wr.astype(BF16)
    wr_lo = (wr - wr_hi.astype(F32)).astype(BF16)
    nt_dims = (((1,), (1,)), ((), ()))
    logits = (lax.dot_general(wr_hi, t_hi, nt_dims, preferred_element_type=F32)
              + lax.dot_general(wr_hi, t_lo, nt_dims, preferred_element_type=F32)
              + lax.dot_general(wr_lo, t_hi, nt_dims, preferred_element_type=F32)) + br_ref[:, 0:1]
    gidx, gate4 = _route(logits)

    sub8 = lax.broadcasted_iota(jnp.int32, (8, rows), 0).astype(F32)
    onehot = jnp.where(sub8 == gidx, 1.0, 0.0)

    @pl.when(pl.program_id(0) == 0)
    def _build_tri():
        r_i = lax.broadcasted_iota(jnp.int32, (rows, rows), 0)
        c_i = lax.broadcasted_iota(jnp.int32, (rows, rows), 1)
        tri_sc[...] = jnp.where(r_i < c_i, 1.0, 0.0).astype(BF16)
    before = jnp.dot(onehot.astype(BF16), tri_sc[...], preferred_element_type=F32)
    rank = jnp.sum(before * onehot, axis=0, keepdims=True)
    counts = jnp.broadcast_to(jnp.sum(onehot, axis=1, keepdims=True), (8, ROUTER_LANES))

    units = jnp.ceil(counts * (1.0 / MOE_UNIT))
    first = MOE_UNIT * (pltpu.roll(units, 1, 0) + pltpu.roll(units, 2, 0) + pltpu.roll(units, 3, 0))
    slot_t = jnp.sum(onehot * first[:, 0:1], axis=0, keepdims=True) + rank
    s_sub = lax.broadcasted_iota(jnp.int32, (MOE_TILE_SLOTS, rows), 0).astype(F32)
    perm = jnp.where(s_sub == slot_t, 1.0, 0.0).astype(BF16)

    ts_ref[:, 0:D_MODEL] = jnp.dot(perm, t_hi, preferred_element_type=F32).astype(BF16)
    g_hi = gate4.astype(BF16).astype(F32)
    slot_rows = jnp.where(sub8 == 0.0, slot_t, 0.0)
    cols = jnp.concatenate(
        [g_hi, slot_rows, jnp.zeros((ROUTER_LANES // 2 - 16, rows), F32), gate4 - g_hi,
         jnp.zeros((ROUTER_LANES // 2 - 8, rows), F32)], axis=0)
    by_token = jnp.transpose(cols)
    lane = lax.broadcasted_iota(jnp.int32, (rows, ROUTER_LANES), 1)
    packed = jnp.where(lane == 8, 0.0, by_token).astype(BF16)
    ts_ref[:, D_MODEL:MOE_ROW_W] = jnp.dot(perm, packed, preferred_element_type=F32).astype(BF16)
    slot_ref[...] = jnp.broadcast_to(by_token[:, 8:9], (rows, ROUTER_LANES))
    cnt_ref[...] = counts


def _moe_expert_kernel(src_ref, grp_ref, nt_ref, *refs):
    units = refs[0:MOE_MTILE_UNITS]
    wg_ref, wu_ref, wd_ref, out_ref, wg_sc, wu_sc, wd_sc = refs[MOE_MTILE_UNITS:]
    del src_ref
    j = pl.program_id(0)
    active = j < nt_ref[0]

    @pl.when(jnp.logical_and(active, jnp.logical_or(j == 0, grp_ref[j] != grp_ref[jnp.maximum(j - 1, 0)])))
    def _cast_weights():
        for e in range(EXP_PER_GROUP):
            wg_sc[e] = wg_ref[e].astype(BF16)
            wu_sc[e] = wu_ref[e].astype(BF16)
            wd_sc[e] = wd_ref[e].astype(BF16)

    @pl.when(active)
    def _compute():
        tsg = jnp.concatenate([u[...] for u in units], axis=0)
        ts = tsg[:, 0:D_MODEL]
        gp = tsg[:, D_MODEL:MOE_ROW_W].astype(F32)
        gs = gp + pltpu.roll(gp, ROUTER_LANES // 2, 1)
        acc = None
        for e in range(EXP_PER_GROUP):
            hg = jnp.dot(ts, wg_sc[e], preferred_element_type=F32)
            hu = jnp.dot(ts, wu_sc[e], preferred_element_type=F32)
            act = (hg * _sigmoid(hg) * hu * gs[:, e:e + 1]).astype(BF16)
            o = jnp.dot(act, wd_sc[e], preferred_element_type=F32)
            acc = o if acc is None else acc + o
        out_ref[...] = acc.astype(BF16)

    @pl.when(jnp.logical_not(active))
    def _idle():
        out_ref[...] = jnp.zeros_like(out_ref)


def _moe_combine_kernel(final, bt_out, n_bm_tiles, pos_ref, x_ref, slot_ref, *refs):
    del pos_ref
    units = refs[0:MOE_TILE_UNITS]
    fn_ref = refs[MOE_TILE_UNITS]
    outs = refs[MOE_TILE_UNITS + 1:]
    rows = x_ref.shape[0]
    os_local = jnp.concatenate([u[...] for u in units], axis=0)
    s_lane = lax.broadcasted_iota(jnp.int32, (rows, MOE_TILE_SLOTS), 1).astype(F32)
    perm_t = jnp.where(s_lane == slot_ref[:, 0:1], 1.0, 0.0).astype(BF16)
    y = x_ref[...] + jnp.dot(perm_t, os_local, preferred_element_type=F32)
    if final:
        y = _rms(y, fn_ref[...])
    if bt_out:
        o_bm, o_rest = outs
        i = pl.program_id(0)

        @pl.when(i < n_bm_tiles)
        def _batch_major():
            for t in range(rows // bt_out):
                o_bm[:, t, :] = y[t * bt_out:(t + 1) * bt_out, :]

        @pl.when(i >= n_bm_tiles)
        def _rest():
            o_rest[...] = y
    else:
        (o_ref,) = outs
        o_ref[...] = y


def _moe_plan(counts, n_tiles, n_mtiles):
    i32 = jnp.int32
    g = jnp.arange(N_GROUPS, dtype=i32)
    c = jnp.arange(n_tiles, dtype=i32)
    earlier_g = (g[:, None] < g[None, :]).astype(i32)
    earlier_c = (c[:, None] < c[None, :]).astype(i32)
    units = (counts + (MOE_UNIT - 1)) // MOE_UNIT
    loc_off = jnp.sum(units[:, :, None] * earlier_g[None], axis=1)
    before_c = jnp.sum(units[:, None, :] * earlier_c[:, :, None], axis=0)
    grp_tiles = (jnp.sum(units, axis=0) + (MOE_MTILE_UNITS - 1)) // MOE_MTILE_UNITS
    tile_start = jnp.sum(grp_tiles[:, None] * earlier_g, axis=0)
    grp_end = tile_start + grp_tiles
    u = jnp.arange(MOE_TILE_UNITS, dtype=i32)
    g_of = jnp.sum((u[None, :, None] >= (loc_off + units)[:, None, :]).astype(i32), axis=-1)
    valid = g_of < N_GROUPS
    pick = (g_of[:, :, None] == g[None, None, :]).astype(i32)
    base = MOE_MTILE_UNITS * tile_start[None, :] + before_c - loc_off
    pos = jnp.sum(pick * base[:, None, :], axis=-1) + u[None, :]
    pos = jnp.where(valid, pos, 0).astype(i32)
    local_id = c[:, None] * MOE_TILE_UNITS + u[None, :]
    n_pos = n_mtiles * MOE_MTILE_UNITS
    p = jnp.arange(n_pos, dtype=i32)
    hit = jnp.logical_and(valid.reshape(-1)[None, :], pos.reshape(-1)[None, :] == p[:, None])
    src = jnp.sum(jnp.where(hit, local_id.reshape(-1)[None, :], 0), axis=-1).astype(i32)
    j = jnp.arange(n_mtiles, dtype=i32)
    grp = jnp.minimum(jnp.sum((j[:, None] >= grp_end[None, :]).astype(i32), axis=-1), N_GROUPS - 1)
    n_used = jnp.sum(grp_tiles, keepdims=True)
    return src, grp.astype(i32), n_used.astype(i32), pos.reshape(-1)


def _moe(x1, w, l, final, bt_out=0, n_bm_tiles=0):
    n_rows = x1.shape[0]
    rows = CHUNK_ROWS
    assert n_rows % rows == 0
    n_tiles = n_rows // rows
    n_mtiles = -(-(n_tiles * MOE_TILE_UNITS) // MOE_MTILE_UNITS) + N_GROUPS
    mrows = MOE_MTILE_UNITS * MOE_UNIT
    cparams = pltpu.CompilerParams(dimension_semantics=("arbitrary",), vmem_limit_bytes=VMEM_LIMIT_BYTES)

    row_spec = pl.BlockSpec((rows, D_MODEL), lambda i: (i, 0))
    def route_body(x_ref, n2_ref, *rest):
        _moe_route_kernel(x_ref, n2_ref.at[pl.ds(l, 1)], *rest)
    ts, slot, cnt = pl.pallas_call(
        route_body,
        grid=(n_tiles,),
        in_specs=[row_spec, _const_spec(w["norm2"].shape), _layer_spec(w["moe_w_r"], l),
                  _layer_spec(w["moe_b_r"], l)],
        out_specs=[pl.BlockSpec((MOE_TILE_SLOTS, MOE_ROW_W), lambda i: (i, 0)),
                   pl.BlockSpec((rows, ROUTER_LANES), lambda i: (i, 0)),
                   pl.BlockSpec((8, ROUTER_LANES), lambda i: (i, 0))],
        out_shape=[jax.ShapeDtypeStruct((n_tiles * MOE_TILE_SLOTS, MOE_ROW_W), BF16),
                   jax.ShapeDtypeStruct((n_rows, ROUTER_LANES), F32),
                   jax.ShapeDtypeStruct((n_tiles * 8, ROUTER_LANES), F32)],
        scratch_shapes=[pltpu.VMEM((rows, rows), BF16)],
        compiler_params=cparams,
        name="moe_route",
    )(x1, w["norm2"], w["moe_w_r"], w["moe_b_r"])

    counts = cnt.reshape(n_tiles, 8, ROUTER_LANES)[:, :N_GROUPS, 0].astype(jnp.int32)
    src, grp, n_used, pos = _moe_plan(counts, n_tiles, n_mtiles)

    def unit_in(k):
        return pl.BlockSpec((MOE_UNIT, MOE_ROW_W), lambda j, src, grp, nt: (src[j * MOE_MTILE_UNITS + k], 0))

    def grp_w(shape):
        return pl.BlockSpec((None, EXP_PER_GROUP) + shape, lambda j, src, grp, nt: (l, grp[j], 0, 0))
    os_sorted = pl.pallas_call(
        _moe_expert_kernel,
        grid_spec=pltpu.PrefetchScalarGridSpec(
            num_scalar_prefetch=3,
            grid=(n_mtiles,),
            in_specs=[unit_in(k) for k in range(MOE_MTILE_UNITS)]
            + [grp_w((D_MODEL, EXPERT_FF)), grp_w((D_MODEL, EXPERT_FF)), grp_w((EXPERT_FF, D_MODEL))],
            out_specs=pl.BlockSpec((mrows, D_MODEL), lambda j, src, grp, nt: (j, 0)),
            scratch_shapes=[pltpu.VMEM((EXP_PER_GROUP, D_MODEL, EXPERT_FF), BF16),
                            pltpu.VMEM((EXP_PER_GROUP, D_MODEL, EXPERT_FF), BF16),
                            pltpu.VMEM((EXP_PER_GROUP, EXPERT_FF, D_MODEL), BF16)],
        ),
        out_shape=jax.ShapeDtypeStruct((n_mtiles * mrows, D_MODEL), BF16),
        compiler_params=cparams,
        name="moe_experts",
    )(src, grp, n_used, *([ts] * MOE_MTILE_UNITS), w["moe_w_gate"], w["moe_w_up"], w["moe_w_down"])

    def unit_out(u):
        return pl.BlockSpec((MOE_UNIT, D_MODEL), lambda i, pos: (pos[i * MOE_TILE_UNITS + u], 0))
    if bt_out:
        tc = rows // bt_out
        last = n_bm_tiles - 1
        out_spec = [pl.BlockSpec((bt_out, tc, D_MODEL), lambda i, pos: (0, jnp.minimum(i, last), 0)),
                    pl.BlockSpec((rows, D_MODEL), lambda i, pos: (jnp.maximum(i - n_bm_tiles, 0), 0))]
        out_shape = [jax.ShapeDtypeStruct((bt_out, n_bm_tiles * tc, D_MODEL), F32),
                     jax.ShapeDtypeStruct(((n_tiles - n_bm_tiles) * rows, D_MODEL), F32)]
    else:
        out_spec = pl.BlockSpec((rows, D_MODEL), lambda i, pos: (i, 0))
        out_shape = jax.ShapeDtypeStruct((n_rows, D_MODEL), F32)
    return pl.pallas_call(
        functools.partial(_moe_combine_kernel, final, bt_out, n_bm_tiles),
        grid_spec=pltpu.PrefetchScalarGridSpec(
            num_scalar_prefetch=1,
            grid=(n_tiles,),
            in_specs=[pl.BlockSpec((rows, D_MODEL), lambda i, pos: (i, 0)),
                      pl.BlockSpec((rows, ROUTER_LANES), lambda i, pos: (i, 0))]
            + [unit_out(u) for u in range(MOE_TILE_UNITS)]
            + [pl.BlockSpec(w["final_norm"].shape, lambda i, pos: (0, 0))],
            out_specs=out_spec,
        ),
        out_shape=out_shape,
        compiler_params=cparams,
        name="moe_combine",
    )(pos, x1, slot, *([os_sorted] * MOE_TILE_UNITS), w["final_norm"])


def _block_diag(blocks):
    L, n, r, c = blocks.shape
    eye = jnp.eye(n, dtype=blocks.dtype)
    return (blocks[:, :, :, None, :] * eye[None, :, None, :, None]).reshape(L, n * r, n * c)


def _to_tm(a):
    a = jnp.swapaxes(a, 0, 1)
    return a.reshape((a.shape[0] * a.shape[1],) + a.shape[2:])


def _from_tm(a, b):
    return jnp.swapaxes(a.reshape((a.shape[0] // b, b) + a.shape[1:]), 0, 1)


def kernel(x_prompt, x_sample, state_rglru_h, state_rglru_conv, state_conf_conv, state_s5_re, state_s5_im, norm1, w_in, rg_conv_w, rg_conv_b, rg_wa, rg_ba, rg_wx, rg_bx, rg_lambda, cf_conv_w, cf_conv_b, cf_ln_g, cf_ln_b, s5_a_re, s5_a_im, s5_log_dt, s5_b_re, s5_b_im, s5_c_re, s5_c_im, s5_d, s5_w_glu, s5_b_glu, gn_rg, gn_cf, gn_s5, w_out, norm2, moe_w_grp, moe_b_grp, moe_w_exp, moe_b_exp, moe_w_gate, moe_w_up, moe_w_down, final_norm):
    L = w_in.shape[0]
    B, T, _ = x_prompt.shape
    DB, DT, _ = x_sample.shape

    ab_re, ab_im, bb_re, bb_im = _s5_prep(s5_a_re, s5_a_im, s5_log_dt,
                                          jnp.swapaxes(s5_b_re, 2, 3), jnp.swapaxes(s5_b_im, 2, 3))

    def row(a):
        return a.reshape(L, -1)

    W = dict(
        norm1=row(norm1), w_in=w_in.astype(BF16),
        rg_conv_w=rg_conv_w, rg_conv_b=row(rg_conv_b),
        rg_wg=jnp.concatenate([_block_diag(rg_wa), _block_diag(rg_wx)], axis=-1).astype(BF16),
        rg_bg=jnp.concatenate([row(rg_ba), row(rg_bx)], axis=-1),
        rg_lambda=row(rg_lambda),
        cf_conv_w=cf_conv_w, cf_conv_b=row(cf_conv_b), cf_ln_g=row(cf_ln_g), cf_ln_b=row(cf_ln_b),
        s5_ab_re=row(ab_re), s5_ab_im=row(ab_im),
        s5_bb_re=_block_diag(bb_re).astype(BF16), s5_bb_im=_block_diag(bb_im).astype(BF16),
        s5_c_re=_block_diag(jnp.swapaxes(s5_c_re, 2, 3)).astype(BF16),
        s5_c_im=_block_diag(jnp.swapaxes(s5_c_im, 2, 3)).astype(BF16),
        s5_d=row(s5_d), s5_w_glu=s5_w_glu.astype(BF16), s5_b_glu=row(s5_b_glu),
        gn_rg=row(gn_rg), gn_cf=row(gn_cf), gn_s5=row(gn_s5), w_out=w_out.astype(BF16),
        norm2=row(norm2),
        moe_w_r=jnp.swapaxes(jnp.concatenate(
            [moe_w_grp, moe_w_exp.reshape(L, D_MODEL, N_EXPERTS),
             jnp.zeros((L, D_MODEL, ROUTER_ROWS - N_GROUPS - N_EXPERTS), F32)], axis=-1), 1, 2),
        moe_b_r=jnp.broadcast_to(jnp.concatenate(
            [moe_b_grp, moe_b_exp.reshape(L, N_EXPERTS),
             jnp.zeros((L, ROUTER_ROWS - N_GROUPS - N_EXPERTS), F32)], axis=-1)[:, :, None],
            (L, ROUTER_ROWS, ROUTER_LANES)),
        moe_w_gate=moe_w_gate, moe_w_up=moe_w_up, moe_w_down=moe_w_down,
    )

    tc_p = CHUNK_ROWS // B
    n_p = (B * T) // CHUNK_ROWS
    n_s = (DB * DT) // CHUNK_ROWS
    assert n_p * CHUNK_ROWS == B * T and n_s == 1 and DB * DT == CHUNK_ROWS
    all_rows = (n_p + n_s) * CHUNK_ROWS
    x_p, x_s, s_blk0 = x_prompt, _to_tm(x_sample), 0
    p_states, s_states = [], []
    W["final_norm"] = final_norm.reshape(1, D_MODEL)
    st_p = (jnp.zeros((B, D_RNN), F32), jnp.zeros(((RG_CONV - 1) * B, D_RNN), F32),
            jnp.zeros(((CF_CONV - 1) * B, CF_W), F32), jnp.zeros((B, S5_N), F32),
            jnp.zeros((B, S5_N), F32))
    for l in range(L):
        st_s = (state_rglru_h[l], _to_tm(state_rglru_conv[l]), _to_tm(state_conf_conv[l]),
                state_s5_re[l].reshape(DB, S5_N), state_s5_im[l].reshape(DB, S5_N))
        x1, *np_l = _mixer(x_p, B, tc_p, n_p, st_p, W, l, out_rows=all_rows)
        x1, *ns_l = _mixer(x_s, DB, DT, n_s, st_s, W, l, x_blk0=s_blk0, out=x1, out_blk0=n_p)
        p_states.append(np_l)
        s_states.append(ns_l)
        if l < L - 1:
            x_p = x_s = _moe(x1, W, l, False)
            s_blk0 = n_p
        else:
            y_prompt, y_s = _moe(x1, W, l, True, bt_out=B, n_bm_tiles=n_p)

    def assemble(states, b):
        h, rc, cf, sre, sim = [jnp.stack([s[j] for s in states]) for j in range(5)]
        rc = jnp.swapaxes(rc.reshape(L, RG_CONV - 1, b, D_RNN), 1, 2)
        cf = jnp.swapaxes(cf.reshape(L, CF_CONV - 1, b, CF_W), 1, 2)
        return (h, rc, cf, sre.reshape(L, b, S5_GROUPS, S5_STATE),
                sim.reshape(L, b, S5_GROUPS, S5_STATE))

    y_sample = _from_tm(y_s, DB)
    return (y_prompt, y_sample) + assemble(p_states, B) + assemble(s_states, DB)
```

```python
import functools

import jax
import jax.numpy as jnp
from jax import lax
from jax.experimental import pallas as pl
from jax.experimental.pallas import tpu as pltpu

F32 = jnp.float32
BF16 = jnp.bfloat16

D_MODEL = 1024
D_RNN = 512
RG_HEADS = 8
RG_CONV = 4
RG_C = 8.0
CF_W = 256
CF_CONV = 31
S5_W = 256
S5_GROUPS = 16
S5_GROUP_CH = 16
S5_STATE = 64
S5_N = S5_GROUPS * S5_STATE
P_IN = 2 * D_RNN + 2 * CF_W + S5_W
N_GROUPS = 4
EXP_PER_GROUP = 4
N_EXPERTS = 16
EXPERT_FF = 256
EPS = 1e-6

ROUTER_LANES = 128
ROUTER_ROWS = 32
VMEM_LIMIT_BYTES = 60000 * 1024
ROW_BLOCK = 64
CHUNK_ROWS = 512
HP_TAIL_CHUNKS = 1
MOE_UNIT = 32
MOE_TILE_UNITS = CHUNK_ROWS // MOE_UNIT + N_GROUPS
MOE_TILE_SLOTS = MOE_TILE_UNITS * MOE_UNIT
MOE_MTILE_UNITS = 16
MOE_ROW_W = D_MODEL + ROUTER_LANES


def _rms(x, g):
    return x * lax.rsqrt(jnp.mean(x * x, axis=-1, keepdims=True) + EPS) * g


def _softplus(z):
    return jnp.maximum(z, 0.0) + jnp.log1p(jnp.exp(-jnp.abs(z)))


def _sigmoid(z):
    return 0.5 * jnp.tanh(0.5 * z) + 0.5


def _row_blocks(rows, rb, fn, unroll=False):
    if unroll:
        for i in range(rows // rb):
            fn(i * rb)
        return

    def body(i, carry):
        fn(pl.multiple_of(i * rb, rb))
        return carry
    lax.fori_loop(0, rows // rb, body, 0)


def _s5_prep_kernel(are_ref, aim_ref, ldt_ref, bre_ref, bim_ref,
                    abre_ref, abim_ref, bbre_ref, bbim_ref):
    a_re = are_ref[...]
    a_im = aim_ref[...]
    dt = jnp.exp(ldt_ref[...])
    mag = jnp.exp(a_re * dt)
    ab_re = mag * jnp.cos(a_im * dt)
    ab_im = mag * jnp.sin(a_im * dt)
    den = a_re * a_re + a_im * a_im
    nr = ab_re - 1.0
    ni = ab_im
    k_re = (nr * a_re + ni * a_im) / den
    k_im = (ni * a_re - nr * a_im) / den
    abre_ref[...] = ab_re
    abim_ref[...] = ab_im
    b_re = bre_ref[...]
    b_im = bim_ref[...]
    bbre_ref[...] = k_re[:, None, :] * b_re - k_im[:, None, :] * b_im
    bbim_ref[...] = k_re[:, None, :] * b_im + k_im[:, None, :] * b_re


def _s5_prep(a_re, a_im, log_dt, b_re_ghp, b_im_ghp):
    L = a_re.shape[0]
    gp = pl.BlockSpec((None, S5_GROUPS, S5_STATE), lambda l: (l, 0, 0))
    ghp = pl.BlockSpec((None, S5_GROUPS, S5_GROUP_CH, S5_STATE), lambda l: (l, 0, 0, 0))
    return pl.pallas_call(
        _s5_prep_kernel,
        grid=(L,),
        in_specs=[gp, gp, pl.BlockSpec((None, S5_GROUPS, 1), lambda l: (l, 0, 0)), ghp, ghp],
        out_specs=[gp, gp, ghp, ghp],
        out_shape=[jax.ShapeDtypeStruct((L, S5_GROUPS, S5_STATE), F32)] * 2
        + [jax.ShapeDtypeStruct((L, S5_GROUPS, S5_GROUP_CH, S5_STATE), F32)] * 2,
        name="s5_prep",
    )(a_re, a_im, log_dt[..., None], b_re_ghp, b_im_ghp)


def _mixer_kernel(bt, tc, n_chunks, x_bm, lo,
                  x_ref, h0_ref, rb_ref, fb_ref, sre_ref, sim_ref,
                  n1_ref, win_ref, rgw_ref, rgb_ref, wg_ref, bg_ref, lam_ref,
                  cfw_ref, cfb_ref, lng_ref, lnb_ref,
                  abre_ref, abim_ref, bbre_ref, bbim_ref, cre_ref, cim_ref, d_ref,
                  wglu_ref, bglu_ref, gnrg_ref, gncf_ref, gns5_ref, wout_ref,
                  x1_ref, hl_ref, rbn_ref, fbn_ref, sren_ref, simn_ref,
                  xpad, upad, h_sc, sre_sc, sim_sc, hn_sc, proj_sc, xc_sc, gate_sc,
                  a_sc, b_sc, v_sc, y_sc, zb_sc, gl_sc, bre_sc, bim_sc, mix_sc, *maybe_xtm):
    rows = tc * bt
    rb = ROW_BLOCK
    rg_hist = (RG_CONV - 1) * bt
    cf_hist = (CF_CONV - 1) * bt
    c = pl.program_id(0)
    act_dt = hn_sc.dtype

    def mm(a, w_ref, name):
        if lo is None:
            return jnp.dot(a.astype(BF16), w_ref[...], preferred_element_type=F32)
        a_hi = a.astype(BF16)
        a_lo = (a - a_hi.astype(F32)).astype(BF16)
        w_hi = w_ref[...]
        return (jnp.dot(a_hi, w_hi, preferred_element_type=F32)
                + jnp.dot(a_lo, w_hi, preferred_element_type=F32)
                + jnp.dot(a_hi, lo[name][...], preferred_element_type=F32))

    @pl.when(c == 0)
    def _init():
        xpad[0:rg_hist, :] = rb_ref[...]
        upad[0:cf_hist, :] = fb_ref[...]
        h_sc[...] = h0_ref[...]
        sre_sc[...] = sre_ref[...]
        sim_sc[...] = sim_ref[...]

    o_cf = 2 * D_RNN
    o_s5 = o_cf + 2 * CF_W

    if x_bm:
        (x_src,) = maybe_xtm
        for t in range(tc):
            x_src[t * bt:(t + 1) * bt, :] = x_ref[:, t, :]
    else:
        x_src = x_ref

    def norm_in(r0):
        x = x_src[r0:r0 + rb, :]
        hn_sc[r0:r0 + rb, :] = _rms(x, n1_ref[...]).astype(act_dt)
    _row_blocks(rows, rb, norm_in, unroll=True)
    proj_sc[...] = mm(hn_sc[...], win_ref, "w_in")

    xpad[rg_hist:rg_hist + rows, :] = proj_sc[:, 0:D_RNN]

    def rg_conv(r0):
        acc = rgb_ref[...] + rgw_ref[0:1, :] * xpad[r0:r0 + rb, :]
        for k in range(1, RG_CONV):
            acc = acc + rgw_ref[k:k + 1, :] * xpad[r0 + k * bt:r0 + k * bt + rb, :]
        xc_sc[r0:r0 + rb, :] = acc
    _row_blocks(rows, rb, rg_conv, unroll=True)

    gate_sc[...] = mm(xc_sc[...], wg_ref, "rg_wg") + bg_ref[...]

    su_all = proj_sc[:, o_s5:o_s5 + S5_W].astype(act_dt)
    bre_sc[...] = mm(su_all, bbre_ref, "s5_bb_re")
    bim_sc[...] = mm(su_all, bbim_ref, "s5_bb_im")

    neg_c_sp = -RG_C * _softplus(-lam_ref[...])

    def rg_coeffs(r0):
        g = gate_sc[r0:r0 + rb, :]
        r = _sigmoid(g[:, 0:D_RNN])
        i = _sigmoid(g[:, D_RNN:2 * D_RNN])
        a = jnp.exp(neg_c_sp * r)
        a_sc[r0:r0 + rb, :] = a
        b_sc[r0:r0 + rb, :] = jnp.sqrt(1.0 - a * a) * (i * xc_sc[r0:r0 + rb, :])
    _row_blocks(rows, rb, rg_coeffs, unroll=True)

    def cf_glu(r0):
        ca = proj_sc[r0:r0 + rb, o_cf:o_cf + CF_W]
        cg = proj_sc[r0:r0 + rb, o_cf + CF_W:o_cf + 2 * CF_W]
        upad[cf_hist + r0:cf_hist + r0 + rb, :] = ca * _sigmoid(cg)
    _row_blocks(rows, rb, cf_glu, unroll=True)

    h = h_sc[...]
    for t in range(tc):
        r0 = t * bt
        h = a_sc[r0:r0 + bt, :] * h + b_sc[r0:r0 + bt, :]
        b_sc[r0:r0 + bt, :] = h
    h_sc[...] = h
    hl_ref[...] = h

    a_r = jnp.broadcast_to(abre_ref[...], (bt, S5_N))
    a_i = jnp.broadcast_to(abim_ref[...], (bt, S5_N))
    s_r = sre_sc[...]
    s_i = sim_sc[...]
    for t in range(tc):
        r0 = t * bt
        n_r = a_r * s_r - a_i * s_i + bre_sc[r0:r0 + bt, :]
        n_i = a_r * s_i + a_i * s_r + bim_sc[r0:r0 + bt, :]
        bre_sc[r0:r0 + bt, :] = n_r
        bim_sc[r0:r0 + bt, :] = n_i
        s_r, s_i = n_r, n_i
    sre_sc[...] = s_r
    sim_sc[...] = s_i
    sren_ref[...] = s_r
    simn_ref[...] = s_i

    def cf_conv(r0):
        acc = cfb_ref[...] + cfw_ref[0:1, :] * upad[r0:r0 + rb, :]
        for k in range(1, CF_CONV):
            acc = acc + cfw_ref[k:k + 1, :] * upad[r0 + k * bt:r0 + k * bt + rb, :]
        v_sc[r0:r0 + rb, :] = acc
    _row_blocks(rows, rb, cf_conv, unroll=True)

    def rg_out(r0):
        h = b_sc[r0:r0 + rb, :]
        yb = proj_sc[r0:r0 + rb, D_RNN:2 * D_RNN]
        mix_sc[r0:r0 + rb, 0:D_RNN] = _rms(h * jax.nn.gelu(yb), gnrg_ref[...]).astype(act_dt)
    _row_blocks(rows, rb, rg_out, unroll=True)

    def cf_out(r0):
        v = v_sc[r0:r0 + rb, :]
        mu = jnp.mean(v, axis=-1, keepdims=True)
        cen = v - mu
        var = jnp.mean(cen * cen, axis=-1, keepdims=True)
        vn = cen * lax.rsqrt(var + EPS) * lng_ref[...] + lnb_ref[...]
        mix_sc[r0:r0 + rb, D_RNN:D_RNN + CF_W] = _rms(vn * _sigmoid(vn), gncf_ref[...]).astype(act_dt)
    _row_blocks(rows, rb, cf_out, unroll=True)

    y_sc[...] = mm(bre_sc[...], cre_ref, "s5_c_re") - mm(bim_sc[...], cim_ref, "s5_c_im")

    def s5_gelu(r0):
        su = proj_sc[r0:r0 + rb, o_s5:o_s5 + S5_W]
        z = jax.nn.gelu(y_sc[r0:r0 + rb, :] + d_ref[...] * su)
        y_sc[r0:r0 + rb, :] = z
        zb_sc[r0:r0 + rb, :] = z.astype(act_dt)
    _row_blocks(rows, rb, s5_gelu, unroll=True)
    gl_sc[...] = mm(zb_sc[...], wglu_ref, "s5_w_glu") + bglu_ref[...]

    def s5_out(r0):
        oc = y_sc[r0:r0 + rb, :] * _sigmoid(gl_sc[r0:r0 + rb, :])
        mix_sc[r0:r0 + rb, D_RNN + CF_W:D_MODEL] = _rms(oc, gns5_ref[...]).astype(act_dt)
    _row_blocks(rows, rb, s5_out, unroll=True)

    x1_ref[...] = x_src[...] + mm(mix_sc[...], wout_ref, "w_out")

    rbn_ref[...] = xpad[rows:rows + rg_hist, :]
    fbn_ref[...] = upad[rows:rows + cf_hist, :]
    if n_chunks > 1:
        xpad[0:rg_hist, :] = xpad[rows:rows + rg_hist, :]
        upad[0:cf_hist, :] = upad[rows:rows + cf_hist, :]


def _const_spec(shape):
    nd = len(shape)
    return pl.BlockSpec(shape, lambda *_: (0,) * nd)


def _layer_spec(a, l):
    nd = a.ndim - 1
    return pl.BlockSpec((None,) + a.shape[1:], lambda *_: (l,) + (0,) * nd)


MATMUL_WEIGHTS = ("w_in", "rg_wg", "s5_bb_re", "s5_bb_im", "s5_c_re", "s5_c_im", "s5_w_glu", "w_out")


def _mixer(x, bt, tc, n_chunks, state, w, l, x_blk0=0, out_rows=None, out=None, out_blk0=0,
           w_lo=None):
    x_bm = x.ndim == 3
    rows = tc * bt
    assert rows % ROW_BLOCK == 0 and bt % 8 == 0
    assert n_chunks == 1 or tc >= CF_CONV - 1
    h0, rbuf, fbuf, s_re, s_im = state
    weights = [w["norm1"], w["w_in"], w["rg_conv_w"], w["rg_conv_b"], w["rg_wg"], w["rg_bg"],
               w["rg_lambda"], w["cf_conv_w"], w["cf_conv_b"], w["cf_ln_g"], w["cf_ln_b"],
               w["s5_ab_re"], w["s5_ab_im"], w["s5_bb_re"], w["s5_bb_im"], w["s5_c_re"], w["s5_c_im"],
               w["s5_d"], w["s5_w_glu"], w["s5_b_glu"], w["gn_rg"], w["gn_cf"], w["gn_s5"], w["w_out"]]
    states = [h0, rbuf, fbuf, s_re, s_im]
    last = n_chunks - 1
    if x_bm:
        x_spec = pl.BlockSpec((bt, tc, D_MODEL), lambda c: (0, x_blk0 + jnp.minimum(c, last), 0))
    else:
        x_spec = pl.BlockSpec((rows, D_MODEL), lambda c: (x_blk0 + jnp.minimum(c, last), 0))
    in_specs = ([x_spec] + [_const_spec(a.shape) for a in states]
                + [_const_spec(a.shape) if a.ndim == 2 else _layer_spec(a, l) for a in weights])
    row_params = [1 + len(states) + i for i, a in enumerate(weights) if a.ndim == 2]
    operands = [x, *states, *weights]
    n_lo = 0
    if w_lo is not None:
        n_lo = len(MATMUL_WEIGHTS)
        in_specs += [_layer_spec(w_lo[k], l) for k in MATMUL_WEIGHTS]
        operands += [w_lo[k] for k in MATMUL_WEIGHTS]
    act_dt = BF16 if w_lo is None else F32
    aliases = {}
    if out is not None:
        in_specs.append(pl.BlockSpec(memory_space=pl.ANY))
        aliases = {len(operands): 0}
        operands.append(out)
        out_rows = out.shape[0]
    n_fill = 0 if out is not None else out_rows // rows - n_chunks
    n_in = len(operands)
    out_shape = [jax.ShapeDtypeStruct((out_rows, D_MODEL), F32)] + [
        jax.ShapeDtypeStruct(a.shape, F32) for a in states]
    out_specs = ([pl.BlockSpec((rows, D_MODEL), lambda c: (out_blk0 + c, 0))]
                 + [_const_spec(a.shape) for a in states])

    def body(*refs):
        refs = list(refs)
        for i in row_params:
            refs[i] = refs[i].at[pl.ds(l, 1)]
        n_w = n_in - len(aliases) - n_lo
        lo = dict(zip(MATMUL_WEIGHTS, refs[n_w:n_w + n_lo])) if n_lo else None
        keep = refs[:n_w] + refs[n_in:]
        if not n_fill:
            _mixer_kernel(bt, tc, n_chunks, x_bm, lo, *keep)
            return
        c = pl.program_id(0)
        x1_ref = keep[n_w]

        @pl.when(c < n_chunks)
        def _chunk():
            _mixer_kernel(bt, tc, n_chunks, x_bm, lo, *keep)

        @pl.when(c >= n_chunks)
        def _fill():
            x1_ref[...] = jnp.zeros_like(x1_ref)
    scratch = [
        pltpu.VMEM(((tc + RG_CONV - 1) * bt, D_RNN), F32),
        pltpu.VMEM(((tc + CF_CONV - 1) * bt, CF_W), F32),
        pltpu.VMEM((bt, D_RNN), F32),
        pltpu.VMEM((bt, S5_N), F32),
        pltpu.VMEM((bt, S5_N), F32),
        pltpu.VMEM((rows, D_MODEL), act_dt),
        pltpu.VMEM((rows, P_IN), F32),
        pltpu.VMEM((rows, D_RNN), F32),
        pltpu.VMEM((rows, 2 * D_RNN), F32),
        pltpu.VMEM((rows, D_RNN), F32),
        pltpu.VMEM((rows, D_RNN), F32),
        pltpu.VMEM((rows, CF_W), F32),
        pltpu.VMEM((rows, S5_W), F32),
        pltpu.VMEM((rows, S5_W), act_dt),
        pltpu.VMEM((rows, S5_W), F32),
        pltpu.VMEM((rows, S5_N), F32),
        pltpu.VMEM((rows, S5_N), F32),
        pltpu.VMEM((rows, D_MODEL), act_dt),
    ]
    if x_bm:
        scratch.append(pltpu.VMEM((rows, D_MODEL), F32))
    return pl.pallas_call(
        body,
        grid=(n_chunks + n_fill,),
        in_specs=in_specs,
        out_specs=out_specs,
        out_shape=out_shape,
        scratch_shapes=scratch,
        input_output_aliases=aliases,
        compiler_params=pltpu.CompilerParams(dimension_semantics=("arbitrary",),
                                             vmem_limit_bytes=VMEM_LIMIT_BYTES),
        name="mixer",
    )(*operands)


def _route(logits):
    shape = logits.shape
    sub = lax.broadcasted_iota(jnp.int32, shape, 0).astype(F32)
    far = float(shape[0])
    neg = jnp.float32(-jnp.inf)
    is_grp = sub < N_GROUPS
    gl = jnp.where(is_grp, logits, neg)
    gmax = jnp.max(gl, axis=0, keepdims=True)
    gidx = jnp.min(jnp.where(gl == gmax, sub, far), axis=0, keepdims=True)
    p_sel = 1.0 / jnp.sum(jnp.where(is_grp, jnp.exp(logits - gmax), 0.0), axis=0, keepdims=True)
    lo = N_GROUPS + EXP_PER_GROUP * gidx
    in_grp = jnp.logical_and(sub >= lo, sub < lo + EXP_PER_GROUP)
    e1 = jnp.where(in_grp, logits, neg)
    v1 = jnp.max(e1, axis=0, keepdims=True)
    i1 = jnp.min(jnp.where(e1 == v1, sub, far), axis=0, keepdims=True)
    e2 = jnp.where(sub == i1, neg, e1)
    v2 = jnp.max(e2, axis=0, keepdims=True)
    i2 = jnp.min(jnp.where(e2 == v2, sub, far), axis=0, keepdims=True)
    ex = jnp.exp(v2 - v1)
    w1 = 1.0 / (1.0 + ex)
    w2 = ex * w1
    sub8 = lax.broadcasted_iota(jnp.int32, (8, shape[1]), 0).astype(F32)
    gate4 = jnp.where(sub8 == i1 - lo, p_sel * w1, jnp.where(sub8 == i2 - lo, p_sel * w2, 0.0))
    return gidx, gate4


def _moe_route_kernel(x_ref, n2_ref, wr_ref, br_ref, ts_ref, slot_ref, cnt_ref, tri_sc):
    rows = x_ref.shape[0]
    x = x_ref[...]
    t = _rms(x, n2_ref[...])
    t_hi = t.astype(BF16)
    t_lo = (t - t_hi.astype(F32)).astype(BF16)
    wr = wr_ref[...]
    wr_hi = wr.astype(BF16)
    wr_lo = (wr - wr_hi.astype(F32)).astype(BF16)
    nt_dims = (((1,), (1,)), ((), ()))
    logits = (lax.dot_general(wr_hi, t_hi, nt_dims, preferred_element_type=F32)
              + lax.dot_general(wr_hi, t_lo, nt_dims, preferred_element_type=F32)
              + lax.dot_general(wr_lo, t_hi, nt_dims, preferred_element_type=F32)) + br_ref[:, 0:1]
    gidx, gate4 = _route(logits)

    sub8 = lax.broadcasted_iota(jnp.int32, (8, rows), 0).astype(F32)
    onehot = jnp.where(sub8 == gidx, 1.0, 0.0)

    @pl.when(pl.program_id(0) == 0)
    def _build_tri():
        r_i = lax.broadcasted_iota(jnp.int32, (rows, rows), 0)
        c_i = lax.broadcasted_iota(jnp.int32, (rows, rows), 1)
        tri_sc[...] = jnp.where(r_i < c_i, 1.0, 0.0).astype(BF16)
    before = jnp.dot(onehot.astype(BF16), tri_sc[...], preferred_element_type=F32)
    rank = jnp.sum(before * onehot, axis=0, keepdims=True)
    counts = jnp.broadcast_to(jnp.sum(onehot, axis=1, keepdims=True), (8, ROUTER_LANES))

    units = jnp.ceil(counts * (1.0 / MOE_UNIT))
    first = MOE_UNIT * (pltpu.roll(units, 1, 0) + pltpu.roll(units, 2, 0) + pltpu.roll(units, 3, 0))
    slot_t = jnp.sum(onehot * first[:, 0:1], axis=0, keepdims=True) + rank
    s_sub = lax.broadcasted_iota(jnp.int32, (MOE_TILE_SLOTS, rows), 0).astype(F32)
    perm = jnp.where(s_sub == slot_t, 1.0, 0.0).astype(BF16)

    ts_ref[:, 0:D_MODEL] = jnp.dot(perm, t_hi, preferred_element_type=F32).astype(BF16)
    g_hi = gate4.astype(BF16).astype(F32)
    slot_rows = jnp.where(sub8 == 0.0, slot_t, 0.0)
    cols = jnp.concatenate(
        [g_hi, slot_rows, jnp.zeros((ROUTER_LANES // 2 - 16, rows), F32), gate4 - g_hi,
         jnp.zeros((ROUTER_LANES // 2 - 8, rows), F32)], axis=0)
    by_token = jnp.transpose(cols)
    lane = lax.broadcasted_iota(jnp.int32, (rows, ROUTER_LANES), 1)
    packed = jnp.where(lane == 8, 0.0, by_token).astype(BF16)
    ts_ref[:, D_MODEL:MOE_ROW_W] = jnp.dot(perm, packed, preferred_element_type=F32).astype(BF16)
    slot_ref[...] = jnp.broadcast_to(by_token[:, 8:9], (rows, ROUTER_LANES))
    cnt_ref[...] = counts


def _moe_expert_kernel(src_ref, grp_ref, nt_ref, *refs):
    units = refs[0:MOE_MTILE_UNITS]
    wg_ref, wu_ref, wd_ref, out_ref, wg_sc, wu_sc, wd_sc = refs[MOE_MTILE_UNITS:]
    del src_ref
    j = pl.program_id(0)
    active = j < nt_ref[0]

    @pl.when(jnp.logical_and(active, jnp.logical_or(j == 0, grp_ref[j] != grp_ref[jnp.maximum(j - 1, 0)])))
    def _cast_weights():
        for e in range(EXP_PER_GROUP):
            wg_sc[e] = wg_ref[e].astype(BF16)
            wu_sc[e] = wu_ref[e].astype(BF16)
            wd_sc[e] = wd_ref[e].astype(BF16)

    @pl.when(active)
    def _compute():
        tsg = jnp.concatenate([u[...] for u in units], axis=0)
        ts = tsg[:, 0:D_MODEL]
        gp = tsg[:, D_MODEL:MOE_ROW_W].astype(F32)
        gs = gp + pltpu.roll(gp, ROUTER_LANES // 2, 1)
        acc = None
        for e in range(EXP_PER_GROUP):
            hg = jnp.dot(ts, wg_sc[e], preferred_element_type=F32)
            hu = jnp.dot(ts, wu_sc[e], preferred_element_type=F32)
            act = (hg * _sigmoid(hg) * hu * gs[:, e:e + 1]).astype(BF16)
            o = jnp.dot(act, wd_sc[e], preferred_element_type=F32)
            acc = o if acc is None else acc + o
        out_ref[...] = acc.astype(BF16)

    @pl.when(jnp.logical_not(active))
    def _idle():
        out_ref[...] = jnp.zeros_like(out_ref)


def _moe_combine_kernel(final, bt_out, n_bm_tiles, pos_ref, x_ref, slot_ref, *refs):
    del pos_ref
    units = refs[0:MOE_TILE_UNITS]
    fn_ref = refs[MOE_TILE_UNITS]
    outs = refs[MOE_TILE_UNITS + 1:]
    rows = x_ref.shape[0]
    os_local = jnp.concatenate([u[...] for u in units], axis=0)
    s_lane = lax.broadcasted_iota(jnp.int32, (rows, MOE_TILE_SLOTS), 1).astype(F32)
    perm_t = jnp.where(s_lane == slot_ref[:, 0:1], 1.0, 0.0).astype(BF16)
    y = x_ref[...] + jnp.dot(perm_t, os_local, preferred_element_type=F32)
    if final:
        y = _rms(y, fn_ref[...])
    if bt_out:
        o_bm, o_rest = outs
        i = pl.program_id(0)

        @pl.when(i < n_bm_tiles)
        def _batch_major():
            for t in range(rows // bt_out):
                o_bm[:, t, :] = y[t * bt_out:(t + 1) * bt_out, :]

        @pl.when(i >= n_bm_tiles)
        def _rest():
            o_rest[...] = y
    else:
        (o_ref,) = outs
        o_ref[...] = y


def _moe_plan(counts, n_tiles, n_mtiles):
    i32 = jnp.int32
    g = jnp.arange(N_GROUPS, dtype=i32)
    c = jnp.arange(n_tiles, dtype=i32)
    earlier_g = (g[:, None] < g[None, :]).astype(i32)
    earlier_c = (c[:, None] < c[None, :]).astype(i32)
    units = (counts + (MOE_UNIT - 1)) // MOE_UNIT
    loc_off = jnp.sum(units[:, :, None] * earlier_g[None], axis=1)
    before_c = jnp.sum(units[:, None, :] * earlier_c[:, :, None], axis=0)
    grp_tiles = (jnp.sum(units, axis=0) + (MOE_MTILE_UNITS - 1)) // MOE_MTILE_UNITS
    tile_start = jnp.sum(grp_tiles[:, None] * earlier_g, axis=0)
    grp_end = tile_start + grp_tiles
    u = jnp.arange(MOE_TILE_UNITS, dtype=i32)
    g_of = jnp.sum((u[None, :, None] >= (loc_off + units)[:, None, :]).astype(i32), axis=-1)
    valid = g_of < N_GROUPS
    pick = (g_of[:, :, None] == g[None, None, :]).astype(i32)
    base = MOE_MTILE_UNITS * tile_start[None, :] + before_c - loc_off
    pos = jnp.sum(pick * base[:, None, :], axis=-1) + u[None, :]
    pos = jnp.where(valid, pos, 0).astype(i32)
    local_id = c[:, None] * MOE_TILE_UNITS + u[None, :]
    n_pos = n_mtiles * MOE_MTILE_UNITS
    p = jnp.arange(n_pos, dtype=i32)
    hit = jnp.logical_and(valid.reshape(-1)[None, :], pos.reshape(-1)[None, :] == p[:, None])
    src = jnp.sum(jnp.where(hit, local_id.reshape(-1)[None, :], 0), axis=-1).astype(i32)
    j = jnp.arange(n_mtiles, dtype=i32)
    grp = jnp.minimum(jnp.sum((j[:, None] >= grp_end[None, :]).astype(i32), axis=-1), N_GROUPS - 1)
    n_used = jnp.sum(grp_tiles, keepdims=True)
    return src, grp.astype(i32), n_used.astype(i32), pos.reshape(-1)


def _moe(x1, w, l, final, bt_out=0, n_bm_tiles=0):
    n_rows = x1.shape[0]
    rows = CHUNK_ROWS
    assert n_rows % rows == 0
    n_tiles = n_rows // rows
    n_mtiles = -(-(n_tiles * MOE_TILE_UNITS) // MOE_MTILE_UNITS) + N_GROUPS
    mrows = MOE_MTILE_UNITS * MOE_UNIT
    cparams = pltpu.CompilerParams(dimension_semantics=("arbitrary",), vmem_limit_bytes=VMEM_LIMIT_BYTES)

    row_spec = pl.BlockSpec((rows, D_MODEL), lambda i: (i, 0))
    def route_body(x_ref, n2_ref, *rest):
        _moe_route_kernel(x_ref, n2_ref.at[pl.ds(l, 1)], *rest)
    ts, slot, cnt = pl.pallas_call(
        route_body,
        grid=(n_tiles,),
        in_specs=[row_spec, _const_spec(w["norm2"].shape), _layer_spec(w["moe_w_r"], l),
                  _layer_spec(w["moe_b_r"], l)],
        out_specs=[pl.BlockSpec((MOE_TILE_SLOTS, MOE_ROW_W), lambda i: (i, 0)),
                   pl.BlockSpec((rows, ROUTER_LANES), lambda i: (i, 0)),
                   pl.BlockSpec((8, ROUTER_LANES), lambda i: (i, 0))],
        out_shape=[jax.ShapeDtypeStruct((n_tiles * MOE_TILE_SLOTS, MOE_ROW_W), BF16),
                   jax.ShapeDtypeStruct((n_rows, ROUTER_LANES), F32),
                   jax.ShapeDtypeStruct((n_tiles * 8, ROUTER_LANES), F32)],
        scratch_shapes=[pltpu.VMEM((rows, rows), BF16)],
        compiler_params=cparams,
        name="moe_route",
    )(x1, w["norm2"], w["moe_w_r"], w["moe_b_r"])

    counts = cnt.reshape(n_tiles, 8, ROUTER_LANES)[:, :N_GROUPS, 0].astype(jnp.int32)
    src, grp, n_used, pos = _moe_plan(counts, n_tiles, n_mtiles)

    def unit_in(k):
        return pl.BlockSpec((MOE_UNIT, MOE_ROW_W), lambda j, src, grp, nt: (src[j * MOE_MTILE_UNITS + k], 0))

    def grp_w(shape):
        return pl.BlockSpec((None, EXP_PER_GROUP) + shape, lambda j, src, grp, nt: (l, grp[j], 0, 0))
    os_sorted = pl.pallas_call(
        _moe_expert_kernel,
        grid_spec=pltpu.PrefetchScalarGridSpec(
            num_scalar_prefetch=3,
            grid=(n_mtiles,),
            in_specs=[unit_in(k) for k in range(MOE_MTILE_UNITS)]
            + [grp_w((D_MODEL, EXPERT_FF)), grp_w((D_MODEL, EXPERT_FF)), grp_w((EXPERT_FF, D_MODEL))],
            out_specs=pl.BlockSpec((mrows, D_MODEL), lambda j, src, grp, nt: (j, 0)),
            scratch_shapes=[pltpu.VMEM((EXP_PER_GROUP, D_MODEL, EXPERT_FF), BF16),
                            pltpu.VMEM((EXP_PER_GROUP, D_MODEL, EXPERT_FF), BF16),
                            pltpu.VMEM((EXP_PER_GROUP, EXPERT_FF, D_MODEL), BF16)],
        ),
        out_shape=jax.ShapeDtypeStruct((n_mtiles * mrows, D_MODEL), BF16),
        compiler_params=cparams,
        name="moe_experts",
    )(src, grp, n_used, *([ts] * MOE_MTILE_UNITS), w["moe_w_gate"], w["moe_w_up"], w["moe_w_down"])

    def unit_out(u):
        return pl.BlockSpec((MOE_UNIT, D_MODEL), lambda i, pos: (pos[i * MOE_TILE_UNITS + u], 0))
    if bt_out:
        tc = rows // bt_out
        last = n_bm_tiles - 1
        out_spec = [pl.BlockSpec((bt_out, tc, D_MODEL), lambda i, pos: (0, jnp.minimum(i, last), 0)),
                    pl.BlockSpec((rows, D_MODEL), lambda i, pos: (jnp.maximum(i - n_bm_tiles, 0), 0))]
        out_shape = [jax.ShapeDtypeStruct((bt_out, n_bm_tiles * tc, D_MODEL), F32),
                     jax.ShapeDtypeStruct(((n_tiles - n_bm_tiles) * rows, D_MODEL), F32)]
    else:
        out_spec = pl.BlockSpec((rows, D_MODEL), lambda i, pos: (i, 0))
        out_shape = jax.ShapeDtypeStruct((n_rows, D_MODEL), F32)
    return pl.pallas_call(
        functools.partial(_moe_combine_kernel, final, bt_out, n_bm_tiles),
        grid_spec=pltpu.PrefetchScalarGridSpec(
            num_scalar_prefetch=1,
            grid=(n_tiles,),
            in_specs=[pl.BlockSpec((rows, D_MODEL), lambda i, pos: (i, 0)),
                      pl.BlockSpec((rows, ROUTER_LANES), lambda i, pos: (i, 0))]
            + [unit_out(u) for u in range(MOE_TILE_UNITS)]
            + [pl.BlockSpec(w["final_norm"].shape, lambda i, pos: (0, 0))],
            out_specs=out_spec,
        ),
        out_shape=out_shape,
        compiler_params=cparams,
        name="moe_combine",
    )(pos, x1, slot, *([os_sorted] * MOE_TILE_UNITS), w["final_norm"])


def _block_diag(blocks):
    L, n, r, c = blocks.shape
    eye = jnp.eye(n, dtype=blocks.dtype)
    return (blocks[:, :, :, None, :] * eye[None, :, None, :, None]).reshape(L, n * r, n * c)


def _to_tm(a):
    a = jnp.swapaxes(a, 0, 1)
    return a.reshape((a.shape[0] * a.shape[1],) + a.shape[2:])


def _from_tm(a, b):
    return jnp.swapaxes(a.reshape((a.shape[0] // b, b) + a.shape[1:]), 0, 1)


def kernel(x_prompt, x_sample, state_rglru_h, state_rglru_conv, state_conf_conv, state_s5_re, state_s5_im, norm1, w_in, rg_conv_w, rg_conv_b, rg_wa, rg_ba, rg_wx, rg_bx, rg_lambda, cf_conv_w, cf_conv_b, cf_ln_g, cf_ln_b, s5_a_re, s5_a_im, s5_log_dt, s5_b_re, s5_b_im, s5_c_re, s5_c_im, s5_d, s5_w_glu, s5_b_glu, gn_rg, gn_cf, gn_s5, w_out, norm2, moe_w_grp, moe_b_grp, moe_w_exp, moe_b_exp, moe_w_gate, moe_w_up, moe_w_down, final_norm):
    L = w_in.shape[0]
    B, T, _ = x_prompt.shape
    DB, DT, _ = x_sample.shape

    ab_re, ab_im, bb_re, bb_im = _s5_prep(s5_a_re, s5_a_im, s5_log_dt,
                                          jnp.swapaxes(s5_b_re, 2, 3), jnp.swapaxes(s5_b_im, 2, 3))

    def row(a):
        return a.reshape(L, -1)

    big = dict(
        w_in=w_in, rg_wg=jnp.concatenate([_block_diag(rg_wa), _block_diag(rg_wx)], axis=-1),
        s5_bb_re=_block_diag(bb_re), s5_bb_im=_block_diag(bb_im),
        s5_c_re=_block_diag(jnp.swapaxes(s5_c_re, 2, 3)),
        s5_c_im=_block_diag(jnp.swapaxes(s5_c_im, 2, 3)),
        s5_w_glu=s5_w_glu, w_out=w_out)
    assert set(big) == set(MATMUL_WEIGHTS)
    W = {k: v.astype(BF16) for k, v in big.items()}
    W_lo = {k: (v - W[k].astype(F32)).astype(BF16) for k, v in big.items()}
    W.update(
        norm1=row(norm1), rg_conv_w=rg_conv_w, rg_conv_b=row(rg_conv_b),
        rg_bg=jnp.concatenate([row(rg_ba), row(rg_bx)], axis=-1),
        rg_lambda=row(rg_lambda),
        cf_conv_w=cf_conv_w, cf_conv_b=row(cf_conv_b), cf_ln_g=row(cf_ln_g), cf_ln_b=row(cf_ln_b),
        s5_ab_re=row(ab_re), s5_ab_im=row(ab_im),
        s5_d=row(s5_d), s5_b_glu=row(s5_b_glu),
        gn_rg=row(gn_rg), gn_cf=row(gn_cf), gn_s5=row(gn_s5),
        norm2=row(norm2),
        moe_w_r=jnp.swapaxes(jnp.concatenate(
            [moe_w_grp, moe_w_exp.reshape(L, D_MODEL, N_EXPERTS),
             jnp.zeros((L, D_MODEL, ROUTER_ROWS - N_GROUPS - N_EXPERTS), F32)], axis=-1), 1, 2),
        moe_b_r=jnp.broadcast_to(jnp.concatenate(
            [moe_b_grp, moe_b_exp.reshape(L, N_EXPERTS),
             jnp.zeros((L, ROUTER_ROWS - N_GROUPS - N_EXPERTS), F32)], axis=-1)[:, :, None],
            (L, ROUTER_ROWS, ROUTER_LANES)),
        moe_w_gate=moe_w_gate, moe_w_up=moe_w_up, moe_w_down=moe_w_down,
    )

    tc_p = CHUNK_ROWS // B
    n_p = (B * T) // CHUNK_ROWS
    n_s = (DB * DT) // CHUNK_ROWS
    assert n_p * CHUNK_ROWS == B * T and n_s == 1 and DB * DT == CHUNK_ROWS
    all_rows = (n_p + n_s) * CHUNK_ROWS
    x_p, x_s, s_blk0 = x_prompt, _to_tm(x_sample), 0
    p_states, s_states = [], []
    W["final_norm"] = final_norm.reshape(1, D_MODEL)
    st_p = (jnp.zeros((B, D_RNN), F32), jnp.zeros(((RG_CONV - 1) * B, D_RNN), F32),
            jnp.zeros(((CF_CONV - 1) * B, CF_W), F32), jnp.zeros((B, S5_N), F32),
            jnp.zeros((B, S5_N), F32))
    for l in range(L):
        st_s = (state_rglru_h[l], _to_tm(state_rglru_conv[l]), _to_tm(state_conf_conv[l]),
                state_s5_re[l].reshape(DB, S5_N), state_s5_im[l].reshape(DB, S5_N))
        if l == 0:
            n_lp = n_p - HP_TAIL_CHUNKS
            x1, *st_mid = _mixer(x_p, B, tc_p, n_lp, st_p, W, l, out_rows=all_rows)
            x1, *np_l = _mixer(x_p, B, tc_p, HP_TAIL_CHUNKS, tuple(st_mid), W, l, x_blk0=n_lp,
                               out=x1, out_blk0=n_lp, w_lo=W_lo)
        else:
            x1, *np_l = _mixer(x_p, B, tc_p, n_p, st_p, W, l, out_rows=all_rows)
        x1, *ns_l = _mixer(x_s, DB, DT, n_s, st_s, W, l, x_blk0=s_blk0, out=x1, out_blk0=n_p)
        p_states.append(np_l)
        s_states.append(ns_l)
        if l < L - 1:
            x_p = x_s = _moe(x1, W, l, False)
            s_blk0 = n_p
        else:
            y_prompt, y_s = _moe(x1, W, l, True, bt_out=B, n_bm_tiles=n_p)

    def assemble(states, b):
        h, rc, cf, sre, sim = [jnp.stack([s[j] for s in states]) for j in range(5)]
        rc = jnp.swapaxes(rc.reshape(L, RG_CONV - 1, b, D_RNN), 1, 2)
        cf = jnp.swapaxes(cf.reshape(L, CF_CONV - 1, b, CF_W), 1, 2)
        return (h, rc, cf, sre.reshape(L, b, S5_GROUPS, S5_STATE),
                sim.reshape(L, b, S5_GROUPS, S5_STATE))

    y_sample = _from_tm(y_s, DB)
    return (y_prompt, y_sample) + assemble(p_states, B) + assemble(s_states, DB)
```

```python
import functools

import jax
import jax.numpy as jnp
from jax import lax
from jax.experimental import pallas as pl
from jax.experimental.pallas import tpu as pltpu

F32 = jnp.float32
BF16 = jnp.bfloat16

D_MODEL = 1024
D_RNN = 512
RG_HEADS = 8
RG_CONV = 4
RG_C = 8.0
CF_W = 256
CF_CONV = 31
S5_W = 256
S5_GROUPS = 16
S5_GROUP_CH = 16
S5_STATE = 64
S5_N = S5_GROUPS * S5_STATE
P_IN = 2 * D_RNN + 2 * CF_W + S5_W
N_GROUPS = 4
EXP_PER_GROUP = 4
N_EXPERTS = 16
EXPERT_FF = 256
EPS = 1e-6

ROUTER_LANES = 128
ROUTER_ROWS = 32
VMEM_LIMIT_BYTES = 60000 * 1024
ROW_BLOCK = 64
CHUNK_ROWS = 512
HP_TAIL_CHUNKS = 1
MOE_UNIT = 32
MOE_TILE_UNITS = CHUNK_ROWS // MOE_UNIT + N_GROUPS
MOE_TILE_SLOTS = MOE_TILE_UNITS * MOE_UNIT
MOE_MTILE_UNITS = 16
MOE_ROW_W = D_MODEL + ROUTER_LANES


def _rms(x, g):
    return x * lax.rsqrt(jnp.mean(x * x, axis=-1, keepdims=True) + EPS) * g


def _softplus(z):
    return jnp.maximum(z, 0.0) + jnp.log1p(jnp.exp(-jnp.abs(z)))


def _sigmoid(z):
    return 0.5 * jnp.tanh(0.5 * z) + 0.5


def _row_blocks(rows, rb, fn, unroll=False):
    if unroll:
        for i in range(rows // rb):
            fn(i * rb)
        return

    def body(i, carry):
        fn(pl.multiple_of(i * rb, rb))
        return carry
    lax.fori_loop(0, rows // rb, body, 0)


def _s5_prep_kernel(are_ref, aim_ref, ldt_ref, bre_ref, bim_ref,
                    abre_ref, abim_ref, bbre_ref, bbim_ref):
    a_re = are_ref[...]
    a_im = aim_ref[...]
    dt = jnp.exp(ldt_ref[...])
    mag = jnp.exp(a_re * dt)
    ab_re = mag * jnp.cos(a_im * dt)
    ab_im = mag * jnp.sin(a_im * dt)
    den = a_re * a_re + a_im * a_im
    nr = ab_re - 1.0
    ni = ab_im
    k_re = (nr * a_re + ni * a_im) / den
    k_im = (ni * a_re - nr * a_im) / den
    abre_ref[...] = ab_re
    abim_ref[...] = ab_im
    b_re = bre_ref[...]
    b_im = bim_ref[...]
    bbre_ref[...] = k_re[:, None, :] * b_re - k_im[:, None, :] * b_im
    bbim_ref[...] = k_re[:, None, :] * b_im + k_im[:, None, :] * b_re


def _s5_prep(a_re, a_im, log_dt, b_re_ghp, b_im_ghp):
    L = a_re.shape[0]
    gp = pl.BlockSpec((None, S5_GROUPS, S5_STATE), lambda l: (l, 0, 0))
    ghp = pl.BlockSpec((None, S5_GROUPS, S5_GROUP_CH, S5_STATE), lambda l: (l, 0, 0, 0))
    return pl.pallas_call(
        _s5_prep_kernel,
        grid=(L,),
        in_specs=[gp, gp, pl.BlockSpec((None, S5_GROUPS, 1), lambda l: (l, 0, 0)), ghp, ghp],
        out_specs=[gp, gp, ghp, ghp],
        out_shape=[jax.ShapeDtypeStruct((L, S5_GROUPS, S5_STATE), F32)] * 2
        + [jax.ShapeDtypeStruct((L, S5_GROUPS, S5_GROUP_CH, S5_STATE), F32)] * 2,
        name="s5_prep",
    )(a_re, a_im, log_dt[..., None], b_re_ghp, b_im_ghp)


def _mixer_kernel(bt, tc, n_chunks, x_bm, lo,
                  x_ref, h0_ref, rb_ref, fb_ref, sre_ref, sim_ref,
                  n1_ref, win_ref, rgw_ref, rgb_ref, wg_ref, bg_ref, lam_ref,
                  cfw_ref, cfb_ref, lng_ref, lnb_ref,
                  abre_ref, abim_ref, bbre_ref, bbim_ref, cre_ref, cim_ref, d_ref,
                  wglu_ref, bglu_ref, gnrg_ref, gncf_ref, gns5_ref, wout_ref,
                  x1_ref, hl_ref, rbn_ref, fbn_ref, sren_ref, simn_ref,
                  xpad, upad, h_sc, sre_sc, sim_sc, hn_sc, proj_sc, xc_sc, gate_sc,
                  a_sc, b_sc, v_sc, y_sc, zb_sc, gl_sc, bre_sc, bim_sc, mix_sc, *maybe_xtm):
    rows = tc * bt
    rb = ROW_BLOCK
    rg_hist = (RG_CONV - 1) * bt
    cf_hist = (CF_CONV - 1) * bt
    c = pl.program_id(0)
    act_dt = hn_sc.dtype

    def mm(a, w_ref, name):
        if lo is None:
            return jnp.dot(a.astype(BF16), w_ref[...], preferred_element_type=F32)
        a_hi = a.astype(BF16)
        a_lo = (a - a_hi.astype(F32)).astype(BF16)
        w_hi = w_ref[...]
        return (jnp.dot(a_hi, w_hi, preferred_element_type=F32)
                + jnp.dot(a_lo, w_hi, preferred_element_type=F32)
                + jnp.dot(a_hi, lo[name][...], preferred_element_type=F32))

    @pl.when(c == 0)
    def _init():
        xpad[0:rg_hist, :] = rb_ref[...]
        upad[0:cf_hist, :] = fb_ref[...]
        h_sc[...] = h0_ref[...]
        sre_sc[...] = sre_ref[...]
        sim_sc[...] = sim_ref[...]

    o_cf = 2 * D_RNN
    o_s5 = o_cf + 2 * CF_W

    if x_bm:
        (x_src,) = maybe_xtm
        for t in range(tc):
            x_src[t * bt:(t + 1) * bt, :] = x_ref[:, t, :]
    else:
        x_src = x_ref

    def norm_in(r0):
        x = x_src[r0:r0 + rb, :]
        hn_sc[r0:r0 + rb, :] = _rms(x, n1_ref[...]).astype(act_dt)
    _row_blocks(rows, rb, norm_in, unroll=True)
    proj_sc[...] = mm(hn_sc[...], win_ref, "w_in")

    xpad[rg_hist:rg_hist + rows, :] = proj_sc[:, 0:D_RNN]

    def rg_conv(r0):
        acc = rgb_ref[...] + rgw_ref[0:1, :] * xpad[r0:r0 + rb, :]
        for k in range(1, RG_CONV):
            acc = acc + rgw_ref[k:k + 1, :] * xpad[r0 + k * bt:r0 + k * bt + rb, :]
        xc_sc[r0:r0 + rb, :] = acc
    _row_blocks(rows, rb, rg_conv, unroll=True)

    gate_sc[...] = mm(xc_sc[...], wg_ref, "rg_wg") + bg_ref[...]

    su_all = proj_sc[:, o_s5:o_s5 + S5_W].astype(act_dt)
    bre_sc[...] = mm(su_all, bbre_ref, "s5_bb_re")
    bim_sc[...] = mm(su_all, bbim_ref, "s5_bb_im")

    neg_c_sp = -RG_C * _softplus(-lam_ref[...])

    def rg_coeffs(r0):
        g = gate_sc[r0:r0 + rb, :]
        r = _sigmoid(g[:, 0:D_RNN])
        i = _sigmoid(g[:, D_RNN:2 * D_RNN])
        a = jnp.exp(neg_c_sp * r)
        a_sc[r0:r0 + rb, :] = a
        b_sc[r0:r0 + rb, :] = jnp.sqrt(1.0 - a * a) * (i * xc_sc[r0:r0 + rb, :])
    _row_blocks(rows, rb, rg_coeffs, unroll=True)

    def cf_glu(r0):
        ca = proj_sc[r0:r0 + rb, o_cf:o_cf + CF_W]
        cg = proj_sc[r0:r0 + rb, o_cf + CF_W:o_cf + 2 * CF_W]
        upad[cf_hist + r0:cf_hist + r0 + rb, :] = ca * _sigmoid(cg)
    _row_blocks(rows, rb, cf_glu, unroll=True)

    h = h_sc[...]
    for t in range(tc):
        r0 = t * bt
        h = a_sc[r0:r0 + bt, :] * h + b_sc[r0:r0 + bt, :]
        b_sc[r0:r0 + bt, :] = h
    h_sc[...] = h
    hl_ref[...] = h

    a_r = jnp.broadcast_to(abre_ref[...], (bt, S5_N))
    a_i = jnp.broadcast_to(abim_ref[...], (bt, S5_N))
    s_r = sre_sc[...]
    s_i = sim_sc[...]
    for t in range(tc):
        r0 = t * bt
        n_r = a_r * s_r - a_i * s_i + bre_sc[r0:r0 + bt, :]
        n_i = a_r * s_i + a_i * s_r + bim_sc[r0:r0 + bt, :]
        bre_sc[r0:r0 + bt, :] = n_r
        bim_sc[r0:r0 + bt, :] = n_i
        s_r, s_i = n_r, n_i
    sre_sc[...] = s_r
    sim_sc[...] = s_i
    sren_ref[...] = s_r
    simn_ref[...] = s_i

    def cf_conv(r0):
        acc = cfb_ref[...] + cfw_ref[0:1, :] * upad[r0:r0 + rb, :]
        for k in range(1, CF_CONV):
            acc = acc + cfw_ref[k:k + 1, :] * upad[r0 + k * bt:r0 + k * bt + rb, :]
        v_sc[r0:r0 + rb, :] = acc
    _row_blocks(rows, rb, cf_conv, unroll=True)

    def rg_out(r0):
        h = b_sc[r0:r0 + rb, :]
        yb = proj_sc[r0:r0 + rb, D_RNN:2 * D_RNN]
        mix_sc[r0:r0 + rb, 0:D_RNN] = _rms(h * jax.nn.gelu(yb), gnrg_ref[...]).astype(act_dt)
    _row_blocks(rows, rb, rg_out, unroll=True)

    def cf_out(r0):
        v = v_sc[r0:r0 + rb, :]
        mu = jnp.mean(v, axis=-1, keepdims=True)
        cen = v - mu
        var = jnp.mean(cen * cen, axis=-1, keepdims=True)
        vn = cen * lax.rsqrt(var + EPS) * lng_ref[...] + lnb_ref[...]
        mix_sc[r0:r0 + rb, D_RNN:D_RNN + CF_W] = _rms(vn * _sigmoid(vn), gncf_ref[...]).astype(act_dt)
    _row_blocks(rows, rb, cf_out, unroll=True)

    y_sc[...] = mm(bre_sc[...], cre_ref, "s5_c_re") - mm(bim_sc[...], cim_ref, "s5_c_im")

    def s5_gelu(r0):
        su = proj_sc[r0:r0 + rb, o_s5:o_s5 + S5_W]
        z = jax.nn.gelu(y_sc[r0:r0 + rb, :] + d_ref[...] * su)
        y_sc[r0:r0 + rb, :] = z
        zb_sc[r0:r0 + rb, :] = z.astype(act_dt)
    _row_blocks(rows, rb, s5_gelu, unroll=True)
    gl_sc[...] = mm(zb_sc[...], wglu_ref, "s5_w_glu") + bglu_ref[...]

    def s5_out(r0):
        oc = y_sc[r0:r0 + rb, :] * _sigmoid(gl_sc[r0:r0 + rb, :])
        mix_sc[r0:r0 + rb, D_RNN + CF_W:D_MODEL] = _rms(oc, gns5_ref[...]).astype(act_dt)
    _row_blocks(rows, rb, s5_out, unroll=True)

    x1_ref[...] = x_src[...] + mm(mix_sc[...], wout_ref, "w_out")

    rbn_ref[...] = xpad[rows:rows + rg_hist, :]
    fbn_ref[...] = upad[rows:rows + cf_hist, :]
    if n_chunks > 1:
        xpad[0:rg_hist, :] = xpad[rows:rows + rg_hist, :]
        upad[0:cf_hist, :] = upad[rows:rows + cf_hist, :]


def _const_spec(shape):
    nd = len(shape)
    return pl.BlockSpec(shape, lambda *_: (0,) * nd)


def _layer_spec(a, l):
    nd = a.ndim - 1
    return pl.BlockSpec((None,) + a.shape[1:], lambda *_: (l,) + (0,) * nd)


MATMUL_WEIGHTS = ("w_in", "rg_wg", "s5_bb_re", "s5_bb_im", "s5_c_re", "s5_c_im", "s5_w_glu", "w_out")


def _mixer(x, bt, tc, n_chunks, state, w, l, x_blk0=0, out_rows=None, out=None, out_blk0=0,
           w_lo=None):
    x_bm = x.ndim == 3
    rows = tc * bt
    assert rows % ROW_BLOCK == 0 and bt % 8 == 0
    assert n_chunks == 1 or tc >= CF_CONV - 1
    h0, rbuf, fbuf, s_re, s_im = state
    weights = [w["norm1"], w["w_in"], w["rg_conv_w"], w["rg_conv_b"], w["rg_wg"], w["rg_bg"],
               w["rg_lambda"], w["cf_conv_w"], w["cf_conv_b"], w["cf_ln_g"], w["cf_ln_b"],
               w["s5_ab_re"], w["s5_ab_im"], w["s5_bb_re"], w["s5_bb_im"], w["s5_c_re"], w["s5_c_im"],
               w["s5_d"], w["s5_w_glu"], w["s5_b_glu"], w["gn_rg"], w["gn_cf"], w["gn_s5"], w["w_out"]]
    states = [h0, rbuf, fbuf, s_re, s_im]
    last = n_chunks - 1
    if x_bm:
        x_spec = pl.BlockSpec((bt, tc, D_MODEL), lambda c: (0, x_blk0 + jnp.minimum(c, last), 0))
    else:
        x_spec = pl.BlockSpec((rows, D_MODEL), lambda c: (x_blk0 + jnp.minimum(c, last), 0))
    in_specs = ([x_spec] + [_const_spec(a.shape) for a in states]
                + [_const_spec(a.shape) if a.ndim == 2 else _layer_spec(a, l) for a in weights])
    row_params = [1 + len(states) + i for i, a in enumerate(weights) if a.ndim == 2]
    operands = [x, *states, *weights]
    n_lo = 0
    if w_lo is not None:
        n_lo = len(MATMUL_WEIGHTS)
        in_specs += [_layer_spec(w_lo[k], l) for k in MATMUL_WEIGHTS]
        operands += [w_lo[k] for k in MATMUL_WEIGHTS]
    act_dt = BF16 if w_lo is None else F32
    aliases = {}
    if out is not None:
        in_specs.append(pl.BlockSpec(memory_space=pl.ANY))
        aliases = {len(operands): 0}
        operands.append(out)
        out_rows = out.shape[0]
    n_fill = 0 if out is not None else out_rows // rows - n_chunks
    n_in = len(operands)
    out_shape = [jax.ShapeDtypeStruct((out_rows, D_MODEL), F32)] + [
        jax.ShapeDtypeStruct(a.shape, F32) for a in states]
    out_specs = ([pl.BlockSpec((rows, D_MODEL), lambda c: (out_blk0 + c, 0))]
                 + [_const_spec(a.shape) for a in states])

    def body(*refs):
        refs = list(refs)
        for i in row_params:
            refs[i] = refs[i].at[pl.ds(l, 1)]
        n_w = n_in - len(aliases) - n_lo
        lo = dict(zip(MATMUL_WEIGHTS, refs[n_w:n_w + n_lo])) if n_lo else None
        keep = refs[:n_w] + refs[n_in:]
        if not n_fill:
            _mixer_kernel(bt, tc, n_chunks, x_bm, lo, *keep)
            return
        c = pl.program_id(0)
        x1_ref = keep[n_w]

        @pl.when(c < n_chunks)
        def _chunk():
            _mixer_kernel(bt, tc, n_chunks, x_bm, lo, *keep)

        @pl.when(c >= n_chunks)
        def _fill():
            x1_ref[...] = jnp.zeros_like(x1_ref)
    scratch = [
        pltpu.VMEM(((tc + RG_CONV - 1) * bt, D_RNN), F32),
        pltpu.VMEM(((tc + CF_CONV - 1) * bt, CF_W), F32),
        pltpu.VMEM((bt, D_RNN), F32),
        pltpu.VMEM((bt, S5_N), F32),
        pltpu.VMEM((bt, S5_N), F32),
        pltpu.VMEM((rows, D_MODEL), act_dt),
        pltpu.VMEM((rows, P_IN), F32),
        pltpu.VMEM((rows, D_RNN), F32),
        pltpu.VMEM((rows, 2 * D_RNN), F32),
        pltpu.VMEM((rows, D_RNN), F32),
        pltpu.VMEM((rows, D_RNN), F32),
        pltpu.VMEM((rows, CF_W), F32),
        pltpu.VMEM((rows, S5_W), F32),
        pltpu.VMEM((rows, S5_W), act_dt),
        pltpu.VMEM((rows, S5_W), F32),
        pltpu.VMEM((rows, S5_N), F32),
        pltpu.VMEM((rows, S5_N), F32),
        pltpu.VMEM((rows, D_MODEL), act_dt),
    ]
    if x_bm:
        scratch.append(pltpu.VMEM((rows, D_MODEL), F32))
    return pl.pallas_call(
        body,
        grid=(n_chunks + n_fill,),
        in_specs=in_specs,
        out_specs=out_specs,
        out_shape=out_shape,
        scratch_shapes=scratch,
        input_output_aliases=aliases,
        compiler_params=pltpu.CompilerParams(dimension_semantics=("arbitrary",),
                                             vmem_limit_bytes=VMEM_LIMIT_BYTES),
        name="mixer",
    )(*operands)


def _route(logits):
    shape = logits.shape
    sub = lax.broadcasted_iota(jnp.int32, shape, 0).astype(F32)
    far = float(shape[0])
    neg = jnp.float32(-jnp.inf)
    is_grp = sub < N_GROUPS
    gl = jnp.where(is_grp, logits, neg)
    gmax = jnp.max(gl, axis=0, keepdims=True)
    gidx = jnp.min(jnp.where(gl == gmax, sub, far), axis=0, keepdims=True)
    p_sel = 1.0 / jnp.sum(jnp.where(is_grp, jnp.exp(logits - gmax), 0.0), axis=0, keepdims=True)
    lo = N_GROUPS + EXP_PER_GROUP * gidx
    in_grp = jnp.logical_and(sub >= lo, sub < lo + EXP_PER_GROUP)
    e1 = jnp.where(in_grp, logits, neg)
    v1 = jnp.max(e1, axis=0, keepdims=True)
    i1 = jnp.min(jnp.where(e1 == v1, sub, far), axis=0, keepdims=True)
    e2 = jnp.where(sub == i1, neg, e1)
    v2 = jnp.max(e2, axis=0, keepdims=True)
    i2 = jnp.min(jnp.where(e2 == v2, sub, far), axis=0, keepdims=True)
    ex = jnp.exp(v2 - v1)
    w1 = 1.0 / (1.0 + ex)
    w2 = ex * w1
    sub8 = lax.broadcasted_iota(jnp.int32, (8, shape[1]), 0).astype(F32)
    gate4 = jnp.where(sub8 == i1 - lo, p_sel * w1, jnp.where(sub8 == i2 - lo, p_sel * w2, 0.0))
    return gidx, gate4


def _moe_route_kernel(x_ref, n2_ref, wr_ref, br_ref, ts_ref, slot_ref, cnt_ref, tri_sc):
    rows = x_ref.shape[0]
    x = x_ref[...]
    t = _rms(x, n2_ref[...])
    t_hi = t.astype(BF16)
    t_lo = (t - t_hi.astype(F32)).astype(BF16)
    wr = wr_ref[...]
    wr_hi = wr.astype(BF16)
    wr_lo = (wr - wr_hi.astype(F32)).astype(BF16)
    nt_dims = (((1,), (1,)), ((), ()))
    logits = (lax.dot_general(wr_hi, t_hi, nt_dims, preferred_element_type=F32)
              + lax.dot_general(wr_hi, t_lo, nt_dims, preferred_element_type=F32)
              + lax.dot_general(wr_lo, t_hi, nt_dims, preferred_element_type=F32)) + br_ref[:, 0:1]
    gidx, gate4 = _route(logits)

    sub8 = lax.broadcasted_iota(jnp.int32, (8, rows), 0).astype(F32)
    onehot = jnp.where(sub8 == gidx, 1.0, 0.0)

    @pl.when(pl.program_id(0) == 0)
    def _build_tri():
        r_i = lax.broadcasted_iota(jnp.int32, (rows, rows), 0)
        c_i = lax.broadcasted_iota(jnp.int32, (rows, rows), 1)
        tri_sc[...] = jnp.where(r_i < c_i, 1.0, 0.0).astype(BF16)
    before = jnp.dot(onehot.astype(BF16), tri_sc[...], preferred_element_type=F32)
    rank = jnp.sum(before * onehot, axis=0, keepdims=True)
    counts = jnp.broadcast_to(jnp.sum(onehot, axis=1, keepdims=True), (8, ROUTER_LANES))

    units = jnp.ceil(counts * (1.0 / MOE_UNIT))
    first = MOE_UNIT * (pltpu.roll(units, 1, 0) + pltpu.roll(units, 2, 0) + pltpu.roll(units, 3, 0))
    slot_t = jnp.sum(onehot * first[:, 0:1], axis=0, keepdims=True) + rank
    s_sub = lax.broadcasted_iota(jnp.int32, (MOE_TILE_SLOTS, rows), 0).astype(F32)
    perm = jnp.where(s_sub == slot_t, 1.0, 0.0).astype(BF16)

    ts_ref[:, 0:D_MODEL] = jnp.dot(perm, t_hi, preferred_element_type=F32).astype(BF16)
    g_hi = gate4.astype(BF16).astype(F32)
    slot_rows = jnp.where(sub8 == 0.0, slot_t, 0.0)
    cols = jnp.concatenate(
        [g_hi, slot_rows, jnp.zeros((ROUTER_LANES // 2 - 16, rows), F32), gate4 - g_hi,
         jnp.zeros((ROUTER_LANES // 2 - 8, rows), F32)], axis=0)
    by_token = jnp.transpose(cols)
    lane = lax.broadcasted_iota(jnp.int32, (rows, ROUTER_LANES), 1)
    packed = jnp.where(lane == 8, 0.0, by_token).astype(BF16)
    ts_ref[:, D_MODEL:MOE_ROW_W] = jnp.dot(perm, packed, preferred_element_type=F32).astype(BF16)
    slot_ref[...] = jnp.broadcast_to(by_token[:, 8:9], (rows, ROUTER_LANES))
    cnt_ref[...] = counts


def _moe_expert_kernel(src_ref, grp_ref, nt_ref, *refs):
    units = refs[0:MOE_MTILE_UNITS]
    wg_ref, wu_ref, wd_ref, out_ref, wg_sc, wu_sc, wd_sc = refs[MOE_MTILE_UNITS:]
    del src_ref
    j = pl.program_id(0)
    active = j < nt_ref[0]

    @pl.when(jnp.logical_and(active, jnp.logical_or(j == 0, grp_ref[j] != grp_ref[jnp.maximum(j - 1, 0)])))
    def _cast_weights():
        for e in range(EXP_PER_GROUP):
            wg_sc[e] = wg_ref[e].astype(BF16)
            wu_sc[e] = wu_ref[e].astype(BF16)
            wd_sc[e] = wd_ref[e].astype(BF16)

    @pl.when(active)
    def _compute():
        tsg = jnp.concatenate([u[...] for u in units], axis=0)
        ts = tsg[:, 0:D_MODEL]
        gp = tsg[:, D_MODEL:MOE_ROW_W].astype(F32)
        gs = gp + pltpu.roll(gp, ROUTER_LANES // 2, 1)
        acc = None
        for e in range(EXP_PER_GROUP):
            hg = jnp.dot(ts, wg_sc[e], preferred_element_type=F32)
            hu = jnp.dot(ts, wu_sc[e], preferred_element_type=F32)
            act = (hg * _sigmoid(hg) * hu * gs[:, e:e + 1]).astype(BF16)
            o = jnp.dot(act, wd_sc[e], preferred_element_type=F32)
            acc = o if acc is None else acc + o
        out_ref[...] = acc.astype(BF16)

    @pl.when(jnp.logical_not(active))
    def _idle():
        out_ref[...] = jnp.zeros_like(out_ref)


def _moe_combine_kernel(final, bt_out, n_bm_tiles, pos_ref, x_ref, slot_ref, *refs):
    del pos_ref
    units = refs[0:MOE_TILE_UNITS]
    fn_ref = refs[MOE_TILE_UNITS]
    outs = refs[MOE_TILE_UNITS + 1:]
    rows = x_ref.shape[0]
    os_local = jnp.concatenate([u[...] for u in units], axis=0)
    s_lane = lax.broadcasted_iota(jnp.int32, (rows, MOE_TILE_SLOTS), 1).astype(F32)
    perm_t = jnp.where(s_lane == slot_ref[:, 0:1], 1.0, 0.0).astype(BF16)
    y = x_ref[...] + jnp.dot(perm_t, os_local, preferred_element_type=F32)
    if final:
        y = _rms(y, fn_ref[...])
    if bt_out:
        o_bm, o_rest = outs
        i = pl.program_id(0)

        @pl.when(i < n_bm_tiles)
        def _batch_major():
            for t in range(rows // bt_out):
                o_bm[:, t, :] = y[t * bt_out:(t + 1) * bt_out, :]

        @pl.when(i >= n_bm_tiles)
        def _rest():
            o_rest[...] = y
    else:
        (o_ref,) = outs
        o_ref[...] = y


def _moe_plan(counts, n_tiles, n_mtiles):
    i32 = jnp.int32
    g = jnp.arange(N_GROUPS, dtype=i32)
    c = jnp.arange(n_tiles, dtype=i32)
    earlier_g = (g[:, None] < g[None, :]).astype(i32)
    earlier_c = (c[:, None] < c[None, :]).astype(i32)
    units = (counts + (MOE_UNIT - 1)) // MOE_UNIT
    loc_off = jnp.sum(units[:, :, None] * earlier_g[None], axis=1)
    before_c = jnp.sum(units[:, None, :] * earlier_c[:, :, None], axis=0)
    grp_tiles = (jnp.sum(units, axis=0) + (MOE_MTILE_UNITS - 1)) // MOE_MTILE_UNITS
    tile_start = jnp.sum(grp_tiles[:, None] * earlier_g, axis=0)
    grp_end = tile_start + grp_tiles
    u = jnp.arange(MOE_TILE_UNITS, dtype=i32)
    g_of = jnp.sum((u[None, :, None] >= (loc_off + units)[:, None, :]).astype(i32), axis=-1)
    valid = g_of < N_GROUPS
    pick = (g_of[:, :, None] == g[None, None, :]).astype(i32)
    base = MOE_MTILE_UNITS * tile_start[None, :] + before_c - loc_off
    pos = jnp.sum(pick * base[:, None, :], axis=-1) + u[None, :]
    pos = jnp.where(valid, pos, 0).astype(i32)
    local_id = c[:, None] * MOE_TILE_UNITS + u[None, :]
    n_pos = n_mtiles * MOE_MTILE_UNITS
    p = jnp.arange(n_pos, dtype=i32)
    hit = jnp.logical_and(valid.reshape(-1)[None, :], pos.reshape(-1)[None, :] == p[:, None])
    src = jnp.sum(jnp.where(hit, local_id.reshape(-1)[None, :], 0), axis=-1).astype(i32)
    j = jnp.arange(n_mtiles, dtype=i32)
    grp = jnp.minimum(jnp.sum((j[:, None] >= grp_end[None, :]).astype(i32), axis=-1), N_GROUPS - 1)
    n_used = jnp.sum(grp_tiles, keepdims=True)
    return src, grp.astype(i32), n_used.astype(i32), pos.reshape(-1)


def _moe(x1, w, l, final, bt_out=0, n_bm_tiles=0):
    n_rows = x1.shape[0]
    rows = CHUNK_ROWS
    assert n_rows % rows == 0
    n_tiles = n_rows // rows
    n_mtiles = -(-(n_tiles * MOE_TILE_UNITS) // MOE_MTILE_UNITS) + N_GROUPS
    mrows = MOE_MTILE_UNITS * MOE_UNIT
    cparams = pltpu.CompilerParams(dimension_semantics=("arbitrary",), vmem_limit_bytes=VMEM_LIMIT_BYTES)

    row_spec = pl.BlockSpec((rows, D_MODEL), lambda i: (i, 0))
    def route_body(x_ref, n2_ref, *rest):
        _moe_route_kernel(x_ref, n2_ref.at[pl.ds(l, 1)], *rest)
    ts, slot, cnt = pl.pallas_call(
        route_body,
        grid=(n_tiles,),
        in_specs=[row_spec, _const_spec(w["norm2"].shape), _layer_spec(w["moe_w_r"], l),
                  _layer_spec(w["moe_b_r"], l)],
        out_specs=[pl.BlockSpec((MOE_TILE_SLOTS, MOE_ROW_W), lambda i: (i, 0)),
                   pl.BlockSpec((rows, ROUTER_LANES), lambda i: (i, 0)),
                   pl.BlockSpec((8, ROUTER_LANES), lambda i: (i, 0))],
        out_shape=[jax.ShapeDtypeStruct((n_tiles * MOE_TILE_SLOTS, MOE_ROW_W), BF16),
                   jax.ShapeDtypeStruct((n_rows, ROUTER_LANES), F32),
                   jax.ShapeDtypeStruct((n_tiles * 8, ROUTER_LANES), F32)],
        scratch_shapes=[pltpu.VMEM((rows, rows), BF16)],
        compiler_params=cparams,
        name="moe_route",
    )(x1, w["norm2"], w["moe_w_r"], w["moe_b_r"])

    counts = cnt.reshape(n_tiles, 8, ROUTER_LANES)[:, :N_GROUPS, 0].astype(jnp.int32)
    src, grp, n_used, pos = _moe_plan(counts, n_tiles, n_mtiles)

    def unit_in(k):
        return pl.BlockSpec((MOE_UNIT, MOE_ROW_W), lambda j, src, grp, nt: (src[j * MOE_MTILE_UNITS + k], 0))

    def grp_w(shape):
        return pl.BlockSpec((None, EXP_PER_GROUP) + shape, lambda j, src, grp, nt: (l, grp[j], 0, 0))
    os_sorted = pl.pallas_call(
        _moe_expert_kernel,
        grid_spec=pltpu.PrefetchScalarGridSpec(
            num_scalar_prefetch=3,
            grid=(n_mtiles,),
            in_specs=[unit_in(k) for k in range(MOE_MTILE_UNITS)]
            + [grp_w((D_MODEL, EXPERT_FF)), grp_w((D_MODEL, EXPERT_FF)), grp_w((EXPERT_FF, D_MODEL))],
            out_specs=pl.BlockSpec((mrows, D_MODEL), lambda j, src, grp, nt: (j, 0)),
            scratch_shapes=[pltpu.VMEM((EXP_PER_GROUP, D_MODEL, EXPERT_FF), BF16),
                            pltpu.VMEM((EXP_PER_GROUP, D_MODEL, EXPERT_FF), BF16),
                            pltpu.VMEM((EXP_PER_GROUP, EXPERT_FF, D_MODEL), BF16)],
        ),
        out_shape=jax.ShapeDtypeStruct((n_mtiles * mrows, D_MODEL), BF16),
        compiler_params=cparams,
        name="moe_experts",
    )(src, grp, n_used, *([ts] * MOE_MTILE_UNITS), w["moe_w_gate"], w["moe_w_up"], w["moe_w_down"])

    def unit_out(u):
        return pl.BlockSpec((MOE_UNIT, D_MODEL), lambda i, pos: (pos[i * MOE_TILE_UNITS + u], 0))
    if bt_out:
        tc = rows // bt_out
        last = n_bm_tiles - 1
        out_spec = [pl.BlockSpec((bt_out, tc, D_MODEL), lambda i, pos: (0, jnp.minimum(i, last), 0)),
                    pl.BlockSpec((rows, D_MODEL), lambda i, pos: (jnp.maximum(i - n_bm_tiles, 0), 0))]
        out_shape = [jax.ShapeDtypeStruct((bt_out, n_bm_tiles * tc, D_MODEL), F32),
                     jax.ShapeDtypeStruct(((n_tiles - n_bm_tiles) * rows, D_MODEL), F32)]
    else:
        out_spec = pl.BlockSpec((rows, D_MODEL), lambda i, pos: (i, 0))
        out_shape = jax.ShapeDtypeStruct((n_rows, D_MODEL), F32)
    return pl.pallas_call(
        functools.partial(_moe_combine_kernel, final, bt_out, n_bm_tiles),
        grid_spec=pltpu.PrefetchScalarGridSpec(
            num_scalar_prefetch=1,
            grid=(n_tiles,),
            in_specs=[pl.BlockSpec((rows, D_MODEL), lambda i, pos: (i, 0)),
                      pl.BlockSpec((rows, ROUTER_LANES), lambda i, pos: (i, 0))]
            + [unit_out(u) for u in range(MOE_TILE_UNITS)]
            + [pl.BlockSpec(w["final_norm"].shape, lambda i, pos: (0, 0))],
            out_specs=out_spec,
        ),
        out_shape=out_shape,
        compiler_params=cparams,
        name="moe_combine",
    )(pos, x1, slot, *([os_sorted] * MOE_TILE_UNITS), w["final_norm"])


def _block_diag(blocks):
    L, n, r, c = blocks.shape
    eye = jnp.eye(n, dtype=blocks.dtype)
    return (blocks[:, :, :, None, :] * eye[None, :, None, :, None]).reshape(L, n * r, n * c)


def _to_tm(a):
    a = jnp.swapaxes(a, 0, 1)
    return a.reshape((a.shape[0] * a.shape[1],) + a.shape[2:])


def _from_tm(a, b):
    return jnp.swapaxes(a.reshape((a.shape[0] // b, b) + a.shape[1:]), 0, 1)


def kernel(x_prompt, x_sample, state_rglru_h, state_rglru_conv, state_conf_conv, state_s5_re, state_s5_im, norm1, w_in, rg_conv_w, rg_conv_b, rg_wa, rg_ba, rg_wx, rg_bx, rg_lambda, cf_conv_w, cf_conv_b, cf_ln_g, cf_ln_b, s5_a_re, s5_a_im, s5_log_dt, s5_b_re, s5_b_im, s5_c_re, s5_c_im, s5_d, s5_w_glu, s5_b_glu, gn_rg, gn_cf, gn_s5, w_out, norm2, moe_w_grp, moe_b_grp, moe_w_exp, moe_b_exp, moe_w_gate, moe_w_up, moe_w_down, final_norm):
    L = w_in.shape[0]
    B, T, _ = x_prompt.shape
    DB, DT, _ = x_sample.shape

    ab_re, ab_im, bb_re, bb_im = _s5_prep(s5_a_re, s5_a_im, s5_log_dt,
                                          jnp.swapaxes(s5_b_re, 2, 3), jnp.swapaxes(s5_b_im, 2, 3))

    def row(a):
        return a.reshape(L, -1)

    big = dict(
        w_in=w_in, rg_wg=jnp.concatenate([_block_diag(rg_wa), _block_diag(rg_wx)], axis=-1),
        s5_bb_re=_block_diag(bb_re), s5_bb_im=_block_diag(bb_im),
        s5_c_re=_block_diag(jnp.swapaxes(s5_c_re, 2, 3)),
        s5_c_im=_block_diag(jnp.swapaxes(s5_c_im, 2, 3)),
        s5_w_glu=s5_w_glu, w_out=w_out)
    assert set(big) == set(MATMUL_WEIGHTS)
    W = {k: v.astype(BF16) for k, v in big.items()}
    W_lo = {k: (v[:1] - W[k][:1].astype(F32)).astype(BF16) for k, v in big.items()}
    W.update(
        norm1=row(norm1), rg_conv_w=rg_conv_w, rg_conv_b=row(rg_conv_b),
        rg_bg=jnp.concatenate([row(rg_ba), row(rg_bx)], axis=-1),
        rg_lambda=row(rg_lambda),
        cf_conv_w=cf_conv_w, cf_conv_b=row(cf_conv_b), cf_ln_g=row(cf_ln_g), cf_ln_b=row(cf_ln_b),
        s5_ab_re=row(ab_re), s5_ab_im=row(ab_im),
        s5_d=row(s5_d), s5_b_glu=row(s5_b_glu),
        gn_rg=row(gn_rg), gn_cf=row(gn_cf), gn_s5=row(gn_s5),
        norm2=row(norm2),
        moe_w_r=jnp.swapaxes(jnp.concatenate(
            [moe_w_grp, moe_w_exp.reshape(L, D_MODEL, N_EXPERTS),
             jnp.zeros((L, D_MODEL, ROUTER_ROWS - N_GROUPS - N_EXPERTS), F32)], axis=-1), 1, 2),
        moe_b_r=jnp.broadcast_to(jnp.concatenate(
            [moe_b_grp, moe_b_exp.reshape(L, N_EXPERTS),
             jnp.zeros((L, ROUTER_ROWS - N_GROUPS - N_EXPERTS), F32)], axis=-1)[:, :, None],
            (L, ROUTER_ROWS, ROUTER_LANES)),
        moe_w_gate=moe_w_gate, moe_w_up=moe_w_up, moe_w_down=moe_w_down,
    )

    tc_p = CHUNK_ROWS // B
    n_p = (B * T) // CHUNK_ROWS
    n_s = (DB * DT) // CHUNK_ROWS
    assert n_p * CHUNK_ROWS == B * T and n_s == 1 and DB * DT == CHUNK_ROWS
    all_rows = (n_p + n_s) * CHUNK_ROWS
    x_p, x_s, s_blk0 = x_prompt, _to_tm(x_sample), 0
    p_states, s_states = [], []
    W["final_norm"] = final_norm.reshape(1, D_MODEL)
    st_p = (jnp.zeros((B, D_RNN), F32), jnp.zeros(((RG_CONV - 1) * B, D_RNN), F32),
            jnp.zeros(((CF_CONV - 1) * B, CF_W), F32), jnp.zeros((B, S5_N), F32),
            jnp.zeros((B, S5_N), F32))
    for l in range(L):
        st_s = (state_rglru_h[l], _to_tm(state_rglru_conv[l]), _to_tm(state_conf_conv[l]),
                state_s5_re[l].reshape(DB, S5_N), state_s5_im[l].reshape(DB, S5_N))
        if l == 0:
            n_lp = n_p - HP_TAIL_CHUNKS
            x1, *st_mid = _mixer(x_p, B, tc_p, n_lp, st_p, W, l, out_rows=all_rows)
            x1, *np_l = _mixer(x_p, B, tc_p, HP_TAIL_CHUNKS, tuple(st_mid), W, l, x_blk0=n_lp,
                               out=x1, out_blk0=n_lp, w_lo=W_lo)
        else:
            x1, *np_l = _mixer(x_p, B, tc_p, n_p, st_p, W, l, out_rows=all_rows)
        x1, *ns_l = _mixer(x_s, DB, DT, n_s, st_s, W, l, x_blk0=s_blk0, out=x1, out_blk0=n_p)
        p_states.append(np_l)
        s_states.append(ns_l)
        if l < L - 1:
            x_p = x_s = _moe(x1, W, l, False)
            s_blk0 = n_p
        else:
            y_prompt, y_s = _moe(x1, W, l, True, bt_out=B, n_bm_tiles=n_p)

    def assemble(states, b):
        h, rc, cf, sre, sim = [jnp.stack([s[j] for s in states]) for j in range(5)]
        rc = jnp.swapaxes(rc.reshape(L, RG_CONV - 1, b, D_RNN), 1, 2)
        cf = jnp.swapaxes(cf.reshape(L, CF_CONV - 1, b, CF_W), 1, 2)
        return (h, rc, cf, sre.reshape(L, b, S5_GROUPS, S5_STATE),
                sim.reshape(L, b, S5_GROUPS, S5_STATE))

    y_sample = _from_tm(y_s, DB)
    return (y_prompt, y_sample) + assemble(p_states, B) + assemble(s_states, DB)
```

```python
import functools

import jax
import jax.numpy as jnp
from jax import lax
from jax.experimental import pallas as pl
from jax.experimental.pallas import tpu as pltpu

F32 = jnp.float32
BF16 = jnp.bfloat16

D_MODEL = 1024
D_RNN = 512
RG_HEADS = 8
RG_CONV = 4
RG_C = 8.0
CF_W = 256
CF_CONV = 31
S5_W = 256
S5_GROUPS = 16
S5_GROUP_CH = 16
S5_STATE = 64
S5_N = S5_GROUPS * S5_STATE
P_IN = 2 * D_RNN + 2 * CF_W + S5_W
N_GROUPS = 4
EXP_PER_GROUP = 4
N_EXPERTS = 16
EXPERT_FF = 256
EPS = 1e-6

ROUTER_LANES = 128
ROUTER_ROWS = 32
VMEM_LIMIT_BYTES = 60000 * 1024
ROW_BLOCK = 128
CHUNK_ROWS = 512
HP_TAIL_CHUNKS = 1
MOE_UNIT = 32
MOE_TILE_UNITS = CHUNK_ROWS // MOE_UNIT + N_GROUPS
MOE_TILE_SLOTS = MOE_TILE_UNITS * MOE_UNIT
MOE_MTILE_UNITS = 16
MOE_ROW_W = D_MODEL + ROUTER_LANES


def _rms(x, g):
    return x * lax.rsqrt(jnp.mean(x * x, axis=-1, keepdims=True) + EPS) * g


def _softplus(z):
    return jnp.maximum(z, 0.0) + jnp.log1p(jnp.exp(-jnp.abs(z)))


def _sigmoid(z):
    return 0.5 * jnp.tanh(0.5 * z) + 0.5


def _sigmoid_of_twice(zh):
    return 0.5 * jnp.tanh(zh) + 0.5


_GELU_C0 = 0.7978845608028654
_GELU_C1 = _GELU_C0 * 0.044715


def _gelu(x):
    hx = 0.5 * x
    return hx + hx * jnp.tanh(x * (_GELU_C0 + _GELU_C1 * (x * x)))


def _row_blocks(rows, rb, fn, unroll=False):
    if unroll:
        for i in range(rows // rb):
            fn(i * rb)
        return

    def body(i, carry):
        fn(pl.multiple_of(i * rb, rb))
        return carry
    lax.fori_loop(0, rows // rb, body, 0)


def _s5_prep_kernel(are_ref, aim_ref, ldt_ref, bre_ref, bim_ref,
                    abre_ref, abim_ref, bbre_ref, bbim_ref):
    a_re = are_ref[...]
    a_im = aim_ref[...]
    dt = jnp.exp(ldt_ref[...])
    mag = jnp.exp(a_re * dt)
    ab_re = mag * jnp.cos(a_im * dt)
    ab_im = mag * jnp.sin(a_im * dt)
    den = a_re * a_re + a_im * a_im
    nr = ab_re - 1.0
    ni = ab_im
    k_re = (nr * a_re + ni * a_im) / den
    k_im = (ni * a_re - nr * a_im) / den
    abre_ref[...] = ab_re
    abim_ref[...] = ab_im
    b_re = bre_ref[...]
    b_im = bim_ref[...]
    bbre_ref[...] = k_re[:, None, :] * b_re - k_im[:, None, :] * b_im
    bbim_ref[...] = k_re[:, None, :] * b_im + k_im[:, None, :] * b_re


def _s5_prep(a_re, a_im, log_dt, b_re_ghp, b_im_ghp):
    L = a_re.shape[0]
    gp = pl.BlockSpec((None, S5_GROUPS, S5_STATE), lambda l: (l, 0, 0))
    ghp = pl.BlockSpec((None, S5_GROUPS, S5_GROUP_CH, S5_STATE), lambda l: (l, 0, 0, 0))
    return pl.pallas_call(
        _s5_prep_kernel,
        grid=(L,),
        in_specs=[gp, gp, pl.BlockSpec((None, S5_GROUPS, 1), lambda l: (l, 0, 0)), ghp, ghp],
        out_specs=[gp, gp, ghp, ghp],
        out_shape=[jax.ShapeDtypeStruct((L, S5_GROUPS, S5_STATE), F32)] * 2
        + [jax.ShapeDtypeStruct((L, S5_GROUPS, S5_GROUP_CH, S5_STATE), F32)] * 2,
        name="s5_prep",
    )(a_re, a_im, log_dt[..., None], b_re_ghp, b_im_ghp)


def _mixer_kernel(bt, tc, n_chunks, x_bm, lo,
                  x_ref, h0_ref, rb_ref, fb_ref, sre_ref, sim_ref,
                  n1_ref, win_ref, rgw_ref, rgb_ref, wg_ref, bg_ref, lam_ref,
                  cfw_ref, cfb_ref, lng_ref, lnb_ref,
                  abre_ref, abim_ref, bbre_ref, bbim_ref, cre_ref, cim_ref, d_ref,
                  wglu_ref, bglu_ref, gnrg_ref, gncf_ref, gns5_ref, wout_ref,
                  x1_ref, hl_ref, rbn_ref, fbn_ref, sren_ref, simn_ref,
                  xpad, upad, h_sc, sre_sc, sim_sc, hn_sc, proj_sc, xc_sc, gate_sc,
                  a_sc, b_sc, v_sc, y_sc, zb_sc, gl_sc, bre_sc, bim_sc, mix_sc, *maybe_xtm):
    rows = tc * bt
    rb = ROW_BLOCK
    rg_hist = (RG_CONV - 1) * bt
    cf_hist = (CF_CONV - 1) * bt
    c = pl.program_id(0)
    act_dt = hn_sc.dtype

    def mm(a, w_ref, name):
        if lo is None:
            return jnp.dot(a.astype(BF16), w_ref[...], preferred_element_type=F32)
        a_hi = a.astype(BF16)
        a_lo = (a - a_hi.astype(F32)).astype(BF16)
        w_hi = w_ref[...]
        return (jnp.dot(a_hi, w_hi, preferred_element_type=F32)
                + jnp.dot(a_lo, w_hi, preferred_element_type=F32)
                + jnp.dot(a_hi, lo[name][...], preferred_element_type=F32))

    @pl.when(c == 0)
    def _init():
        xpad[0:rg_hist, :] = rb_ref[...]
        upad[0:cf_hist, :] = fb_ref[...]
        h_sc[...] = h0_ref[...]
        sre_sc[...] = sre_ref[...]
        sim_sc[...] = sim_ref[...]

    o_cf = 2 * D_RNN
    o_s5 = o_cf + 2 * CF_W

    if x_bm:
        (x_src,) = maybe_xtm
        for t in range(tc):
            x_src[t * bt:(t + 1) * bt, :] = x_ref[:, t, :]
    else:
        x_src = x_ref

    def norm_in(r0):
        x = x_src[r0:r0 + rb, :]
        hn_sc[r0:r0 + rb, :] = _rms(x, n1_ref[...]).astype(act_dt)
    _row_blocks(rows, rb, norm_in, unroll=True)
    proj_sc[...] = mm(hn_sc[...], win_ref, "w_in")

    xpad[rg_hist:rg_hist + rows, :] = proj_sc[:, 0:D_RNN]

    def rg_conv(r0):
        acc = rgb_ref[...] + rgw_ref[0:1, :] * xpad[r0:r0 + rb, :]
        for k in range(1, RG_CONV):
            acc = acc + rgw_ref[k:k + 1, :] * xpad[r0 + k * bt:r0 + k * bt + rb, :]
        xc_sc[r0:r0 + rb, :] = acc
    _row_blocks(rows, rb, rg_conv, unroll=True)

    gate_sc[...] = mm(xc_sc[...], wg_ref, "rg_wg") + bg_ref[...]

    su_all = proj_sc[:, o_s5:o_s5 + S5_W].astype(act_dt)
    bre_sc[...] = mm(su_all, bbre_ref, "s5_bb_re")
    bim_sc[...] = mm(su_all, bbim_ref, "s5_bb_im")

    neg_c_sp = -RG_C * _softplus(-lam_ref[...])

    def rg_coeffs(r0):
        g = gate_sc[r0:r0 + rb, :]
        r = _sigmoid_of_twice(g[:, 0:D_RNN])
        i = _sigmoid_of_twice(g[:, D_RNN:2 * D_RNN])
        a = jnp.exp(neg_c_sp * r)
        a_sc[r0:r0 + rb, :] = a
        b_sc[r0:r0 + rb, :] = jnp.sqrt(1.0 - a * a) * (i * xc_sc[r0:r0 + rb, :])
    _row_blocks(rows, rb, rg_coeffs, unroll=True)

    def cf_glu(r0):
        ca = proj_sc[r0:r0 + rb, o_cf:o_cf + CF_W]
        cg = proj_sc[r0:r0 + rb, o_cf + CF_W:o_cf + 2 * CF_W]
        upad[cf_hist + r0:cf_hist + r0 + rb, :] = ca * _sigmoid_of_twice(cg)
    _row_blocks(rows, rb, cf_glu, unroll=True)

    h = h_sc[...]
    for t in range(tc):
        r0 = t * bt
        h = a_sc[r0:r0 + bt, :] * h + b_sc[r0:r0 + bt, :]
        b_sc[r0:r0 + bt, :] = h
    h_sc[...] = h
    hl_ref[...] = h

    a_r = jnp.broadcast_to(abre_ref[...], (bt, S5_N))
    a_i = jnp.broadcast_to(abim_ref[...], (bt, S5_N))
    s_r = sre_sc[...]
    s_i = sim_sc[...]
    for t in range(tc):
        r0 = t * bt
        n_r = a_r * s_r - a_i * s_i + bre_sc[r0:r0 + bt, :]
        n_i = a_r * s_i + a_i * s_r + bim_sc[r0:r0 + bt, :]
        bre_sc[r0:r0 + bt, :] = n_r
        bim_sc[r0:r0 + bt, :] = n_i
        s_r, s_i = n_r, n_i
    sre_sc[...] = s_r
    sim_sc[...] = s_i
    sren_ref[...] = s_r
    simn_ref[...] = s_i

    def cf_conv(r0):
        acc = cfb_ref[...] + cfw_ref[0:1, :] * upad[r0:r0 + rb, :]
        for k in range(1, CF_CONV):
            acc = acc + cfw_ref[k:k + 1, :] * upad[r0 + k * bt:r0 + k * bt + rb, :]
        v_sc[r0:r0 + rb, :] = acc
    _row_blocks(rows, rb, cf_conv, unroll=True)

    def rg_out(r0):
        h = b_sc[r0:r0 + rb, :]
        yb = proj_sc[r0:r0 + rb, D_RNN:2 * D_RNN]
        mix_sc[r0:r0 + rb, 0:D_RNN] = _rms(h * _gelu(yb), gnrg_ref[...]).astype(act_dt)
    _row_blocks(rows, rb, rg_out, unroll=True)

    def cf_out(r0):
        v = v_sc[r0:r0 + rb, :]
        mu = jnp.mean(v, axis=-1, keepdims=True)
        cen = v - mu
        var = jnp.mean(cen * cen, axis=-1, keepdims=True)
        vn = cen * lax.rsqrt(var + EPS) * lng_ref[...] + lnb_ref[...]
        mix_sc[r0:r0 + rb, D_RNN:D_RNN + CF_W] = _rms(vn * _sigmoid(vn), gncf_ref[...]).astype(act_dt)
    _row_blocks(rows, rb, cf_out, unroll=True)

    y_sc[...] = mm(bre_sc[...], cre_ref, "s5_c_re") - mm(bim_sc[...], cim_ref, "s5_c_im")

    def s5_gelu(r0):
        su = proj_sc[r0:r0 + rb, o_s5:o_s5 + S5_W]
        z = _gelu(y_sc[r0:r0 + rb, :] + d_ref[...] * su)
        y_sc[r0:r0 + rb, :] = z
        zb_sc[r0:r0 + rb, :] = z.astype(act_dt)
    _row_blocks(rows, rb, s5_gelu, unroll=True)
    gl_sc[...] = mm(zb_sc[...], wglu_ref, "s5_w_glu") + bglu_ref[...]

    def s5_out(r0):
        oc = y_sc[r0:r0 + rb, :] * _sigmoid_of_twice(gl_sc[r0:r0 + rb, :])
        mix_sc[r0:r0 + rb, D_RNN + CF_W:D_MODEL] = _rms(oc, gns5_ref[...]).astype(act_dt)
    _row_blocks(rows, rb, s5_out, unroll=True)

    x1_ref[...] = x_src[...] + mm(mix_sc[...], wout_ref, "w_out")

    rbn_ref[...] = xpad[rows:rows + rg_hist, :]
    fbn_ref[...] = upad[rows:rows + cf_hist, :]
    if n_chunks > 1:
        xpad[0:rg_hist, :] = xpad[rows:rows + rg_hist, :]
        upad[0:cf_hist, :] = upad[rows:rows + cf_hist, :]


def _const_spec(shape):
    nd = len(shape)
    return pl.BlockSpec(shape, lambda *_: (0,) * nd)


def _layer_spec(a, l):
    nd = a.ndim - 1
    return pl.BlockSpec((None,) + a.shape[1:], lambda *_: (l,) + (0,) * nd)


MATMUL_WEIGHTS = ("w_in", "rg_wg", "s5_bb_re", "s5_bb_im", "s5_c_re", "s5_c_im", "s5_w_glu", "w_out")


def _mixer(x, bt, tc, n_chunks, state, w, l, x_blk0=0, out_rows=None, out=None, out_blk0=0,
           w_lo=None):
    x_bm = x.ndim == 3
    rows = tc * bt
    assert rows % ROW_BLOCK == 0 and bt % 8 == 0
    assert n_chunks == 1 or tc >= CF_CONV - 1
    h0, rbuf, fbuf, s_re, s_im = state
    weights = [w["norm1"], w["w_in"], w["rg_conv_w"], w["rg_conv_b"], w["rg_wg"], w["rg_bg"],
               w["rg_lambda"], w["cf_conv_w"], w["cf_conv_b"], w["cf_ln_g"], w["cf_ln_b"],
               w["s5_ab_re"], w["s5_ab_im"], w["s5_bb_re"], w["s5_bb_im"], w["s5_c_re"], w["s5_c_im"],
               w["s5_d"], w["s5_w_glu"], w["s5_b_glu"], w["gn_rg"], w["gn_cf"], w["gn_s5"], w["w_out"]]
    states = [h0, rbuf, fbuf, s_re, s_im]
    last = n_chunks - 1
    if x_bm:
        x_spec = pl.BlockSpec((bt, tc, D_MODEL), lambda c: (0, x_blk0 + jnp.minimum(c, last), 0))
    else:
        x_spec = pl.BlockSpec((rows, D_MODEL), lambda c: (x_blk0 + jnp.minimum(c, last), 0))
    in_specs = ([x_spec] + [_const_spec(a.shape) for a in states]
                + [_const_spec(a.shape) if a.ndim == 2 else _layer_spec(a, l) for a in weights])
    row_params = [1 + len(states) + i for i, a in enumerate(weights) if a.ndim == 2]
    operands = [x, *states, *weights]
    n_lo = 0
    if w_lo is not None:
        n_lo = len(MATMUL_WEIGHTS)
        in_specs += [_layer_spec(w_lo[k], l) for k in MATMUL_WEIGHTS]
        operands += [w_lo[k] for k in MATMUL_WEIGHTS]
    act_dt = BF16 if w_lo is None else F32
    aliases = {}
    if out is not None:
        in_specs.append(pl.BlockSpec(memory_space=pl.ANY))
        aliases = {len(operands): 0}
        operands.append(out)
        out_rows = out.shape[0]
    n_fill = 0 if out is not None else out_rows // rows - n_chunks
    n_in = len(operands)
    out_shape = [jax.ShapeDtypeStruct((out_rows, D_MODEL), F32)] + [
        jax.ShapeDtypeStruct(a.shape, F32) for a in states]
    out_specs = ([pl.BlockSpec((rows, D_MODEL), lambda c: (out_blk0 + c, 0))]
                 + [_const_spec(a.shape) for a in states])

    def body(*refs):
        refs = list(refs)
        for i in row_params:
            refs[i] = refs[i].at[pl.ds(l, 1)]
        n_w = n_in - len(aliases) - n_lo
        lo = dict(zip(MATMUL_WEIGHTS, refs[n_w:n_w + n_lo])) if n_lo else None
        keep = refs[:n_w] + refs[n_in:]
        if not n_fill:
            _mixer_kernel(bt, tc, n_chunks, x_bm, lo, *keep)
            return
        c = pl.program_id(0)
        x1_ref = keep[n_w]

        @pl.when(c < n_chunks)
        def _chunk():
            _mixer_kernel(bt, tc, n_chunks, x_bm, lo, *keep)

        @pl.when(c >= n_chunks)
        def _fill():
            x1_ref[...] = jnp.zeros_like(x1_ref)
    scratch = [
        pltpu.VMEM(((tc + RG_CONV - 1) * bt, D_RNN), F32),
        pltpu.VMEM(((tc + CF_CONV - 1) * bt, CF_W), F32),
        pltpu.VMEM((bt, D_RNN), F32),
        pltpu.VMEM((bt, S5_N), F32),
        pltpu.VMEM((bt, S5_N), F32),
        pltpu.VMEM((rows, D_MODEL), act_dt),
        pltpu.VMEM((rows, P_IN), F32),
        pltpu.VMEM((rows, D_RNN), F32),
        pltpu.VMEM((rows, 2 * D_RNN), F32),
        pltpu.VMEM((rows, D_RNN), F32),
        pltpu.VMEM((rows, D_RNN), F32),
        pltpu.VMEM((rows, CF_W), F32),
        pltpu.VMEM((rows, S5_W), F32),
        pltpu.VMEM((rows, S5_W), act_dt),
        pltpu.VMEM((rows, S5_W), F32),
        pltpu.VMEM((rows, S5_N), F32),
        pltpu.VMEM((rows, S5_N), F32),
        pltpu.VMEM((rows, D_MODEL), act_dt),
    ]
    if x_bm:
        scratch.append(pltpu.VMEM((rows, D_MODEL), F32))
    return pl.pallas_call(
        body,
        grid=(n_chunks + n_fill,),
        in_specs=in_specs,
        out_specs=out_specs,
        out_shape=out_shape,
        scratch_shapes=scratch,
        input_output_aliases=aliases,
        compiler_params=pltpu.CompilerParams(dimension_semantics=("arbitrary",),
                                             vmem_limit_bytes=VMEM_LIMIT_BYTES),
        name="mixer",
    )(*operands)


def _route(logits):
    shape = logits.shape
    sub = lax.broadcasted_iota(jnp.int32, shape, 0).astype(F32)
    far = float(shape[0])
    neg = jnp.float32(-jnp.inf)
    is_grp = sub < N_GROUPS
    gl = jnp.where(is_grp, logits, neg)
    gmax = jnp.max(gl, axis=0, keepdims=True)
    gidx = jnp.min(jnp.where(gl == gmax, sub, far), axis=0, keepdims=True)
    p_sel = 1.0 / jnp.sum(jnp.where(is_grp, jnp.exp(logits - gmax), 0.0), axis=0, keepdims=True)
    lo = N_GROUPS + EXP_PER_GROUP * gidx
    in_grp = jnp.logical_and(sub >= lo, sub < lo + EXP_PER_GROUP)
    e1 = jnp.where(in_grp, logits, neg)
    v1 = jnp.max(e1, axis=0, keepdims=True)
    i1 = jnp.min(jnp.where(e1 == v1, sub, far), axis=0, keepdims=True)
    e2 = jnp.where(sub == i1, neg, e1)
    v2 = jnp.max(e2, axis=0, keepdims=True)
    i2 = jnp.min(jnp.where(e2 == v2, sub, far), axis=0, keepdims=True)
    ex = jnp.exp(v2 - v1)
    w1 = 1.0 / (1.0 + ex)
    w2 = ex * w1
    sub8 = lax.broadcasted_iota(jnp.int32, (8, shape[1]), 0).astype(F32)
    gate4 = jnp.where(sub8 == i1 - lo, p_sel * w1, jnp.where(sub8 == i2 - lo, p_sel * w2, 0.0))
    return gidx, gate4


def _moe_route_kernel(x_ref, n2_ref, wr_ref, br_ref, ts_ref, slot_ref, cnt_ref, tri_sc):
    rows = x_ref.shape[0]
    x = x_ref[...]
    t = _rms(x, n2_ref[...])
    t_hi = t.astype(BF16)
    t_lo = (t - t_hi.astype(F32)).astype(BF16)
    wr = wr_ref[...]
    wr_hi = wr.astype(BF16)
    wr_lo = (wr - wr_hi.astype(F32)).astype(BF16)
    nt_dims = (((1,), (1,)), ((), ()))
    logits = (lax.dot_general(wr_hi, t_hi, nt_dims, preferred_element_type=F32)
              + lax.dot_general(wr_hi, t_lo, nt_dims, preferred_element_type=F32)
              + lax.dot_general(wr_lo, t_hi, nt_dims, preferred_element_type=F32)) + br_ref[:, 0:1]
    gidx, gate4 = _route(logits)

    sub8 = lax.broadcasted_iota(jnp.int32, (8, rows), 0).astype(F32)
    onehot = jnp.where(sub8 == gidx, 1.0, 0.0)

    @pl.when(pl.program_id(0) == 0)
    def _build_tri():
        r_i = lax.broadcasted_iota(jnp.int32, (rows, rows), 0)
        c_i = lax.broadcasted_iota(jnp.int32, (rows, rows), 1)
        tri_sc[...] = jnp.where(r_i < c_i, 1.0, 0.0).astype(BF16)
    before = jnp.dot(onehot.astype(BF16), tri_sc[...], preferred_element_type=F32)
    rank = jnp.sum(before * onehot, axis=0, keepdims=True)
    counts = jnp.broadcast_to(jnp.sum(onehot, axis=1, keepdims=True), (8, ROUTER_LANES))

    units = jnp.ceil(counts * (1.0 / MOE_UNIT))
    first = MOE_UNIT * (pltpu.roll(units, 1, 0) + pltpu.roll(units, 2, 0) + pltpu.roll(units, 3, 0))
    slot_t = jnp.sum(onehot * first[:, 0:1], axis=0, keepdims=True) + rank
    s_sub = lax.broadcasted_iota(jnp.int32, (MOE_TILE_SLOTS, rows), 0).astype(F32)
    perm = jnp.where(s_sub == slot_t, 1.0, 0.0).astype(BF16)

    ts_ref[:, 0:D_MODEL] = jnp.dot(perm, t_hi, preferred_element_type=F32).astype(BF16)
    g_hi = gate4.astype(BF16).astype(F32)
    slot_rows = jnp.where(sub8 == 0.0, slot_t, 0.0)
    cols = jnp.concatenate(
        [g_hi, slot_rows, jnp.zeros((ROUTER_LANES // 2 - 16, rows), F32), gate4 - g_hi,
         jnp.zeros((ROUTER_LANES // 2 - 8, rows), F32)], axis=0)
    by_token = jnp.transpose(cols)
    lane = lax.broadcasted_iota(jnp.int32, (rows, ROUTER_LANES), 1)
    packed = jnp.where(lane == 8, 0.0, by_token).astype(BF16)
    ts_ref[:, D_MODEL:MOE_ROW_W] = jnp.dot(perm, packed, preferred_element_type=F32).astype(BF16)
    slot_ref[...] = jnp.broadcast_to(by_token[:, 8:9], (rows, ROUTER_LANES))
    cnt_ref[...] = counts


def _moe_expert_kernel(src_ref, grp_ref, nt_ref, *refs):
    units = refs[0:MOE_MTILE_UNITS]
    wg_ref, wu_ref, wd_ref, out_ref, wg_sc, wu_sc, wd_sc = refs[MOE_MTILE_UNITS:]
    del src_ref
    j = pl.program_id(0)
    active = j < nt_ref[0]

    @pl.when(jnp.logical_and(active, jnp.logical_or(j == 0, grp_ref[j] != grp_ref[jnp.maximum(j - 1, 0)])))
    def _cast_weights():
        for e in range(EXP_PER_GROUP):
            wg_sc[e] = wg_ref[e].astype(BF16)
            wu_sc[e] = wu_ref[e].astype(BF16)
            wd_sc[e] = wd_ref[e].astype(BF16)

    @pl.when(active)
    def _compute():
        tsg = jnp.concatenate([u[...] for u in units], axis=0)
        ts = tsg[:, 0:D_MODEL]
        gp = tsg[:, D_MODEL:MOE_ROW_W].astype(F32)
        gs = gp + pltpu.roll(gp, ROUTER_LANES // 2, 1)
        acc = None
        for e in range(EXP_PER_GROUP):
            hg = jnp.dot(ts, wg_sc[e], preferred_element_type=F32)
            hu = jnp.dot(ts, wu_sc[e], preferred_element_type=F32)
            act = (hg * _sigmoid(hg) * hu * gs[:, e:e + 1]).astype(BF16)
            o = jnp.dot(act, wd_sc[e], preferred_element_type=F32)
            acc = o if acc is None else acc + o
        out_ref[...] = acc.astype(BF16)

    @pl.when(jnp.logical_not(active))
    def _idle():
        out_ref[...] = jnp.zeros_like(out_ref)


def _moe_combine_kernel(final, bt_out, n_bm_tiles, pos_ref, x_ref, slot_ref, *refs):
    del pos_ref
    units = refs[0:MOE_TILE_UNITS]
    fn_ref = refs[MOE_TILE_UNITS]
    outs = refs[MOE_TILE_UNITS + 1:]
    rows = x_ref.shape[0]
    os_local = jnp.concatenate([u[...] for u in units], axis=0)
    s_lane = lax.broadcasted_iota(jnp.int32, (rows, MOE_TILE_SLOTS), 1).astype(F32)
    perm_t = jnp.where(s_lane == slot_ref[:, 0:1], 1.0, 0.0).astype(BF16)
    y = x_ref[...] + jnp.dot(perm_t, os_local, preferred_element_type=F32)
    if final:
        y = _rms(y, fn_ref[...])
    if bt_out:
        o_bm, o_rest = outs
        i = pl.program_id(0)

        @pl.when(i < n_bm_tiles)
        def _batch_major():
            for t in range(rows // bt_out):
                o_bm[:, t, :] = y[t * bt_out:(t + 1) * bt_out, :]

        @pl.when(i >= n_bm_tiles)
        def _rest():
            o_rest[...] = y
    else:
        (o_ref,) = outs
        o_ref[...] = y


def _moe_plan(counts, n_tiles, n_mtiles):
    i32 = jnp.int32
    g = jnp.arange(N_GROUPS, dtype=i32)
    c = jnp.arange(n_tiles, dtype=i32)
    earlier_g = (g[:, None] < g[None, :]).astype(i32)
    earlier_c = (c[:, None] < c[None, :]).astype(i32)
    units = (counts + (MOE_UNIT - 1)) // MOE_UNIT
    loc_off = jnp.sum(units[:, :, None] * earlier_g[None], axis=1)
    before_c = jnp.sum(units[:, None, :] * earlier_c[:, :, None], axis=0)
    grp_tiles = (jnp.sum(units, axis=0) + (MOE_MTILE_UNITS - 1)) // MOE_MTILE_UNITS
    tile_start = jnp.sum(grp_tiles[:, None] * earlier_g, axis=0)
    grp_end = tile_start + grp_tiles
    u = jnp.arange(MOE_TILE_UNITS, dtype=i32)
    g_of = jnp.sum((u[None, :, None] >= (loc_off + units)[:, None, :]).astype(i32), axis=-1)
    valid = g_of < N_GROUPS
    pick = (g_of[:, :, None] == g[None, None, :]).astype(i32)
    base = MOE_MTILE_UNITS * tile_start[None, :] + before_c - loc_off
    pos = jnp.sum(pick * base[:, None, :], axis=-1) + u[None, :]
    pos = jnp.where(valid, pos, 0).astype(i32)
    local_id = c[:, None] * MOE_TILE_UNITS + u[None, :]
    n_pos = n_mtiles * MOE_MTILE_UNITS
    p = jnp.arange(n_pos, dtype=i32)
    hit = jnp.logical_and(valid.reshape(-1)[None, :], pos.reshape(-1)[None, :] == p[:, None])
    src = jnp.sum(jnp.where(hit, local_id.reshape(-1)[None, :], 0), axis=-1).astype(i32)
    j = jnp.arange(n_mtiles, dtype=i32)
    grp = jnp.minimum(jnp.sum((j[:, None] >= grp_end[None, :]).astype(i32), axis=-1), N_GROUPS - 1)
    n_used = jnp.sum(grp_tiles, keepdims=True)
    return src, grp.astype(i32), n_used.astype(i32), pos.reshape(-1)


def _moe(x1, w, l, final, bt_out=0, n_bm_tiles=0):
    n_rows = x1.shape[0]
    rows = CHUNK_ROWS
    assert n_rows % rows == 0
    n_tiles = n_rows // rows
    n_mtiles = -(-(n_tiles * MOE_TILE_UNITS) // MOE_MTILE_UNITS) + N_GROUPS
    mrows = MOE_MTILE_UNITS * MOE_UNIT
    cparams = pltpu.CompilerParams(dimension_semantics=("arbitrary",), vmem_limit_bytes=VMEM_LIMIT_BYTES)

    row_spec = pl.BlockSpec((rows, D_MODEL), lambda i: (i, 0))
    def route_body(x_ref, n2_ref, *rest):
        _moe_route_kernel(x_ref, n2_ref.at[pl.ds(l, 1)], *rest)
    ts, slot, cnt = pl.pallas_call(
        route_body,
        grid=(n_tiles,),
        in_specs=[row_spec, _const_spec(w["norm2"].shape), _layer_spec(w["moe_w_r"], l),
                  _layer_spec(w["moe_b_r"], l)],
        out_specs=[pl.BlockSpec((MOE_TILE_SLOTS, MOE_ROW_W), lambda i: (i, 0)),
                   pl.BlockSpec((rows, ROUTER_LANES), lambda i: (i, 0)),
                   pl.BlockSpec((8, ROUTER_LANES), lambda i: (i, 0))],
        out_shape=[jax.ShapeDtypeStruct((n_tiles * MOE_TILE_SLOTS, MOE_ROW_W), BF16),
                   jax.ShapeDtypeStruct((n_rows, ROUTER_LANES), F32),
                   jax.ShapeDtypeStruct((n_tiles * 8, ROUTER_LANES), F32)],
        scratch_shapes=[pltpu.VMEM((rows, rows), BF16)],
        compiler_params=cparams,
        name="moe_route",
    )(x1, w["norm2"], w["moe_w_r"], w["moe_b_r"])

    counts = cnt.reshape(n_tiles, 8, ROUTER_LANES)[:, :N_GROUPS, 0].astype(jnp.int32)
    src, grp, n_used, pos = _moe_plan(counts, n_tiles, n_mtiles)

    def unit_in(k):
        return pl.BlockSpec((MOE_UNIT, MOE_ROW_W), lambda j, src, grp, nt: (src[j * MOE_MTILE_UNITS + k], 0))

    def grp_w(shape):
        return pl.BlockSpec((None, EXP_PER_GROUP) + shape, lambda j, src, grp, nt: (l, grp[j], 0, 0))
    os_sorted = pl.pallas_call(
        _moe_expert_kernel,
        grid_spec=pltpu.PrefetchScalarGridSpec(
            num_scalar_prefetch=3,
            grid=(n_mtiles,),
            in_specs=[unit_in(k) for k in range(MOE_MTILE_UNITS)]
            + [grp_w((D_MODEL, EXPERT_FF)), grp_w((D_MODEL, EXPERT_FF)), grp_w((EXPERT_FF, D_MODEL))],
            out_specs=pl.BlockSpec((mrows, D_MODEL), lambda j, src, grp, nt: (j, 0)),
            scratch_shapes=[pltpu.VMEM((EXP_PER_GROUP, D_MODEL, EXPERT_FF), BF16),
                            pltpu.VMEM((EXP_PER_GROUP, D_MODEL, EXPERT_FF), BF16),
                            pltpu.VMEM((EXP_PER_GROUP, EXPERT_FF, D_MODEL), BF16)],
        ),
        out_shape=jax.ShapeDtypeStruct((n_mtiles * mrows, D_MODEL), BF16),
        compiler_params=cparams,
        name="moe_experts",
    )(src, grp, n_used, *([ts] * MOE_MTILE_UNITS), w["moe_w_gate"], w["moe_w_up"], w["moe_w_down"])

    def unit_out(u):
        return pl.BlockSpec((MOE_UNIT, D_MODEL), lambda i, pos: (pos[i * MOE_TILE_UNITS + u], 0))
    if bt_out:
        tc = rows // bt_out
        last = n_bm_tiles - 1
        out_spec = [pl.BlockSpec((bt_out, tc, D_MODEL), lambda i, pos: (0, jnp.minimum(i, last), 0)),
                    pl.BlockSpec((rows, D_MODEL), lambda i, pos: (jnp.maximum(i - n_bm_tiles, 0), 0))]
        out_shape = [jax.ShapeDtypeStruct((bt_out, n_bm_tiles * tc, D_MODEL), F32),
                     jax.ShapeDtypeStruct(((n_tiles - n_bm_tiles) * rows, D_MODEL), F32)]
    else:
        out_spec = pl.BlockSpec((rows, D_MODEL), lambda i, pos: (i, 0))
        out_shape = jax.ShapeDtypeStruct((n_rows, D_MODEL), F32)
    return pl.pallas_call(
        functools.partial(_moe_combine_kernel, final, bt_out, n_bm_tiles),
        grid_spec=pltpu.PrefetchScalarGridSpec(
            num_scalar_prefetch=1,
            grid=(n_tiles,),
            in_specs=[pl.BlockSpec((rows, D_MODEL), lambda i, pos: (i, 0)),
                      pl.BlockSpec((rows, ROUTER_LANES), lambda i, pos: (i, 0))]
            + [unit_out(u) for u in range(MOE_TILE_UNITS)]
            + [pl.BlockSpec(w["final_norm"].shape, lambda i, pos: (0, 0))],
            out_specs=out_spec,
        ),
        out_shape=out_shape,
        compiler_params=cparams,
        name="moe_combine",
    )(pos, x1, slot, *([os_sorted] * MOE_TILE_UNITS), w["final_norm"])


def _block_diag(blocks):
    L, n, r, c = blocks.shape
    eye = jnp.eye(n, dtype=blocks.dtype)
    return (blocks[:, :, :, None, :] * eye[None, :, None, :, None]).reshape(L, n * r, n * c)


def _to_tm(a):
    a = jnp.swapaxes(a, 0, 1)
    return a.reshape((a.shape[0] * a.shape[1],) + a.shape[2:])


def _from_tm(a, b):
    return jnp.swapaxes(a.reshape((a.shape[0] // b, b) + a.shape[1:]), 0, 1)


def kernel(x_prompt, x_sample, state_rglru_h, state_rglru_conv, state_conf_conv, state_s5_re, state_s5_im, norm1, w_in, rg_conv_w, rg_conv_b, rg_wa, rg_ba, rg_wx, rg_bx, rg_lambda, cf_conv_w, cf_conv_b, cf_ln_g, cf_ln_b, s5_a_re, s5_a_im, s5_log_dt, s5_b_re, s5_b_im, s5_c_re, s5_c_im, s5_d, s5_w_glu, s5_b_glu, gn_rg, gn_cf, gn_s5, w_out, norm2, moe_w_grp, moe_b_grp, moe_w_exp, moe_b_exp, moe_w_gate, moe_w_up, moe_w_down, final_norm):
    L = w_in.shape[0]
    B, T, _ = x_prompt.shape
    DB, DT, _ = x_sample.shape

    ab_re, ab_im, bb_re, bb_im = _s5_prep(s5_a_re, s5_a_im, s5_log_dt,
                                          jnp.swapaxes(s5_b_re, 2, 3), jnp.swapaxes(s5_b_im, 2, 3))

    def row(a):
        return a.reshape(L, -1)

    col = jnp.arange(P_IN)
    in_scale = jnp.where((col >= 2 * D_RNN + CF_W) & (col < 2 * D_RNN + 2 * CF_W), 0.5, 1.0).astype(F32)
    big = dict(
        w_in=w_in * in_scale,
        rg_wg=0.5 * jnp.concatenate([_block_diag(rg_wa), _block_diag(rg_wx)], axis=-1),
        s5_bb_re=_block_diag(bb_re), s5_bb_im=_block_diag(bb_im),
        s5_c_re=_block_diag(jnp.swapaxes(s5_c_re, 2, 3)),
        s5_c_im=_block_diag(jnp.swapaxes(s5_c_im, 2, 3)),
        s5_w_glu=0.5 * s5_w_glu, w_out=w_out)
    assert set(big) == set(MATMUL_WEIGHTS)
    W = {k: v.astype(BF16) for k, v in big.items()}
    W_lo = {k: (v[:1] - W[k][:1].astype(F32)).astype(BF16) for k, v in big.items()}
    W.update(
        norm1=row(norm1), rg_conv_w=rg_conv_w, rg_conv_b=row(rg_conv_b),
        rg_bg=0.5 * jnp.concatenate([row(rg_ba), row(rg_bx)], axis=-1),
        rg_lambda=row(rg_lambda),
        cf_conv_w=cf_conv_w, cf_conv_b=row(cf_conv_b), cf_ln_g=row(cf_ln_g), cf_ln_b=row(cf_ln_b),
        s5_ab_re=row(ab_re), s5_ab_im=row(ab_im),
        s5_d=row(s5_d), s5_b_glu=0.5 * row(s5_b_glu),
        gn_rg=row(gn_rg), gn_cf=row(gn_cf), gn_s5=row(gn_s5),
        norm2=row(norm2),
        moe_w_r=jnp.swapaxes(jnp.concatenate(
            [moe_w_grp, moe_w_exp.reshape(L, D_MODEL, N_EXPERTS),
             jnp.zeros((L, D_MODEL, ROUTER_ROWS - N_GROUPS - N_EXPERTS), F32)], axis=-1), 1, 2),
        moe_b_r=jnp.broadcast_to(jnp.concatenate(
            [moe_b_grp, moe_b_exp.reshape(L, N_EXPERTS),
             jnp.zeros((L, ROUTER_ROWS - N_GROUPS - N_EXPERTS), F32)], axis=-1)[:, :, None],
            (L, ROUTER_ROWS, ROUTER_LANES)),
        moe_w_gate=moe_w_gate, moe_w_up=moe_w_up, moe_w_down=moe_w_down,
    )

    tc_p = CHUNK_ROWS // B
    n_p = (B * T) // CHUNK_ROWS
    n_s = (DB * DT) // CHUNK_ROWS
    assert n_p * CHUNK_ROWS == B * T and n_s == 1 and DB * DT == CHUNK_ROWS
    all_rows = (n_p + n_s) * CHUNK_ROWS
    x_p, x_s, s_blk0 = x_prompt, _to_tm(x_sample), 0
    p_states, s_states = [], []
    W["final_norm"] = final_norm.reshape(1, D_MODEL)
    st_p = (jnp.zeros((B, D_RNN), F32), jnp.zeros(((RG_CONV - 1) * B, D_RNN), F32),
            jnp.zeros(((CF_CONV - 1) * B, CF_W), F32), jnp.zeros((B, S5_N), F32),
            jnp.zeros((B, S5_N), F32))
    for l in range(L):
        st_s = (state_rglru_h[l], _to_tm(state_rglru_conv[l]), _to_tm(state_conf_conv[l]),
                state_s5_re[l].reshape(DB, S5_N), state_s5_im[l].reshape(DB, S5_N))
        if l == 0:
            n_lp = n_p - HP_TAIL_CHUNKS
            x1, *st_mid = _mixer(x_p, B, tc_p, n_lp, st_p, W, l, out_rows=all_rows)
            x1, *np_l = _mixer(x_p, B, tc_p, HP_TAIL_CHUNKS, tuple(st_mid), W, l, x_blk0=n_lp,
                               out=x1, out_blk0=n_lp, w_lo=W_lo)
        else:
            x1, *np_l = _mixer(x_p, B, tc_p, n_p, st_p, W, l, out_rows=all_rows)
        x1, *ns_l = _mixer(x_s, DB, DT, n_s, st_s, W, l, x_blk0=s_blk0, out=x1, out_blk0=n_p)
        p_states.append(np_l)
        s_states.append(ns_l)
        if l < L - 1:
            x_p = x_s = _moe(x1, W, l, False)
            s_blk0 = n_p
        else:
            y_prompt, y_s = _moe(x1, W, l, True, bt_out=B, n_bm_tiles=n_p)

    def assemble(states, b):
        h, rc, cf, sre, sim = [jnp.stack([s[j] for s in states]) for j in range(5)]
        rc = jnp.swapaxes(rc.reshape(L, RG_CONV - 1, b, D_RNN), 1, 2)
        cf = jnp.swapaxes(cf.reshape(L, CF_CONV - 1, b, CF_W), 1, 2)
        return (h, rc, cf, sre.reshape(L, b, S5_GROUPS, S5_STATE),
                sim.reshape(L, b, S5_GROUPS, S5_STATE))

    y_sample = _from_tm(y_s, DB)
    return (y_prompt, y_sample) + assemble(p_states, B) + assemble(s_states, DB)
```
